```python
import jax
import jax.numpy as jnp
from jax import lax
import numpy as np

D_MODEL = 2048
BATCH = 2
SEQ = 4096
DEPTH = 2
DEC_BATCH = 8
DEC_SEQ = 8
PAST_LEN = 16384
PAGE_SIZE = 128

A_GROUP_DIM = 128
A_GROUPS = D_MODEL // 256
A_WIDTH = A_GROUPS * A_GROUP_DIM
CHUNK = 128
HEAD_DIM = 64
N_HEADS = D_MODEL // 128
N_KV = 4
Q_PER_KV = N_HEADS // N_KV
B_WIDTH = N_HEADS * HEAD_DIM
KV_W = N_KV * HEAD_DIM
CMP_BLOCK = 32
SEL_BLOCK = 64
TOP_N = 16
WINDOW = 512
Q_BLOCK = 128
SCALE = HEAD_DIM ** -0.5
D_FF = 11 * D_MODEL // 4
CONV_W = 3
OFF_U = 0
OFF_V = OFF_U + A_WIDTH
OFF_Q = OFF_V + A_WIDTH
OFF_KV = OFF_Q + B_WIDTH
OFF_NSA_GATE = OFF_KV + 6 * KV_W
OFF_MERGE = OFF_NSA_GATE + 3 * N_HEADS
N_IN = OFF_MERGE + 2 * D_MODEL
EPS = 1e-6
NEG = -1e30
FORCE = 1e4
POOL_NUM = 5
POOL_DEN = 4

kernel_name = 'hybrid_gmlp_nsa_convffn_step'


def _rms(x, w):
    xf = x.astype(jnp.float32)
    y = xf * lax.rsqrt(jnp.mean(xf * xf, axis=-1, keepdims=True) + EPS)
    return (y * w.astype(jnp.float32)).astype(x.dtype)


def _ada(c, w, b):
    m = jax.nn.silu(c) @ w + b
    return jnp.split(m[:, None, :], 6, axis=-1)


def _modulate(x, g, shift, scale):
    return _rms(x, g) * (1 + scale) + shift


def _alibi_slopes():
    h = jnp.arange(1, N_HEADS + 1, dtype=jnp.float32)
    return (2.0 ** (-8.0 * h / N_HEADS)).reshape(N_KV, Q_PER_KV)


def _masked_softmax(s, mask, axis):
    s = jnp.where(mask, s, NEG)
    m = jnp.max(s, axis=axis, keepdims=True)
    p = jnp.where(mask, jnp.exp(s - m), 0.0)
    return p / jnp.maximum(jnp.sum(p, axis=axis, keepdims=True), 1e-30)


def _mixer_inputs(h, w_in, a_norm_w, q_norm_w, k_norm_w):
    b, t, _ = h.shape
    proj = h @ w_in
    u = jax.nn.gelu(proj[..., OFF_U:OFF_V])
    v = _rms(jax.nn.gelu(proj[..., OFF_V:OFF_Q]), a_norm_w)
    q = _rms(proj[..., OFF_Q:OFF_KV].reshape(b, t, N_KV, Q_PER_KV, HEAD_DIM), q_norm_w)
    kv = proj[..., OFF_KV:OFF_NSA_GATE].reshape(b, t, 6, N_KV, HEAD_DIM)
    k_cmp, v_cmp = kv[:, :, 0], kv[:, :, 1]
    k_sel, v_sel = _rms(kv[:, :, 2], k_norm_w), kv[:, :, 3]
    k_win, v_win = _rms(kv[:, :, 4], k_norm_w), kv[:, :, 5]
    nsa_gate = jax.nn.sigmoid(proj[..., OFF_NSA_GATE:OFF_MERGE]).reshape(b, t, 3, N_HEADS)
    merge_gate = jax.nn.sigmoid(proj[..., OFF_MERGE:]).reshape(b, t, 2, D_MODEL)
    return u, v, q, k_cmp, v_cmp, k_sel, v_sel, k_win, v_win, nsa_gate, merge_gate


def _chunk_mix(u, v, w_s, b_s, n_chunks, lc):
    b = u.shape[0]
    mask = jnp.tril(jnp.ones((lc, lc), dtype=bool))
    w = jnp.where(mask, w_s[:, :lc, :lc], 0.0)
    vg = v.reshape(b, n_chunks, lc, A_GROUPS, A_GROUP_DIM)
    s = jnp.einsum('gts,bnsgc->bntgc', w, vg) + b_s[:, :lc].T[:, :, None]
    return u * s.reshape(u.shape)


def _compress(rows, pe, w):
    b, L = rows.shape[:2]
    blk = rows.reshape(b, L // CMP_BLOCK, CMP_BLOCK, N_KV, HEAD_DIM) + pe[:, None, :]
    return jnp.einsum('bnigd,ide->bnge', blk, w)


def _cmp_keys(k_rows, v_rows, pe_k, w_k, pe_v, w_v, k_norm_w):
    return _rms(_compress(k_rows, pe_k, w_k), k_norm_w), _compress(v_rows, pe_v, w_v)


def _cmp_attend(q, q_pos, kc, vc, slopes):
    nc = kc.shape[1]
    end = (jnp.arange(nc) + 1) * CMP_BLOCK - 1
    dist = q_pos[:, None] - end[None, :]
    s = jnp.einsum('bqgrd,bngd->bgrqn', q, kc).astype(jnp.float32) * SCALE
    s = s - slopes[:, :, None, None] * dist.astype(jnp.float32)
    p = _masked_softmax(s, dist >= 0, -1)
    o = jnp.einsum('bgrqn,bngd->bqgrd', p.astype(vc.dtype), vc)
    return o, p


def _select_blocks(p, q_pos, n_sel):
    b, g, r, tq, nc = p.shape
    imp = jnp.sum(p, axis=2).reshape(b, g, tq, n_sel, SEL_BLOCK // CMP_BLOCK).sum(-1)
    j = jnp.arange(n_sel)
    cur = (q_pos // SEL_BLOCK)[:, None]
    forced = (j == 0) | (j == cur) | (j == cur - 1)
    score = jnp.where(j <= cur, imp + FORCE * forced, NEG)
    _, idx = lax.top_k(score, min(TOP_N, n_sel))
    return idx


def _sel_attend(q, q_pos, kb, vb, idx, slopes):
    s = jnp.einsum('bqgrd,bgqnkd->bgrqnk', q, kb).astype(jnp.float32) * SCALE
    pos = idx[..., None] * SEL_BLOCK + jnp.arange(SEL_BLOCK)
    dist = q_pos[None, None, :, None, None] - pos
    s = s - slopes[None, :, :, None, None, None] * dist[:, :, None].astype(jnp.float32)
    p = _masked_softmax(s, (dist >= 0)[:, :, None], (-2, -1))
    return jnp.einsum('bgrqnk,bgqnkd->bqgrd', p.astype(vb.dtype), vb)


def _win_attend(q, q_pos, kw, vw, k_pos, slopes):
    dist = q_pos[:, None] - k_pos[None, :]
    mask = (dist >= 0) & (dist < WINDOW) & (k_pos >= 0)[None, :]
    s = jnp.einsum('bqgrd,bkgd->bgrqk', q, kw).astype(jnp.float32) * SCALE
    s = s - slopes[:, :, None, None] * dist.astype(jnp.float32)
    p = _masked_softmax(s, mask, -1)
    return jnp.einsum('bgrqk,bkgd->bqgrd', p.astype(vw.dtype), vw)


def _nsa_prompt(q, k_cmp, v_cmp, k_sel, v_sel, k_win, v_win, cmpw, slopes):
    b, t = q.shape[:2]
    kc, vc = _cmp_keys(k_cmp, v_cmp, *cmpw)
    ns = t // SEL_BLOCK
    k_blk = k_sel.reshape(b, ns, SEL_BLOCK, N_KV, HEAD_DIM)
    v_blk = v_sel.reshape(b, ns, SEL_BLOCK, N_KV, HEAD_DIM)
    zpad = ((0, 0), (WINDOW, 0), (0, 0), (0, 0))
    kw_pad = jnp.pad(k_win, zpad)
    vw_pad = jnp.pad(v_win, zpad)
    b_ix = jnp.arange(b)[:, None, None, None]
    g_ix = jnp.arange(N_KV)[None, :, None, None]
    nqb = t // Q_BLOCK
    q_blocks = q.reshape(b, nqb, Q_BLOCK, N_KV, Q_PER_KV, HEAD_DIM).transpose(1, 0, 2, 3, 4, 5)

    def one_block(args):
        i, qb = args
        start = i * Q_BLOCK
        q_pos = start + jnp.arange(Q_BLOCK)
        o_c, p = _cmp_attend(qb, q_pos, kc, vc, slopes)
        idx = _select_blocks(p, q_pos, ns)
        o_s = _sel_attend(qb, q_pos, k_blk[b_ix, idx, :, g_ix, :], v_blk[b_ix, idx, :, g_ix, :], idx, slopes)
        kw = lax.dynamic_slice_in_dim(kw_pad, start, WINDOW + Q_BLOCK, axis=1)
        vw = lax.dynamic_slice_in_dim(vw_pad, start, WINDOW + Q_BLOCK, axis=1)
        k_pos = start - WINDOW + jnp.arange(WINDOW + Q_BLOCK)
        o_w = _win_attend(qb, q_pos, kw, vw, k_pos, slopes)
        return jnp.stack([o_c, o_s, o_w], 0)

    out = lax.map(one_block, (jnp.arange(nqb), q_blocks))
    return out.transpose(1, 2, 0, 3, 4, 5, 6).reshape(3, b, t, N_HEADS, HEAD_DIM)


def _nsa_sample(q, k_cmp, v_cmp, k_sel, v_sel, k_win, v_win, pool_kc, pool_vc, pool_ks, pool_vs,
                buf_k, buf_v, page_table, cmpw, slopes):
    bd, ds = q.shape[:2]
    n_pages = page_table.shape[1]
    past = n_pages * PAGE_SIZE
    new_pad = -(-ds // SEL_BLOCK) * SEL_BLOCK
    q_pos = past + jnp.arange(ds)

    def pad_new(r):
        return jnp.pad(r, ((0, 0), (0, new_pad - ds), (0, 0), (0, 0)))

    def full_rows(pool, r):
        past_rows = pool[page_table].reshape(bd, past, N_KV, HEAD_DIM)
        return jnp.concatenate([past_rows, pad_new(r)], axis=1)

    kc, vc = _cmp_keys(full_rows(pool_kc, k_cmp), full_rows(pool_vc, v_cmp), *cmpw)
    o_c, p = _cmp_attend(q, q_pos, kc, vc, slopes)
    ns = (past + new_pad) // SEL_BLOCK
    idx = _select_blocks(p, q_pos, ns)
    npb = past // SEL_BLOCK
    bpp = PAGE_SIZE // SEL_BLOCK
    b_ix = jnp.arange(bd)[:, None, None, None]
    g_ix = jnp.arange(N_KV)[None, :, None, None]
    jp = jnp.minimum(idx, npb - 1)
    phys = page_table[b_ix, jp // bpp] * bpp + jp % bpp
    jn = jnp.maximum(idx - npb, 0)
    in_past = (idx < npb)[..., None, None]

    def gather_sel(pool, r):
        from_pool = pool.reshape(-1, SEL_BLOCK, N_KV, HEAD_DIM)[phys, :, g_ix, :]
        new_blk = pad_new(r).reshape(bd, new_pad // SEL_BLOCK, SEL_BLOCK, N_KV, HEAD_DIM)
        from_new = new_blk[b_ix, jn, :, g_ix, :]
        return jnp.where(in_past, from_pool, from_new)

    o_s = _sel_attend(q, q_pos, gather_sel(pool_ks, k_sel), gather_sel(pool_vs, v_sel), idx, slopes)
    wb = buf_k.shape[1]
    kw = jnp.concatenate([buf_k, k_win], axis=1)
    vw = jnp.concatenate([buf_v, v_win], axis=1)
    k_pos = past - wb + jnp.arange(wb + ds)
    o_w = _win_attend(q, q_pos, kw, vw, k_pos, slopes)
    keep = min(WINDOW, wb + ds)
    out = jnp.stack([o_c, o_s, o_w], 0).reshape(3, bd, ds, N_HEADS, HEAD_DIM)
    return out, kw[:, -keep:], vw[:, -keep:]


def _nsa_combine(out, nsa_gate):
    b, t = nsa_gate.shape[:2]
    return jnp.einsum('kbthd,btkh->bthd', out, nsa_gate).reshape(b, t, B_WIDTH)


def _merge(o_a, o_b, gates, w_branch, w_out):
    pa = o_a @ w_branch[:A_WIDTH]
    pb = o_b @ w_branch[A_WIDTH:]
    return (gates[:, :, 0] * pa + gates[:, :, 1] * pb) @ w_out


def _conv_ffn(h, prev, w_a, w_b, conv_w, conv_b, w_down):
    a = h @ w_a
    t = a.shape[1]
    ap = jnp.concatenate([prev, a], axis=1)
    conv = sum(ap[:, k:k + t] * conv_w[k] for k in range(CONV_W)) + conv_b
    y = (jax.nn.gelu(conv) * (h @ w_b)) @ w_down
    return y, ap[:, -(CONV_W - 1):]


def setup_inputs(seed: int = 0) -> dict:
    key = jax.random.key(seed)
    keys = iter(jax.random.split(key, 48))

    def nrm(shape, scale):
        return scale * jax.random.normal(next(keys), shape, jnp.float32)

    def gain(shape):
        return 1.0 + nrm(shape, 0.02)

    n_pages = PAST_LEN // PAGE_SIZE
    n_pool = (DEC_BATCH * n_pages * POOL_NUM) // POOL_DEN
    wb = min(WINDOW, PAST_LEN)
    pool = (DEPTH, n_pool, PAGE_SIZE, N_KV, HEAD_DIM)
    win = (DEPTH, DEC_BATCH, wb, N_KV, HEAD_DIM)
    page_table = jax.random.permutation(next(keys), n_pool)[: DEC_BATCH * n_pages]
    page_table = page_table.reshape(DEC_BATCH, n_pages).astype(jnp.int32)
    return {
        'x_prompt': nrm((BATCH, SEQ, D_MODEL), 1.0),
        'x_sample': nrm((DEC_BATCH, DEC_SEQ, D_MODEL), 1.0),
        'cache_k_cmp': nrm(pool, 1.0),
        'cache_v_cmp': nrm(pool, 1.0),
        'cache_k_sel': nrm(pool, 1.0),
        'cache_v_sel': nrm(pool, 1.0),
        'cache_k_win': nrm(win, 1.0),
        'cache_v_win': nrm(win, 1.0),
        'state_ffn_conv': nrm((DEPTH, DEC_BATCH, CONV_W - 1, D_FF), 1.0),
        'page_table': page_table,
        'c_prompt': nrm((BATCH, D_MODEL), 1.0),
        'c_sample': nrm((DEC_BATCH, D_MODEL), 1.0),
        'w_ada': nrm((DEPTH, D_MODEL, 6 * D_MODEL), 0.5 * D_MODEL ** -0.5),
        'b_ada': nrm((DEPTH, 6 * D_MODEL), 0.02),
        'norm1_w': gain((DEPTH, D_MODEL)),
        'norm2_w': gain((DEPTH, D_MODEL)),
        'w_in': nrm((DEPTH, D_MODEL, N_IN), D_MODEL ** -0.5),
        'a_norm_w': gain((DEPTH, A_WIDTH)),
        'a_spatial_w': nrm((DEPTH, A_GROUPS, CHUNK, CHUNK), CHUNK ** -0.5),
        'a_spatial_b': gain((DEPTH, A_GROUPS, CHUNK)),
        'q_norm_w': gain((DEPTH, HEAD_DIM)),
        'k_norm_w': gain((DEPTH, HEAD_DIM)),
        'cmp_pe_k': nrm((DEPTH, CMP_BLOCK, HEAD_DIM), 0.1),
        'cmp_w_k': nrm((DEPTH, CMP_BLOCK, HEAD_DIM, HEAD_DIM), (CMP_BLOCK * HEAD_DIM) ** -0.5),
        'cmp_pe_v': nrm((DEPTH, CMP_BLOCK, HEAD_DIM), 0.1),
        'cmp_w_v': nrm((DEPTH, CMP_BLOCK, HEAD_DIM, HEAD_DIM), (CMP_BLOCK * HEAD_DIM) ** -0.5),
        'w_branch': nrm((DEPTH, A_WIDTH + B_WIDTH, D_MODEL), A_WIDTH ** -0.5),
        'w_out': nrm((DEPTH, D_MODEL, D_MODEL), D_MODEL ** -0.5),
        'ffn_w_a': nrm((DEPTH, D_MODEL, D_FF), D_MODEL ** -0.5),
        'ffn_w_b': nrm((DEPTH, D_MODEL, D_FF), D_MODEL ** -0.5),
        'ffn_conv_w': nrm((DEPTH, CONV_W, D_FF), 0.5),
        'ffn_conv_b': nrm((DEPTH, D_FF), 0.02),
        'ffn_w_down': nrm((DEPTH, D_FF, D_MODEL), D_FF ** -0.5),
    }


def reference(x_prompt, x_sample, cache_k_cmp, cache_v_cmp, cache_k_sel, cache_v_sel, cache_k_win,
              cache_v_win, state_ffn_conv, page_table, c_prompt, c_sample, w_ada, b_ada, norm1_w, norm2_w,
              w_in, a_norm_w, a_spatial_w, a_spatial_b, q_norm_w, k_norm_w, cmp_pe_k, cmp_w_k, cmp_pe_v,
              cmp_w_v, w_branch, w_out, ffn_w_a, ffn_w_b, ffn_conv_w, ffn_conv_b, ffn_w_down):
    slopes = _alibi_slopes()
    names = ['p_k_cmp', 'p_v_cmp', 'p_k_sel', 'p_v_sel', 'p_k_win', 'p_v_win', 'p_ffn_conv',
             's_k_cmp', 's_v_cmp', 's_k_sel', 's_v_sel', 's_k_win', 's_v_win', 's_chunk_v', 's_ffn_conv']
    st = {n: [] for n in names}
    xp, xs = x_prompt, x_sample
    t = xp.shape[1]
    ds = xs.shape[1]
    for l in range(DEPTH):
        cmpw = (cmp_pe_k[l], cmp_w_k[l], cmp_pe_v[l], cmp_w_v[l], k_norm_w[l])
        ffn = (ffn_w_a[l], ffn_w_b[l], ffn_conv_w[l], ffn_conv_b[l], ffn_w_down[l])
        mp = _ada(c_prompt, w_ada[l], b_ada[l])
        h = _modulate(xp, norm1_w[l], mp[0], mp[1])
        u, v, q, kc, vc, ks, vs, kw, vw, ng, mg = _mixer_inputs(h, w_in[l], a_norm_w[l], q_norm_w[l], k_norm_w[l])
        o_a = _chunk_mix(u, v, a_spatial_w[l], a_spatial_b[l], t // CHUNK, CHUNK)
        o_b = _nsa_combine(_nsa_prompt(q, kc, vc, ks, vs, kw, vw, cmpw, slopes), ng)
        xp = xp + mp[2] * _merge(o_a, o_b, mg, w_branch[l], w_out[l])
        h = _modulate(xp, norm2_w[l], mp[3], mp[4])
        f, conv_rows = _conv_ffn(h, jnp.zeros((xp.shape[0], CONV_W - 1, D_FF), h.dtype), *ffn)
        xp = xp + mp[5] * f
        wp = min(WINDOW, t)
        st['p_k_cmp'].append(kc)
        st['p_v_cmp'].append(vc)
        st['p_k_sel'].append(ks)
        st['p_v_sel'].append(vs)
        st['p_k_win'].append(kw[:, -wp:])
        st['p_v_win'].append(vw[:, -wp:])
        st['p_ffn_conv'].append(conv_rows)
        ms = _ada(c_sample, w_ada[l], b_ada[l])
        h = _modulate(xs, norm1_w[l], ms[0], ms[1])
        u, v, q, kc, vc, ks, vs, kw, vw, ng, mg = _mixer_inputs(h, w_in[l], a_norm_w[l], q_norm_w[l], k_norm_w[l])
        o_a = _chunk_mix(u, v, a_spatial_w[l], a_spatial_b[l], 1, ds)
        out, new_bk, new_bv = _nsa_sample(q, kc, vc, ks, vs, kw, vw, cache_k_cmp[l], cache_v_cmp[l],
                                          cache_k_sel[l], cache_v_sel[l], cache_k_win[l], cache_v_win[l],
                                          page_table, cmpw, slopes)
        o_b = _nsa_combine(out, ng)
        xs = xs + ms[2] * _merge(o_a, o_b, mg, w_branch[l], w_out[l])
        h = _modulate(xs, norm2_w[l], ms[3], ms[4])
        f, conv_rows = _conv_ffn(h, state_ffn_conv[l].astype(h.dtype), *ffn)
        xs = xs + ms[5] * f
        st['s_k_cmp'].append(kc)
        st['s_v_cmp'].append(vc)
        st['s_k_sel'].append(ks)
        st['s_v_sel'].append(vs)
        st['s_k_win'].append(new_bk)
        st['s_v_win'].append(new_bv)
        st['s_chunk_v'].append(v)
        st['s_ffn_conv'].append(conv_rows)
    st = {n: jnp.stack(a, axis=0) for n, a in st.items()}
    return (xp, xs, st['p_k_cmp'], st['p_v_cmp'], st['p_k_sel'], st['p_v_sel'], st['p_k_win'], st['p_v_win'],
            st['p_ffn_conv'], st['s_k_cmp'], st['s_v_cmp'], st['s_k_sel'], st['s_v_sel'], st['s_k_win'],
            st['s_v_win'], st['s_chunk_v'], st['s_ffn_conv'])
```

```python
import functools

import jax
import jax.numpy as jnp
from jax import lax
from jax.experimental import pallas as pl
from jax.experimental.pallas import tpu as pltpu

F32 = jnp.float32
BF16 = jnp.bfloat16

HEAD_DIM = 64
N_KV = 4
Q_PER_KV = 4
N_HEADS = N_KV * Q_PER_KV
KV_W = N_KV * HEAD_DIM
A_GROUPS = 8
A_GROUP_DIM = 128
A_WIDTH = A_GROUPS * A_GROUP_DIM
B_WIDTH = N_HEADS * HEAD_DIM
CHUNK = 128
CMP_BLOCK = 32
SEL_BLOCK = 64
TOP_N = 16
WINDOW = 512
Q_BLOCK = 128
PAGE_SIZE = 128
CONV_W = 3
SCALE = HEAD_DIM ** -0.5
EPS = 1e-6
NEG = -1e30
FORCE = 1e4

LANES = 128
SUBLANES = 8
MIB = 1024 * 1024

TN = 512
COL_U = 0
COL_V = A_WIDTH
COL_Q = 2 * A_WIDTH
COL_KV = COL_Q + B_WIDTH
N_MAIN = COL_KV + 6 * KV_W
N_MAIN_BLOCKS = N_MAIN // TN
SEL_PAGES = 32


def _cparams(sem, vmem_mib):
    return pltpu.CompilerParams(dimension_semantics=sem, vmem_limit_bytes=vmem_mib * MIB)


def _group_rms(x, gmat, w):
    y = x * x
    hi = y.astype(BF16)
    lo = (y - hi.astype(F32)).astype(BF16)
    ms = jnp.dot(hi, gmat, preferred_element_type=F32) + jnp.dot(lo, gmat, preferred_element_type=F32)
    return x * lax.rsqrt(ms + EPS) * w


def _group_mean_matrix(n):
    r = jnp.arange(n)
    return jnp.where((r[:, None] // HEAD_DIM) == (r[None, :] // HEAD_DIM), 1.0 / HEAD_DIM, 0.0).astype(BF16)


def _ada_kernel(c_ref, w_ref, b_ref, o_ref):
    c = c_ref[...]
    s = (c * jax.nn.sigmoid(c)).astype(BF16)
    o_ref[...] = jnp.dot(s, w_ref[...].astype(BF16), preferred_element_type=F32) + b_ref[...]


def _ada(c_all, w_ada, b_ada):
    depth, d, n6 = w_ada.shape
    rows = c_all.shape[0]
    tn = 1024
    return pl.pallas_call(
        _ada_kernel,
        grid=(depth, n6 // tn),
        in_specs=[
            pl.BlockSpec((rows, d), lambda l, j: (0, 0)),
            pl.BlockSpec((None, d, tn), lambda l, j: (l, 0, j)),
            pl.BlockSpec((None, 1, tn), lambda l, j: (l, 0, j)),
        ],
        out_specs=pl.BlockSpec((None, rows, tn), lambda l, j: (l, 0, j)),
        out_shape=jax.ShapeDtypeStruct((depth, rows, n6), F32),
        compiler_params=_cparams(("arbitrary", "arbitrary"), 40),
        name="ada",
    )(c_all, w_ada, b_ada.reshape(depth, 1, n6))


def _modulate_kernel(x_ref, g_ref, sh_ref, sc_ref, o_ref):
    x = x_ref[...]
    y = x * lax.rsqrt(jnp.mean(x * x, axis=-1, keepdims=True) + EPS)
    y = y * g_ref[...]
    o_ref[...] = (y * (1.0 + sc_ref[...]) + sh_ref[...]).astype(BF16)


def _modulate(x2, g, shift, scale, t, l):
    m, d = x2.shape
    nb = m // t
    tm = min(t, 512)
    per = t // tm
    return pl.pallas_call(
        _modulate_kernel,
        grid=(m // tm,),
        in_specs=[
            pl.BlockSpec((tm, d), lambda i: (i, 0)),
            pl.BlockSpec((None, 1, d), lambda i: (l, 0, 0)),
            pl.BlockSpec((None, 1, d), lambda i: (i // per, 0, 0)),
            pl.BlockSpec((None, 1, d), lambda i: (i // per, 0, 0)),
        ],
        out_specs=pl.BlockSpec((tm, d), lambda i: (i, 0)),
        out_shape=jax.ShapeDtypeStruct((m, d), BF16),
        compiler_params=_cparams(("arbitrary",), 32),
        name="modulate",
    )(x2, g.reshape(g.shape[0], 1, d), shift.reshape(nb, 1, d), scale.reshape(nb, 1, d))


def _inproj_kernel(h_ref, w1_ref, w2_ref, qn_ref, kn_ref, gm_ref, o_ref, wbf_ref):
    j = pl.program_id(0)
    i = pl.program_id(1)

    @pl.when((i == 0) & (j < N_MAIN_BLOCKS))
    def _():
        wbf_ref[...] = w1_ref[...].astype(BF16)

    @pl.when((i == 0) & (j >= N_MAIN_BLOCKS))
    def _():
        wbf_ref[...] = w2_ref[...].astype(BF16)

    acc = jnp.dot(h_ref[...], wbf_ref[...], preferred_element_type=F32)

    @pl.when(j < 4)
    def _():
        o_ref[...] = jax.nn.gelu(acc)

    @pl.when((j == 4) | (j == 5))
    def _():
        o_ref[...] = _group_rms(acc, gm_ref[...], qn_ref[...])

    @pl.when(j == 6)
    def _():
        o_ref[...] = acc

    @pl.when((j == 7) | (j == 8))
    def _():
        lane = lax.broadcasted_iota(jnp.int32, acc.shape, 1)
        o_ref[...] = jnp.where(lane < KV_W, _group_rms(acc, gm_ref[...], kn_ref[...]), acc)

    @pl.when(j >= N_MAIN_BLOCKS)
    def _():
        o_ref[...] = jax.nn.sigmoid(acc)


def _inproj(h, w_in, w_gm, qn, kn, l, tm):
    m, d = h.shape
    n_gm = w_gm.shape[1] // TN
    nj = N_MAIN_BLOCKS + n_gm
    gm = _group_mean_matrix(TN)
    return pl.pallas_call(
        _inproj_kernel,
        grid=(nj, m // tm),
        in_specs=[
            pl.BlockSpec((tm, d), lambda j, i: (i, 0)),
            pl.BlockSpec((None, d, TN), lambda j, i: (l, 0, jnp.minimum(j, N_MAIN_BLOCKS - 1))),
            pl.BlockSpec((d, TN), lambda j, i: (0, jnp.maximum(j - N_MAIN_BLOCKS, 0))),
            pl.BlockSpec((1, TN), lambda j, i: (0, 0)),
            pl.BlockSpec((1, TN), lambda j, i: (0, 0)),
            pl.BlockSpec((TN, TN), lambda j, i: (0, 0)),
        ],
        out_specs=pl.BlockSpec((tm, TN), lambda j, i: (i, j)),
        out_shape=jax.ShapeDtypeStruct((m, nj * TN), F32),
        scratch_shapes=[pltpu.VMEM((d, TN), BF16)],
        compiler_params=_cparams(("arbitrary", "arbitrary"), 48),
        name="inproj",
    )(h, w_in, w_gm, qn, kn, gm)


def _chunkmix_kernel(u_ref, gv_ref, w_ref, bt_ref, an_ref, oa_ref, v_ref, *, lc, causal_block):
    gv = gv_ref[...]
    v = gv * lax.rsqrt(jnp.mean(gv * gv, axis=-1, keepdims=True) + EPS) * an_ref[...]
    v_ref[...] = v
    row = lax.broadcasted_iota(jnp.int32, (lc, lc), 0)
    col = lax.broadcasted_iota(jnp.int32, (lc, lc), 1)
    shift = causal_block.bit_length() - 1
    mask = (col <= row) & ((row >> shift) == (col >> shift))
    vb = v.astype(BF16)
    for g in range(A_GROUPS):
        sl = slice(g * A_GROUP_DIM, (g + 1) * A_GROUP_DIM)
        wg = jnp.where(mask, w_ref[g], 0.0).astype(BF16)
        s = jnp.dot(wg, vb[:, sl], preferred_element_type=F32) + bt_ref[:, g:g + 1]
        oa_ref[:, sl] = (u_ref[:, sl] * s).astype(BF16)


def _chunkmix(proj, w_s, b_t, a_norm, lc, causal_block):
    m = proj.shape[0]
    return pl.pallas_call(
        functools.partial(_chunkmix_kernel, lc=lc, causal_block=causal_block),
        grid=(m // lc,),
        in_specs=[
            pl.BlockSpec((lc, A_WIDTH), lambda c: (c, 0)),
            pl.BlockSpec((lc, A_WIDTH), lambda c: (c, 1)),
            pl.BlockSpec((A_GROUPS, lc, lc), lambda c: (0, 0, 0)),
            pl.BlockSpec((lc, A_GROUPS), lambda c: (0, 0)),
            pl.BlockSpec((1, A_WIDTH), lambda c: (0, 0)),
        ],
        out_specs=[
            pl.BlockSpec((lc, A_WIDTH), lambda c: (c, 0)),
            pl.BlockSpec((lc, A_WIDTH), lambda c: (c, 0)),
        ],
        out_shape=[jax.ShapeDtypeStruct((m, A_WIDTH), BF16), jax.ShapeDtypeStruct((m, A_WIDTH), F32)],
        compiler_params=_cparams(("arbitrary",), 32),
        name="chunkmix",
    )(proj, proj, w_s, b_t, a_norm)


def _compress_rows(half_refs, n_blk, bd_ref, pe_ref):
    halves = []
    for ref in half_refs:
        acc = jnp.zeros((n_blk, LANES), F32)
        for i in range(CMP_BLOCK):
            rows = ref[pl.ds(i, n_blk, stride=CMP_BLOCK), :] + pe_ref[i:i + 1, :]
            acc = acc + jnp.dot(rows.astype(BF16), bd_ref[i], preferred_element_type=F32)
        halves.append(acc)
    return jnp.concatenate(halves, axis=1)


def _compress_kernel(k0_ref, k1_ref, v0_ref, v1_ref, bdk_ref, bdv_ref, pek_ref, pev_ref, knw_ref, gm_ref,
                     kc_ref, vc_ref, *, nc):
    kc_ref[...] = _group_rms(_compress_rows((k0_ref, k1_ref), nc, bdk_ref, pek_ref), gm_ref[...], knw_ref[...])
    vc_ref[...] = _compress_rows((v0_ref, v1_ref), nc, bdv_ref, pev_ref)


def _compress(src3, kblk, vblk, cmp_consts):
    b, t, _ = src3.shape
    nc = t // CMP_BLOCK
    bdk, bdv, pek, pev, knw, gm = cmp_consts
    const3 = lambda i: (0, 0, 0)
    const2 = lambda i: (0, 0)
    half = lambda blk, h: pl.BlockSpec((None, t, LANES), lambda i: (i, 0, 2 * blk + h))
    return pl.pallas_call(
        functools.partial(_compress_kernel, nc=nc),
        grid=(b,),
        in_specs=[
            half(kblk, 0), half(kblk, 1), half(vblk, 0), half(vblk, 1),
            pl.BlockSpec((CMP_BLOCK, LANES, LANES), const3),
            pl.BlockSpec((CMP_BLOCK, LANES, LANES), const3),
            pl.BlockSpec((CMP_BLOCK, LANES), const2),
            pl.BlockSpec((CMP_BLOCK, LANES), const2),
            pl.BlockSpec((1, KV_W), const2),
            pl.BlockSpec((KV_W, KV_W), const2),
        ],
        out_specs=[pl.BlockSpec((None, nc, KV_W), lambda i: (i, 0, 0))] * 2,
        out_shape=[jax.ShapeDtypeStruct((b, nc, KV_W), F32)] * 2,
        compiler_params=_cparams(("arbitrary",), 40),
        name="compress",
    )(src3, src3, src3, src3, bdk, bdv, pek, pev, knw, gm)


def _cmp_consts(cmp_pe_k, cmp_w_k, cmp_pe_v, cmp_w_v, k_norm_w):
    per = LANES // HEAD_DIM
    eye = jnp.eye(per, dtype=F32)

    def bd(w):
        return jnp.einsum('gh,ide->igdhe', eye, w).reshape(CMP_BLOCK, LANES, LANES).astype(BF16)

    return (bd(cmp_w_k), bd(cmp_w_v), jnp.tile(cmp_pe_k, (1, per)), jnp.tile(cmp_pe_v, (1, per)),
            jnp.tile(k_norm_w, N_KV)[None, :], _group_mean_matrix(KV_W))


def _pair_sum(psum, pair):
    hi = psum.astype(BF16)
    lo = (psum - hi.astype(F32)).astype(BF16)
    return jnp.dot(hi, pair, preferred_element_type=F32) + jnp.dot(lo, pair, preferred_element_type=F32)


def _top_mask(score, n_valid, top_n):
    rows, width = score.shape
    jj = lax.broadcasted_iota(jnp.int32, (rows, width), 1)
    rank = jnp.zeros((rows, width), F32)
    for k in range(n_valid):
        ck = score[:, k:k + 1]
        beats = (ck > score) | ((ck == score) & (jj > k))
        rank = rank + jnp.where(beats, 1.0, 0.0)
    return jnp.where((rank < top_n) & (jj < n_valid), 1.0, 0.0)


def _softmax_rows(s, mask):
    s = jnp.where(mask, s, NEG)
    m = jnp.max(s, axis=-1, keepdims=True)
    p = jnp.where(mask, jnp.exp(s - m), 0.0)
    return p / jnp.maximum(jnp.sum(p, axis=-1, keepdims=True), 1e-30)


def _nsa_prompt_kernel(slopes_ref, q_ref, kct_ref, vc_ref, kst_ref, vs_ref, kwt_ref, vw_ref, ng_ref, pair_ref,
                       o_ref, m_sc, l_sc, acc_sc, *, t, tk):
    g = pl.program_id(1)
    i = pl.program_id(2)
    nc = t // CMP_BLOCK
    ns = t // SEL_BLOCK
    rows = Q_PER_KV * Q_BLOCK
    start = i * Q_BLOCK
    q = (q_ref[...] * SCALE).astype(BF16).reshape(rows, HEAD_DIM)
    qpos = start + lax.broadcasted_iota(jnp.int32, (Q_BLOCK, 1), 0)
    slopes = [slopes_ref[g, r] for r in range(Q_PER_KV)]

    s = jnp.dot(q, kct_ref[...], preferred_element_type=F32)
    endc = (lax.broadcasted_iota(jnp.int32, (1, nc), 1) + 1) * CMP_BLOCK - 1
    dist_c = qpos - endc
    mask_c = dist_c >= 0
    dist_cf = dist_c.astype(F32)
    p_heads = []
    for r in range(Q_PER_KV):
        sr = s[r * Q_BLOCK:(r + 1) * Q_BLOCK] - slopes[r] * dist_cf
        p_heads.append(_softmax_rows(sr, mask_c))
    o_c = [jnp.dot(p.astype(BF16), vc_ref[...], preferred_element_type=F32) for p in p_heads]
    psum = ((p_heads[0] + p_heads[1]) + p_heads[2]) + p_heads[3]
    imp = _pair_sum(psum, pair_ref[...])

    jj = lax.broadcasted_iota(jnp.int32, (Q_BLOCK, ns), 1)
    cur = qpos >> 6
    forced = (jj == 0) | (jj == cur) | (jj == cur - 1)
    score = jnp.where(jj <= cur, imp + jnp.where(forced, FORCE, 0.0), NEG)
    sel = _top_mask(score, ns, min(TOP_N, ns)).astype(BF16)

    m_sc[...] = jnp.full(m_sc.shape, NEG, F32)
    l_sc[...] = jnp.zeros(l_sc.shape, F32)
    acc_sc[...] = jnp.zeros(acc_sc.shape, F32)
    n_tiles = (start + Q_BLOCK - 1) // tk + 1

    def sel_tile(ti, carry):
        k0 = pl.multiple_of(ti * tk, tk)
        st = jnp.dot(q, kst_ref[:, pl.ds(k0, tk)], preferred_element_type=F32)
        kpos = k0 + lax.broadcasted_iota(jnp.int32, (1, tk), 1)
        dist = qpos - kpos
        blk = lax.broadcasted_iota(jnp.int32, (ns, tk), 0)
        expand = jnp.where((kpos >> 6) == blk, 1.0, 0.0).astype(BF16)
        chosen = jnp.dot(sel, expand, preferred_element_type=F32)
        mask = (dist >= 0) & (chosen > 0.5)
        distf = dist.astype(F32)
        vt = vs_ref[pl.ds(k0, tk), :]
        for r in range(Q_PER_KV):
            rs = slice(r * Q_BLOCK, (r + 1) * Q_BLOCK)
            sr = jnp.where(mask, st[rs] - slopes[r] * distf, NEG)
            m_old = m_sc[rs]
            m_new = jnp.maximum(m_old, jnp.max(sr, axis=-1, keepdims=True))
            alpha = jnp.exp(m_old - m_new)
            p = jnp.where(mask, jnp.exp(sr - m_new), 0.0)
            l_sc[rs] = alpha * l_sc[rs] + jnp.sum(p, axis=-1, keepdims=True)
            acc_sc[rs] = alpha * acc_sc[rs] + jnp.dot(p.astype(BF16), vt, preferred_element_type=F32)
            m_sc[rs] = m_new
        return carry

    lax.fori_loop(0, n_tiles, sel_tile, 0)

    wk = WINDOW + Q_BLOCK
    k0w = pl.multiple_of(jnp.maximum(start - WINDOW, 0), Q_BLOCK)
    sw = jnp.dot(q, kwt_ref[:, pl.ds(k0w, wk)], preferred_element_type=F32)
    dist_w = qpos - (k0w + lax.broadcasted_iota(jnp.int32, (1, wk), 1))
    mask_w = (dist_w >= 0) & (dist_w < WINDOW)
    dist_wf = dist_w.astype(F32)
    vwt = vw_ref[pl.ds(k0w, wk), :]

    ng = ng_ref[...]
    outs = []
    for r in range(Q_PER_KV):
        rs = slice(r * Q_BLOCK, (r + 1) * Q_BLOCK)
        pw = _softmax_rows(sw[rs] - slopes[r] * dist_wf, mask_w)
        o_w = jnp.dot(pw.astype(BF16), vwt, preferred_element_type=F32)
        o_s = acc_sc[rs] / jnp.maximum(l_sc[rs], 1e-30)
        outs.append(ng[:, r:r + 1] * o_c[r] + ng[:, Q_PER_KV + r:Q_PER_KV + r + 1] * o_s
                    + ng[:, 2 * Q_PER_KV + r:2 * Q_PER_KV + r + 1] * o_w)
    o_ref[...] = jnp.concatenate(outs, axis=1).astype(BF16)


def _pair_matrix(nc, ns_pad):
    n = jnp.arange(nc)[:, None]
    j = jnp.arange(ns_pad)[None, :]
    return jnp.where((n // (SEL_BLOCK // CMP_BLOCK)) == j, 1.0, 0.0).astype(BF16)


def _nsa_prompt(slopes, q_t, kct, vc, kst, vs, kwt, vw, ng):
    b, _, _, t, _ = q_t.shape
    nc = t // CMP_BLOCK
    ns = t // SEL_BLOCK
    tk = min(512, t)
    rows = Q_PER_KV * Q_BLOCK
    kt_spec = lambda n: pl.BlockSpec((None, None, HEAD_DIM, n), lambda bi, g, i: (bi, g, 0, 0))
    v_spec = lambda n: pl.BlockSpec((None, None, n, HEAD_DIM), lambda bi, g, i: (bi, g, 0, 0))
    return pl.pallas_call(
        functools.partial(_nsa_prompt_kernel, t=t, tk=tk),
        grid=(b, N_KV, t // Q_BLOCK),
        in_specs=[
            pl.BlockSpec(memory_space=pltpu.SMEM),
            pl.BlockSpec((None, None, Q_PER_KV, Q_BLOCK, HEAD_DIM), lambda bi, g, i: (bi, g, 0, i, 0)),
            kt_spec(nc), v_spec(nc), kt_spec(t), v_spec(t), kt_spec(t), v_spec(t),
            pl.BlockSpec((None, None, Q_BLOCK, 3 * Q_PER_KV), lambda bi, g, i: (bi, g, i, 0)),
            pl.BlockSpec((nc, ns), lambda bi, g, i: (0, 0)),
        ],
        out_specs=pl.BlockSpec((None, Q_BLOCK, KV_W), lambda bi, g, i: (bi, i, g)),
        out_shape=jax.ShapeDtypeStruct((b, t, B_WIDTH), BF16),
        scratch_shapes=[pltpu.VMEM((rows, 1), F32), pltpu.VMEM((rows, 1), F32), pltpu.VMEM((rows, HEAD_DIM), F32)],
        compiler_params=_cparams(("arbitrary", "arbitrary", "arbitrary"), 48),
        name="nsa_prompt",
    )(slopes, q_t, kct, vc, kst, vs, kwt, vw, ng, _pair_matrix(nc, ns))


def _branch_kernel(oa_ref, ob_ref, w_ref, g0_ref, g1_ref, o_ref, wbf_ref):
    @pl.when(pl.program_id(1) == 0)
    def _():
        wbf_ref[...] = w_ref[...].astype(BF16)

    pa = jnp.dot(oa_ref[...], wbf_ref[:A_WIDTH, :], preferred_element_type=F32)
    pb = jnp.dot(ob_ref[...], wbf_ref[A_WIDTH:, :], preferred_element_type=F32)
    o_ref[...] = (g0_ref[...] * pa + g1_ref[...] * pb).astype(BF16)


def _branch(o_a, o_b, w_branch, proj, l, tm):
    m = o_a.shape[0]
    kb, d = w_branch.shape[1:]
    nj = d // TN
    return pl.pallas_call(
        _branch_kernel,
        grid=(nj, m // tm),
        in_specs=[
            pl.BlockSpec((tm, A_WIDTH), lambda j, i: (i, 0)),
            pl.BlockSpec((tm, B_WIDTH), lambda j, i: (i, 0)),
            pl.BlockSpec((None, kb, TN), lambda j, i: (l, 0, j)),
            pl.BlockSpec((tm, TN), lambda j, i: (i, N_MAIN_BLOCKS + j)),
            pl.BlockSpec((tm, TN), lambda j, i: (i, N_MAIN_BLOCKS + nj + j)),
        ],
        out_specs=pl.BlockSpec((tm, TN), lambda j, i: (i, j)),
        out_shape=jax.ShapeDtypeStruct((m, d), BF16),
        scratch_shapes=[pltpu.VMEM((kb, TN), BF16)],
        compiler_params=_cparams(("arbitrary", "arbitrary"), 48),
        name="branch",
    )(o_a, o_b, w_branch, proj, proj)


def _resid_kernel(a_ref, w_ref, x_ref, g_ref, o_ref, wbf_ref):
    @pl.when(pl.program_id(1) == 0)
    def _():
        wbf_ref[...] = w_ref[...].astype(BF16)

    o_ref[...] = x_ref[...] + g_ref[...] * jnp.dot(a_ref[...], wbf_ref[...], preferred_element_type=F32)


def _resid_matmul(a, w, x2, gate, l, tm, name):
    m, k = a.shape
    d = w.shape[2]
    if gate.shape[0] == m:
        gate_spec = pl.BlockSpec((tm, TN), lambda j, i: (i, j))
    else:
        nb = gate.shape[0]
        per = (m // nb) // tm
        gate = gate.reshape(nb, 1, d)
        gate_spec = pl.BlockSpec((None, 1, TN), lambda j, i: (i // per, 0, j))
    return pl.pallas_call(
        _resid_kernel,
        grid=(d // TN, m // tm),
        in_specs=[
            pl.BlockSpec((tm, k), lambda j, i: (i, 0)),
            pl.BlockSpec((None, k, TN), lambda j, i: (l, 0, j), pipeline_mode=pl.Buffered(1)),
            pl.BlockSpec((tm, TN), lambda j, i: (i, j)),
            gate_spec,
        ],
        out_specs=pl.BlockSpec((tm, TN), lambda j, i: (i, j)),
        out_shape=jax.ShapeDtypeStruct((m, d), F32),
        scratch_shapes=[pltpu.VMEM((k, TN), BF16)],
        compiler_params=_cparams(("arbitrary", "arbitrary"), 56),
        name=name,
    )(a, w, x2, gate)


def _ffn_up_kernel(h_ref, wa_ref, wb_ref, cw_ref, cb_ref, s0_ref, s1_ref, o_ref, tail_ref,
                   wa_bf, wb_bf, abuf, *, tm, seg, per_batch, keep):
    i = pl.program_id(1)

    @pl.when(i == 0)
    def _():
        wa_bf[...] = wa_ref[...].astype(BF16)
        wb_bf[...] = wb_ref[...].astype(BF16)

    h = h_ref[...]
    a = jnp.dot(h, wa_bf[...], preferred_element_type=F32)
    b = jnp.dot(h, wb_bf[...], preferred_element_type=F32)

    @pl.when(i % per_batch == 0)
    def _():
        abuf[0:SUBLANES, :] = jnp.zeros((SUBLANES, TN), F32)

    abuf[SUBLANES:SUBLANES + tm, :] = a
    a1 = abuf[SUBLANES - 1:SUBLANES - 1 + tm, :]
    a2 = abuf[SUBLANES - 2:SUBLANES - 2 + tm, :]
    if seg < tm:
        r = lax.broadcasted_iota(jnp.int32, (tm, 1), 0) & (seg - 1)
        a1 = jnp.where(r == 0, s1_ref[...], a1)
        a2 = jnp.where(r == 0, s0_ref[...], jnp.where(r == 1, s1_ref[...], a2))
    conv = a2 * cw_ref[0:1, :] + a1 * cw_ref[1:2, :] + a * cw_ref[2:3, :] + cb_ref[...]
    o_ref[...] = (jax.nn.gelu(conv) * b).astype(BF16)
    tail_ref[...] = a[tm - keep:, :]
    abuf[0:SUBLANES, :] = a[tm - SUBLANES:, :]


def _ffn_up(h2, w_a, w_b, conv_w, conv_b, s0, s1, t, l, tm):
    m, d = h2.shape
    dff = w_a.shape[2]
    per_batch = max(t // tm, 1)
    seg = min(t, tm)
    keep = SUBLANES if seg == tm else tm
    if s0 is None:
        s0 = s1 = jnp.zeros((SUBLANES, dff), F32)
        s_spec = pl.BlockSpec((SUBLANES, TN), lambda j, i: (0, j))
    else:
        s_spec = pl.BlockSpec((tm, TN), lambda j, i: (i, j))
    w_spec = pl.BlockSpec((None, d, TN), lambda j, i: (l, 0, j))
    return pl.pallas_call(
        functools.partial(_ffn_up_kernel, tm=tm, seg=seg, per_batch=per_batch, keep=keep),
        grid=(dff // TN, m // tm),
        in_specs=[
            pl.BlockSpec((tm, d), lambda j, i: (i, 0)),
            w_spec, w_spec,
            pl.BlockSpec((None, CONV_W, TN), lambda j, i: (l, 0, j)),
            pl.BlockSpec((None, 1, TN), lambda j, i: (l, 0, j)),
            s_spec, s_spec,
        ],
        out_specs=[
            pl.BlockSpec((tm, TN), lambda j, i: (i, j)),
            pl.BlockSpec((None, keep, TN), lambda j, i: (i, 0, j)),
        ],
        out_shape=[jax.ShapeDtypeStruct((m, dff), BF16), jax.ShapeDtypeStruct((m // tm, keep, dff), F32)],
        scratch_shapes=[pltpu.VMEM((d, TN), BF16), pltpu.VMEM((d, TN), BF16), pltpu.VMEM((tm + SUBLANES, TN), F32)],
        compiler_params=_cparams(("arbitrary", "arbitrary"), 56),
        name="ffn_up",
    )(h2, w_a, w_b, conv_w, conv_b.reshape(conv_b.shape[0], 1, dff), s0, s1)


def _gather_pages(pt_ref, b, c, pool_ref, l, buf_ref, sem, split_lanes):
    copies = []
    for p in range(SEL_PAGES):
        page = pt_ref[b, c * SEL_PAGES + p]
        dst_rows = pl.ds(p * PAGE_SIZE, PAGE_SIZE)
        if split_lanes:
            for h in range(KV_W // LANES):
                copies.append(pltpu.make_async_copy(pool_ref.at[l, page, :, pl.ds(h * LANES, LANES)],
                                                    buf_ref.at[h, dst_rows, :], sem))
        else:
            copies.append(pltpu.make_async_copy(pool_ref.at[l, page], buf_ref.at[dst_rows, :], sem))
    return copies


def _gather_compress_kernel(pt_ref, poolk_ref, poolv_ref, bdk_ref, bdv_ref, pek_ref, pev_ref, knw_ref, gm_ref,
                            kc_ref, vc_ref, kbuf, vbuf, sems, *, l):
    b = pl.program_id(0)
    c = pl.program_id(1)
    kcopies = _gather_pages(pt_ref, b, c, poolk_ref, l, kbuf, sems.at[0], True)
    vcopies = _gather_pages(pt_ref, b, c, poolv_ref, l, vbuf, sems.at[1], True)
    for cp in kcopies + vcopies:
        cp.start()
    n_blk = SEL_PAGES * PAGE_SIZE // CMP_BLOCK
    for cp in kcopies:
        cp.wait()
    kc_ref[...] = _group_rms(_compress_rows((kbuf.at[0], kbuf.at[1]), n_blk, bdk_ref, pek_ref),
                             gm_ref[...], knw_ref[...])
    for cp in vcopies:
        cp.wait()
    vc_ref[...] = _compress_rows((vbuf.at[0], vbuf.at[1]), n_blk, bdv_ref, pev_ref)


def _gather_compress(page_table, pool_k, pool_v, cmp_consts, l):
    bd, n_pages = page_table.shape
    n_chunks = n_pages // SEL_PAGES
    n_blk = SEL_PAGES * PAGE_SIZE // CMP_BLOCK
    bdk, bdv, pek, pev, knw, gm = cmp_consts
    const3 = lambda b, c, pt: (0, 0, 0)
    const2 = lambda b, c, pt: (0, 0)
    grid_spec = pltpu.PrefetchScalarGridSpec(
        num_scalar_prefetch=1,
        grid=(bd, n_chunks),
        in_specs=[
            pl.BlockSpec(memory_space=pl.ANY),
            pl.BlockSpec(memory_space=pl.ANY),
            pl.BlockSpec((CMP_BLOCK, LANES, LANES), const3),
            pl.BlockSpec((CMP_BLOCK, LANES, LANES), const3),
            pl.BlockSpec((CMP_BLOCK, LANES), const2),
            pl.BlockSpec((CMP_BLOCK, LANES), const2),
            pl.BlockSpec((1, KV_W), const2),
            pl.BlockSpec((KV_W, KV_W), const2),
        ],
        out_specs=[pl.BlockSpec((None, n_blk, KV_W), lambda b, c, pt: (b, c, 0))] * 2,
        scratch_shapes=[
            pltpu.VMEM((KV_W // LANES, SEL_PAGES * PAGE_SIZE, LANES), F32),
            pltpu.VMEM((KV_W // LANES, SEL_PAGES * PAGE_SIZE, LANES), F32),
            pltpu.SemaphoreType.DMA((2,)),
        ],
    )
    return pl.pallas_call(
        functools.partial(_gather_compress_kernel, l=l),
        grid_spec=grid_spec,
        out_shape=[jax.ShapeDtypeStruct((bd, n_chunks * n_blk, KV_W), F32)] * 2,
        compiler_params=_cparams(("arbitrary", "arbitrary"), 48),
        name="gather_compress",
    )(page_table, pool_k, pool_v, bdk, bdv, pek, pev, knw, gm)


def _fold_groups(x):
    rows = x.shape[0]
    grp = lax.broadcasted_iota(jnp.int32, (rows, 1), 0) >> 5
    out = jnp.zeros((rows, HEAD_DIM), F32)
    for g in range(N_KV):
        out = out + jnp.where(grp == g, x[:, g * HEAD_DIM:(g + 1) * HEAD_DIM], 0.0)
    return out


def _nt_dot(a, b):
    return lax.dot_general(a, b, (((1,), (1,)), ((), ())), preferred_element_type=F32)


def _sample_cmp_kernel(q_ref, kc_ref, vc_ref, slope_ref, pair_ref, oc_ref, sel_ref, *, past, nc, ns, ns_pad):
    rows = N_HEADS * SUBLANES
    q = (q_ref[...] * SCALE).astype(BF16)
    ncp = kc_ref.shape[0]
    s = _nt_dot(q, kc_ref[...].astype(BF16))
    qpos = past + (lax.broadcasted_iota(jnp.int32, (rows, 1), 0) & (SUBLANES - 1))
    n = lax.broadcasted_iota(jnp.int32, (1, ncp), 1)
    dist = qpos - ((n + 1) * CMP_BLOCK - 1)
    mask = (dist >= 0) & (n < nc)
    p = _softmax_rows(s - slope_ref[...] * dist.astype(F32), mask)
    oc_ref[...] = _fold_groups(jnp.dot(p.astype(BF16), vc_ref[...].astype(BF16), preferred_element_type=F32))

    parts = []
    for g in range(N_KV):
        acc = p[(g * Q_PER_KV) * SUBLANES:(g * Q_PER_KV + 1) * SUBLANES]
        for r in range(1, Q_PER_KV):
            acc = acc + p[(g * Q_PER_KV + r) * SUBLANES:(g * Q_PER_KV + r + 1) * SUBLANES]
        parts.append(acc)
    psum = jnp.concatenate(parts, axis=0)
    imp = _pair_sum(psum, pair_ref[...])
    gq = N_KV * SUBLANES
    jj = lax.broadcasted_iota(jnp.int32, (gq, ns_pad), 1)
    cur = (past + (lax.broadcasted_iota(jnp.int32, (gq, 1), 0) & (SUBLANES - 1))) >> 6
    forced = (jj == 0) | (jj == cur) | (jj == cur - 1)
    score = jnp.where(jj <= cur, imp + jnp.where(forced, FORCE, 0.0), NEG)
    sel = _top_mask(score, ns, min(TOP_N, ns))
    pieces = []
    for g in range(N_KV):
        for r in range(Q_PER_KV):
            pieces.append(sel[g * SUBLANES:(g + 1) * SUBLANES])
    sel_rows = jnp.concatenate(pieces, axis=0).astype(BF16)
    per = SEL_PAGES * PAGE_SIZE // SEL_BLOCK
    for c in range(ns_pad // per):
        sel_ref[c] = sel_rows[:, c * per:(c + 1) * per]


def _sample_cmp(qbd, kc_full, vc_full, slope_rows, past, nc, ns):
    bd, rows, _ = qbd.shape
    ncp = kc_full.shape[1]
    per = SEL_PAGES * PAGE_SIZE // SEL_BLOCK
    ns_pad = -(-ns // per) * per
    pair = _pair_matrix(ncp, ns_pad)
    return pl.pallas_call(
        functools.partial(_sample_cmp_kernel, past=past, nc=nc, ns=ns, ns_pad=ns_pad),
        grid=(bd,),
        in_specs=[
            pl.BlockSpec((None, rows, KV_W), lambda b: (b, 0, 0)),
            pl.BlockSpec((None, ncp, KV_W), lambda b: (b, 0, 0)),
            pl.BlockSpec((None, ncp, KV_W), lambda b: (b, 0, 0)),
            pl.BlockSpec((rows, 1), lambda b: (0, 0)),
            pl.BlockSpec((ncp, ns_pad), lambda b: (0, 0)),
        ],
        out_specs=[
            pl.BlockSpec((None, rows, HEAD_DIM), lambda b: (b, 0, 0)),
            pl.BlockSpec((None, ns_pad // per, rows, per), lambda b: (b, 0, 0, 0)),
        ],
        out_shape=[jax.ShapeDtypeStruct((bd, rows, HEAD_DIM), F32),
                   jax.ShapeDtypeStruct((bd, ns_pad // per, rows, per), BF16)],
        compiler_params=_cparams(("arbitrary",), 40),
        name="sample_cmp",
    )(qbd, kc_full, vc_full, slope_rows, pair)


def _sample_sel_kernel(pt_ref, poolk_ref, poolv_ref, q_ref, selc_ref, seln_ref, kn_ref, vn_ref, bk_ref, bv_ref,
                       kwn_ref, vwn_ref, oc_ref, gate_ref, slope_ref, o_ref,
                       kbuf, vbuf, sems, m_sc, l_sc, acc_sc, *, past, n_chunks, l):
    b = pl.program_id(0)
    c = pl.program_id(1)
    rows = N_HEADS * SUBLANES
    kcopies = _gather_pages(pt_ref, b, c, poolk_ref, l, kbuf, sems.at[0], False)
    vcopies = _gather_pages(pt_ref, b, c, poolv_ref, l, vbuf, sems.at[1], False)
    for cp in kcopies + vcopies:
        cp.start()

    @pl.when(c == 0)
    def _():
        m_sc[...] = jnp.full(m_sc.shape, NEG, F32)
        l_sc[...] = jnp.zeros(l_sc.shape, F32)
        acc_sc[...] = jnp.zeros(acc_sc.shape, F32)

    q = (q_ref[...] * SCALE).astype(BF16)
    qpos = past + (lax.broadcasted_iota(jnp.int32, (rows, 1), 0) & (SUBLANES - 1))
    slope = slope_ref[...]

    def online_update(s, mask, v_bf):
        s = jnp.where(mask, s, NEG)
        m_old = m_sc[...]
        m_new = jnp.maximum(m_old, jnp.max(s, axis=-1, keepdims=True))
        alpha = jnp.exp(m_old - m_new)
        p = jnp.where(mask, jnp.exp(s - m_new), 0.0)
        l_sc[...] = alpha * l_sc[...] + jnp.sum(p, axis=-1, keepdims=True)
        acc_sc[...] = alpha * acc_sc[...] + jnp.dot(p.astype(BF16), v_bf, preferred_element_type=F32)
        m_sc[...] = m_new

    def sel_scores(k_bf, pos0, sel_blocks):
        nk = k_bf.shape[0]
        nb = sel_blocks.shape[1]
        kidx = lax.broadcasted_iota(jnp.int32, (1, nk), 1)
        blk = lax.broadcasted_iota(jnp.int32, (nb, nk), 0)
        expand = jnp.where((kidx >> 6) == blk, 1.0, 0.0).astype(BF16)
        chosen = jnp.dot(sel_blocks, expand, preferred_element_type=F32)
        dist = qpos - (pos0 + kidx)
        mask = (dist >= 0) & (chosen > 0.5)
        return _nt_dot(q, k_bf) - slope * dist.astype(F32), mask

    for cp in kcopies:
        cp.wait()
    s, mask = sel_scores(kbuf[...].astype(BF16), c * (SEL_PAGES * PAGE_SIZE), selc_ref[...])
    for cp in vcopies:
        cp.wait()
    online_update(s, mask, vbuf[...].astype(BF16))

    @pl.when(c == n_chunks - 1)
    def _():
        s2, mask2 = sel_scores(kn_ref[...].astype(BF16), past, seln_ref[:, 0:1])
        online_update(s2, mask2, vn_ref[...].astype(BF16))
        o_s = _fold_groups(acc_sc[...] / jnp.maximum(l_sc[...], 1e-30))

        wb = bk_ref.shape[0]
        kw = jnp.concatenate([bk_ref[...], kwn_ref[...]], axis=0).astype(BF16)
        vw = jnp.concatenate([bv_ref[...], vwn_ref[...]], axis=0).astype(BF16)
        nkw = kw.shape[0]
        kidx = lax.broadcasted_iota(jnp.int32, (1, nkw), 1)
        kpos = past - wb + kidx
        dist = qpos - kpos
        mask_w = (dist >= 0) & (dist < WINDOW) & (kpos >= 0)
        pw = _softmax_rows(_nt_dot(q, kw) - slope * dist.astype(F32), mask_w)
        o_w = _fold_groups(jnp.dot(pw.astype(BF16), vw, preferred_element_type=F32))
        gate = gate_ref[...]
        o_ref[...] = gate[:, 0:1] * oc_ref[...] + gate[:, 1:2] * o_s + gate[:, 2:3] * o_w


def _sample_sel(page_table, pool_k, pool_v, qbd, sel_chunks, k_new, v_new, buf_k, buf_v, kw_new, vw_new,
                o_c, gate_rows, slope_rows, past, l):
    bd, n_pages = page_table.shape
    n_chunks = n_pages // SEL_PAGES
    rows = qbd.shape[1]
    per = sel_chunks.shape[3]
    wb = buf_k.shape[2]
    npad = kw_new.shape[1]
    buf_spec = pl.BlockSpec((None, None, wb, KV_W), lambda b, c, pt: (l, b, 0, 0))
    row3 = lambda n, w: pl.BlockSpec((None, n, w), lambda b, c, pt: (b, 0, 0))
    grid_spec = pltpu.PrefetchScalarGridSpec(
        num_scalar_prefetch=1,
        grid=(bd, n_chunks),
        in_specs=[
            pl.BlockSpec(memory_space=pl.ANY),
            pl.BlockSpec(memory_space=pl.ANY),
            row3(rows, KV_W),
            pl.BlockSpec((None, None, rows, per), lambda b, c, pt: (b, c, 0, 0)),
            pl.BlockSpec((None, None, rows, per), lambda b, c, pt: (b, n_chunks, 0, 0)),
            row3(SEL_BLOCK, KV_W), row3(SEL_BLOCK, KV_W),
            buf_spec, buf_spec,
            row3(npad, KV_W), row3(npad, KV_W),
            row3(rows, HEAD_DIM),
            row3(rows, 3),
            pl.BlockSpec((rows, 1), lambda b, c, pt: (0, 0)),
        ],
        out_specs=row3(rows, HEAD_DIM),
        scratch_shapes=[
            pltpu.VMEM((SEL_PAGES * PAGE_SIZE, KV_W), F32),
            pltpu.VMEM((SEL_PAGES * PAGE_SIZE, KV_W), F32),
            pltpu.SemaphoreType.DMA((2,)),
            pltpu.VMEM((rows, 1), F32), pltpu.VMEM((rows, 1), F32), pltpu.VMEM((rows, KV_W), F32),
        ],
    )
    return pl.pallas_call(
        functools.partial(_sample_sel_kernel, past=past, n_chunks=n_chunks, l=l),
        grid_spec=grid_spec,
        out_shape=jax.ShapeDtypeStruct((bd, rows, HEAD_DIM), F32),
        compiler_params=_cparams(("arbitrary", "arbitrary"), 56),
        name="sample_sel",
    )(page_table, pool_k, pool_v, qbd, sel_chunks, sel_chunks, k_new, v_new, buf_k, buf_v, kw_new, vw_new,
      o_c, gate_rows, slope_rows)


def _alibi_slopes():
    h = jnp.arange(1, N_HEADS + 1, dtype=F32)
    return (2.0 ** (-8.0 * h / N_HEADS)).reshape(N_KV, Q_PER_KV)


def _pad_rows(x, n):
    return jnp.pad(x, ((0, 0), (0, n - x.shape[1]), (0, 0)))


def _layer(l, x, mods, p, sample):
    b, t, d = x.shape
    m = b * t
    x2 = x.reshape(m, d)
    tm = min(t, 1024) if sample is None else m
    slopes = _alibi_slopes()

    h1 = _modulate(x2, p['norm1_w'], mods[0], mods[1], t, l)
    proj = _inproj(h1, p['w_in'], p['w_gm'][l], p['qn'][l], p['kn'][l], l, tm)
    proj3 = proj.reshape(b, t, proj.shape[1])

    def kv_slice(idx):
        return proj3[:, :, COL_KV + idx * KV_W:COL_KV + (idx + 1) * KV_W]

    k_cmp, v_cmp, k_sel, v_sel, k_win, v_win = [kv_slice(i) for i in range(6)]
    nsa_col = proj.shape[1] - TN
    ng = proj3[:, :, nsa_col:nsa_col + 3 * N_HEADS]
    cmp_consts = p['cmp'][l]

    if sample is None:
        o_a, _ = _chunkmix(proj, p['a_spatial_w'][l], p['a_spatial_b'][l].T, p['a_norm_w'][l][None, :], CHUNK, CHUNK)
        kc, vc = _compress(proj3, COL_KV // KV_W, COL_KV // KV_W + 1, cmp_consts)
        nc = t // CMP_BLOCK

        def kt(xk, n):
            return xk.reshape(b, n, N_KV, HEAD_DIM).transpose(0, 2, 3, 1).astype(BF16)

        def vt(xv, n):
            return xv.reshape(b, n, N_KV, HEAD_DIM).transpose(0, 2, 1, 3).astype(BF16)

        q_t = proj3[:, :, COL_Q:COL_Q + B_WIDTH].reshape(b, t, N_KV, Q_PER_KV, HEAD_DIM).transpose(0, 2, 3, 1, 4)
        ng_t = ng.reshape(b, t, 3, N_KV, Q_PER_KV).transpose(0, 3, 1, 2, 4).reshape(b, N_KV, t, 3 * Q_PER_KV)
        o_b = _nsa_prompt(slopes, q_t, kt(kc, nc), vt(vc, nc), kt(k_sel, t), vt(v_sel, t), kt(k_win, t),
                          vt(v_win, t), ng_t).reshape(m, B_WIDTH)
        v_norm = None
        wp = min(WINDOW, t)
        new_bk, new_bv = k_win[:, -wp:], v_win[:, -wp:]
    else:
        ds = t
        eye = jnp.eye(b, dtype=F32)
        w8 = p['a_spatial_w'][l][:, :ds, :ds]
        w_bd = jnp.einsum('ab,gts->gatbs', eye, w8).reshape(A_GROUPS, m, m)
        b_t = jnp.tile(p['a_spatial_b'][l][:, :ds].T, (b, 1))
        o_a, v_norm = _chunkmix(proj, w_bd, b_t, p['a_norm_w'][l][None, :], m, ds)

        page_table = sample['page_table']
        n_pages = page_table.shape[1]
        past = n_pages * PAGE_SIZE
        new_pad = -(-ds // SEL_BLOCK) * SEL_BLOCK
        nc = (past + new_pad) // CMP_BLOCK
        ns = (past + new_pad) // SEL_BLOCK
        kc_past, vc_past = _gather_compress(page_table, sample['cache_k_cmp'], sample['cache_v_cmp'], cmp_consts, l)
        new_rows = jnp.concatenate([_pad_rows(k_cmp, new_pad), _pad_rows(v_cmp, new_pad)], axis=2)
        kc_new, vc_new = _compress(new_rows, 0, 1, cmp_consts)
        ncp = -(-nc // LANES) * LANES
        kc_full = _pad_rows(jnp.concatenate([kc_past, kc_new], axis=1), ncp)
        vc_full = _pad_rows(jnp.concatenate([vc_past, vc_new], axis=1), ncp)

        q5 = proj3[:, :, COL_Q:COL_Q + B_WIDTH].reshape(b, ds, N_KV, Q_PER_KV, HEAD_DIM).transpose(0, 2, 3, 1, 4)
        qbd = jnp.einsum('bgrqd,gh->bgrqhd', q5, jnp.eye(N_KV, dtype=F32)).reshape(b, N_HEADS * ds, KV_W)
        slope_rows = jnp.repeat(slopes.reshape(N_HEADS), ds)[:, None]
        gate_rows = ng.reshape(b, ds, 3, N_HEADS).transpose(0, 3, 1, 2).reshape(b, N_HEADS * ds, 3)
        o_c, sel_chunks = _sample_cmp(qbd, kc_full, vc_full, slope_rows, past, nc, ns)
        o_rows = _sample_sel(page_table, sample['cache_k_sel'], sample['cache_v_sel'], qbd, sel_chunks,
                             _pad_rows(k_sel, new_pad), _pad_rows(v_sel, new_pad), sample['cache_k_win'],
                             sample['cache_v_win'], _pad_rows(k_win, LANES), _pad_rows(v_win, LANES), o_c,
                             gate_rows, slope_rows, past, l)
        buf_k, buf_v = sample['cache_k_win'][l], sample['cache_v_win'][l]
        o_b = o_rows.reshape(b, N_HEADS, ds, HEAD_DIM).transpose(0, 2, 1, 3).reshape(m, B_WIDTH).astype(BF16)
        keep = min(WINDOW, buf_k.shape[1] + ds)
        new_bk = jnp.concatenate([buf_k, k_win], axis=1)[:, -keep:]
        new_bv = jnp.concatenate([buf_v, v_win], axis=1)[:, -keep:]

    def gate_of(vec):
        return vec if sample is None else jnp.repeat(vec, t, axis=0)

    mix = _branch(o_a, o_b, p['w_branch'], proj, l, tm)
    x_mid = _resid_matmul(mix, p['w_out'], x2, gate_of(mods[2]), l, tm, 'attn_out')
    h2 = _modulate(x_mid, p['norm2_w'], mods[3], mods[4], t, l)
    if sample is None:
        s0 = s1 = None
    else:
        state = sample['state_ffn_conv'][l]
        s0 = jnp.repeat(state[:, 0], t, axis=0)
        s1 = jnp.repeat(state[:, 1], t, axis=0)
    hid, tails = _ffn_up(h2, p['ffn_w_a'], p['ffn_w_b'], p['ffn_conv_w'], p['ffn_conv_b'], s0, s1, t, l, tm)
    x_out = _resid_matmul(hid, p['ffn_w_down'], x_mid, gate_of(mods[5]), l, tm, 'ffn_down')

    if sample is None:
        conv_rows = tails.reshape(b, t // tm, SUBLANES, -1)[:, -1, SUBLANES - (CONV_W - 1):]
    else:
        conv_rows = tails.reshape(b, t, -1)[:, t - (CONV_W - 1):]
    heads = lambda z: z.reshape(b, -1, N_KV, HEAD_DIM)
    state_out = dict(k_cmp=heads(k_cmp), v_cmp=heads(v_cmp), k_sel=heads(k_sel), v_sel=heads(v_sel),
                     k_win=heads(new_bk), v_win=heads(new_bv), conv=conv_rows)
    if sample is not None:
        state_out['chunk_v'] = v_norm.reshape(b, t, A_WIDTH)
    return x_out.reshape(b, t, d), state_out


def kernel(x_prompt, x_sample, cache_k_cmp, cache_v_cmp, cache_k_sel, cache_v_sel, cache_k_win, cache_v_win,
           state_ffn_conv, page_table, c_prompt, c_sample, w_ada, b_ada, norm1_w, norm2_w, w_in, a_norm_w,
           a_spatial_w, a_spatial_b, q_norm_w, k_norm_w, cmp_pe_k, cmp_w_k, cmp_pe_v, cmp_w_v, w_branch, w_out,
           ffn_w_a, ffn_w_b, ffn_conv_w, ffn_conv_b, ffn_w_down):
    depth, d, n_in = w_in.shape
    bp = x_prompt.shape[0]
    bs = x_sample.shape[0]

    rows = -(-(bp + bs) // SUBLANES) * SUBLANES
    c_all = jnp.pad(jnp.concatenate([c_prompt, c_sample], axis=0), ((0, rows - bp - bs), (0, 0)))
    mods_all = _ada(c_all, w_ada, b_ada)

    n_merge = n_in - N_MAIN - 3 * N_HEADS
    pad = (-(n_merge + 3 * N_HEADS)) % TN
    w_gm = [jnp.concatenate([w_in[l, :, N_MAIN + 3 * N_HEADS:], w_in[l, :, N_MAIN:N_MAIN + 3 * N_HEADS],
                             jnp.zeros((d, pad), F32)], axis=1) for l in range(depth)]
    shared = dict(
        norm1_w=norm1_w, norm2_w=norm2_w, w_in=w_in, w_gm=w_gm,
        qn=jnp.tile(q_norm_w, (1, TN // HEAD_DIM))[:, None, :], kn=jnp.tile(k_norm_w, (1, TN // HEAD_DIM))[:, None, :],
        a_norm_w=a_norm_w, a_spatial_w=a_spatial_w, a_spatial_b=a_spatial_b,
        cmp=[_cmp_consts(cmp_pe_k[l], cmp_w_k[l], cmp_pe_v[l], cmp_w_v[l], k_norm_w[l]) for l in range(depth)],
        w_branch=w_branch, w_out=w_out, ffn_w_a=ffn_w_a, ffn_w_b=ffn_w_b, ffn_conv_w=ffn_conv_w,
        ffn_conv_b=ffn_conv_b, ffn_w_down=ffn_w_down)
    sample = dict(page_table=page_table, cache_k_cmp=cache_k_cmp, cache_v_cmp=cache_v_cmp, cache_k_sel=cache_k_sel,
                  cache_v_sel=cache_v_sel, cache_k_win=cache_k_win, cache_v_win=cache_v_win,
                  state_ffn_conv=state_ffn_conv)
    for name in ('cache_k_cmp', 'cache_v_cmp', 'cache_k_sel', 'cache_v_sel'):
        pool = sample[name]
        sample[name] = pool.reshape(pool.shape[0], pool.shape[1], pool.shape[2], KV_W)
    for name in ('cache_k_win', 'cache_v_win'):
        buf = sample[name]
        sample[name] = buf.reshape(buf.shape[0], buf.shape[1], buf.shape[2], KV_W)

    xp, xs = x_prompt, x_sample
    st_p, st_s = [], []
    for l in range(depth):
        mods = mods_all[l]
        mp = [mods[:bp, k * d:(k + 1) * d] for k in range(6)]
        ms = [mods[bp:bp + bs, k * d:(k + 1) * d] for k in range(6)]
        xp, sp = _layer(l, xp, mp, shared, None)
        xs, ss = _layer(l, xs, ms, shared, sample)
        st_p.append(sp)
        st_s.append(ss)

    def stack(sts, key):
        return jnp.stack([s[key] for s in sts], axis=0)

    return (xp, xs,
            stack(st_p, 'k_cmp'), stack(st_p, 'v_cmp'), stack(st_p, 'k_sel'), stack(st_p, 'v_sel'),
            stack(st_p, 'k_win'), stack(st_p, 'v_win'), stack(st_p, 'conv'),
            stack(st_s, 'k_cmp'), stack(st_s, 'v_cmp'), stack(st_s, 'k_sel'), stack(st_s, 'v_sel'),
            stack(st_s, 'k_win'), stack(st_s, 'v_win'), stack(st_s, 'chunk_v'), stack(st_s, 'conv'))
```

```python
import functools

import jax
import jax.numpy as jnp
from jax import lax
from jax.experimental import pallas as pl
from jax.experimental.pallas import tpu as pltpu

F32 = jnp.float32
BF16 = jnp.bfloat16

HEAD_DIM = 64
N_KV = 4
Q_PER_KV = 4
N_HEADS = N_KV * Q_PER_KV
KV_W = N_KV * HEAD_DIM
A_GROUPS = 8
A_GROUP_DIM = 128
A_WIDTH = A_GROUPS * A_GROUP_DIM
B_WIDTH = N_HEADS * HEAD_DIM
CHUNK = 128
CMP_BLOCK = 32
SEL_BLOCK = 64
TOP_N = 16
WINDOW = 512
Q_BLOCK = 128
PAGE_SIZE = 128
CONV_W = 3
SCALE = HEAD_DIM ** -0.5
EPS = 1e-6
NEG = -1e30
FORCE = 1e4

LANES = 128
SUBLANES = 8
MIB = 1024 * 1024

TN = 512
COL_U = 0
COL_V = A_WIDTH
COL_Q = 2 * A_WIDTH
COL_KV = COL_Q + B_WIDTH
N_MAIN = COL_KV + 6 * KV_W
N_MAIN_BLOCKS = N_MAIN // TN
SEL_PAGES = 32


def _cparams(sem, vmem_mib):
    return pltpu.CompilerParams(dimension_semantics=sem, vmem_limit_bytes=vmem_mib * MIB)


def _group_rms(x, gmat, w):
    y = x * x
    hi = y.astype(BF16)
    lo = (y - hi.astype(F32)).astype(BF16)
    ms = jnp.dot(hi, gmat, preferred_element_type=F32) + jnp.dot(lo, gmat, preferred_element_type=F32)
    return x * lax.rsqrt(ms + EPS) * w


def _group_mean_matrix(n):
    r = jnp.arange(n)
    return jnp.where((r[:, None] // HEAD_DIM) == (r[None, :] // HEAD_DIM), 1.0 / HEAD_DIM, 0.0).astype(BF16)


def _ada_kernel(c_ref, w_ref, b_ref, o_ref):
    c = c_ref[...]
    s = (c * jax.nn.sigmoid(c)).astype(BF16)
    o_ref[...] = jnp.dot(s, w_ref[...].astype(BF16), preferred_element_type=F32) + b_ref[...]


def _ada(c_all, w_ada, b_ada):
    depth, d, n6 = w_ada.shape
    rows = c_all.shape[0]
    tn = 1024
    return pl.pallas_call(
        _ada_kernel,
        grid=(depth, n6 // tn),
        in_specs=[
            pl.BlockSpec((rows, d), lambda l, j: (0, 0)),
            pl.BlockSpec((None, d, tn), lambda l, j: (l, 0, j)),
            pl.BlockSpec((None, 1, tn), lambda l, j: (l, 0, j)),
        ],
        out_specs=pl.BlockSpec((None, rows, tn), lambda l, j: (l, 0, j)),
        out_shape=jax.ShapeDtypeStruct((depth, rows, n6), F32),
        compiler_params=_cparams(("arbitrary", "arbitrary"), 40),
        name="ada",
    )(c_all, w_ada, b_ada.reshape(depth, 1, n6))


def _modulate_kernel(x_ref, g_ref, sh_ref, sc_ref, o_ref):
    x = x_ref[...]
    y = x * lax.rsqrt(jnp.mean(x * x, axis=-1, keepdims=True) + EPS)
    y = y * g_ref[...]
    o_ref[...] = (y * (1.0 + sc_ref[...]) + sh_ref[...]).astype(BF16)


def _modulate(x2, g, shift, scale, t, l):
    m, d = x2.shape
    nb = m // t
    tm = min(t, 512)
    per = t // tm
    return pl.pallas_call(
        _modulate_kernel,
        grid=(m // tm,),
        in_specs=[
            pl.BlockSpec((tm, d), lambda i: (i, 0)),
            pl.BlockSpec((None, 1, d), lambda i: (l, 0, 0)),
            pl.BlockSpec((None, 1, d), lambda i: (i // per, 0, 0)),
            pl.BlockSpec((None, 1, d), lambda i: (i // per, 0, 0)),
        ],
        out_specs=pl.BlockSpec((tm, d), lambda i: (i, 0)),
        out_shape=jax.ShapeDtypeStruct((m, d), BF16),
        compiler_params=_cparams(("arbitrary",), 32),
        name="modulate",
    )(x2, g.reshape(g.shape[0], 1, d), shift.reshape(nb, 1, d), scale.reshape(nb, 1, d))


def _inproj_kernel(h_ref, w1_ref, w2_ref, qn_ref, kn_ref, gm_ref, o_ref, wbf_ref):
    j = pl.program_id(0)
    i = pl.program_id(1)

    @pl.when((i == 0) & (j < N_MAIN_BLOCKS))
    def _():
        wbf_ref[...] = w1_ref[...].astype(BF16)

    @pl.when((i == 0) & (j >= N_MAIN_BLOCKS))
    def _():
        wbf_ref[...] = w2_ref[...].astype(BF16)

    acc = jnp.dot(h_ref[...], wbf_ref[...], preferred_element_type=F32)

    @pl.when(j < 4)
    def _():
        o_ref[...] = jax.nn.gelu(acc)

    @pl.when((j == 4) | (j == 5))
    def _():
        o_ref[...] = _group_rms(acc, gm_ref[...], qn_ref[...])

    @pl.when(j == 6)
    def _():
        o_ref[...] = acc

    @pl.when((j == 7) | (j == 8))
    def _():
        lane = lax.broadcasted_iota(jnp.int32, acc.shape, 1)
        o_ref[...] = jnp.where(lane < KV_W, _group_rms(acc, gm_ref[...], kn_ref[...]), acc)

    @pl.when(j >= N_MAIN_BLOCKS)
    def _():
        o_ref[...] = jax.nn.sigmoid(acc)


def _inproj(h, w_in, w_gm, qn, kn, l, tm):
    m, d = h.shape
    n_gm = w_gm.shape[1] // TN
    nj = N_MAIN_BLOCKS + n_gm
    gm = _group_mean_matrix(TN)
    return pl.pallas_call(
        _inproj_kernel,
        grid=(nj, m // tm),
        in_specs=[
            pl.BlockSpec((tm, d), lambda j, i: (i, 0)),
            pl.BlockSpec((None, d, TN), lambda j, i: (l, 0, jnp.minimum(j, N_MAIN_BLOCKS - 1))),
            pl.BlockSpec((d, TN), lambda j, i: (0, jnp.maximum(j - N_MAIN_BLOCKS, 0))),
            pl.BlockSpec((1, TN), lambda j, i: (0, 0)),
            pl.BlockSpec((1, TN), lambda j, i: (0, 0)),
            pl.BlockSpec((TN, TN), lambda j, i: (0, 0)),
        ],
        out_specs=pl.BlockSpec((tm, TN), lambda j, i: (i, j)),
        out_shape=jax.ShapeDtypeStruct((m, nj * TN), F32),
        scratch_shapes=[pltpu.VMEM((d, TN), BF16)],
        compiler_params=_cparams(("arbitrary", "arbitrary"), 48),
        name="inproj",
    )(h, w_in, w_gm, qn, kn, gm)


def _chunkmix_kernel(u_ref, gv_ref, w_ref, bt_ref, an_ref, oa_ref, v_ref, *, lc, causal_block):
    gv = gv_ref[...]
    v = gv * lax.rsqrt(jnp.mean(gv * gv, axis=-1, keepdims=True) + EPS) * an_ref[...]
    v_ref[...] = v
    row = lax.broadcasted_iota(jnp.int32, (lc, lc), 0)
    col = lax.broadcasted_iota(jnp.int32, (lc, lc), 1)
    shift = causal_block.bit_length() - 1
    mask = (col <= row) & ((row >> shift) == (col >> shift))
    vb = v.astype(BF16)
    for g in range(A_GROUPS):
        sl = slice(g * A_GROUP_DIM, (g + 1) * A_GROUP_DIM)
        wg = jnp.where(mask, w_ref[g], 0.0).astype(BF16)
        s = jnp.dot(wg, vb[:, sl], preferred_element_type=F32) + bt_ref[:, g:g + 1]
        oa_ref[:, sl] = (u_ref[:, sl] * s).astype(BF16)


def _chunkmix(proj, w_s, b_t, a_norm, lc, causal_block):
    m = proj.shape[0]
    return pl.pallas_call(
        functools.partial(_chunkmix_kernel, lc=lc, causal_block=causal_block),
        grid=(m // lc,),
        in_specs=[
            pl.BlockSpec((lc, A_WIDTH), lambda c: (c, 0)),
            pl.BlockSpec((lc, A_WIDTH), lambda c: (c, 1)),
            pl.BlockSpec((A_GROUPS, lc, lc), lambda c: (0, 0, 0)),
            pl.BlockSpec((lc, A_GROUPS), lambda c: (0, 0)),
            pl.BlockSpec((1, A_WIDTH), lambda c: (0, 0)),
        ],
        out_specs=[
            pl.BlockSpec((lc, A_WIDTH), lambda c: (c, 0)),
            pl.BlockSpec((lc, A_WIDTH), lambda c: (c, 0)),
        ],
        out_shape=[jax.ShapeDtypeStruct((m, A_WIDTH), BF16), jax.ShapeDtypeStruct((m, A_WIDTH), F32)],
        compiler_params=_cparams(("arbitrary",), 32),
        name="chunkmix",
    )(proj, proj, w_s, b_t, a_norm)


def _compress_rows(half_refs, n_blk, bd_ref, pe_ref):
    halves = []
    for ref in half_refs:
        acc = jnp.zeros((n_blk, LANES), F32)
        for i in range(CMP_BLOCK):
            rows = ref[pl.ds(i, n_blk, stride=CMP_BLOCK), :] + pe_ref[i:i + 1, :]
            acc = acc + jnp.dot(rows.astype(BF16), bd_ref[i], preferred_element_type=F32)
        halves.append(acc)
    return jnp.concatenate(halves, axis=1)


def _compress_kernel(k0_ref, k1_ref, v0_ref, v1_ref, bdk_ref, bdv_ref, pek_ref, pev_ref, knw_ref, gm_ref,
                     kc_ref, vc_ref, *, nc):
    kc_ref[...] = _group_rms(_compress_rows((k0_ref, k1_ref), nc, bdk_ref, pek_ref), gm_ref[...], knw_ref[...])
    vc_ref[...] = _compress_rows((v0_ref, v1_ref), nc, bdv_ref, pev_ref)


def _compress(src3, kblk, vblk, cmp_consts):
    b, t, _ = src3.shape
    nc = t // CMP_BLOCK
    bdk, bdv, pek, pev, knw, gm = cmp_consts
    const3 = lambda i: (0, 0, 0)
    const2 = lambda i: (0, 0)
    half = lambda blk, h: pl.BlockSpec((None, t, LANES), lambda i: (i, 0, 2 * blk + h))
    return pl.pallas_call(
        functools.partial(_compress_kernel, nc=nc),
        grid=(b,),
        in_specs=[
            half(kblk, 0), half(kblk, 1), half(vblk, 0), half(vblk, 1),
            pl.BlockSpec((CMP_BLOCK, LANES, LANES), const3),
            pl.BlockSpec((CMP_BLOCK, LANES, LANES), const3),
            pl.BlockSpec((CMP_BLOCK, LANES), const2),
            pl.BlockSpec((CMP_BLOCK, LANES), const2),
            pl.BlockSpec((1, KV_W), const2),
            pl.BlockSpec((KV_W, KV_W), const2),
        ],
        out_specs=[pl.BlockSpec((None, nc, KV_W), lambda i: (i, 0, 0))] * 2,
        out_shape=[jax.ShapeDtypeStruct((b, nc, KV_W), F32)] * 2,
        compiler_params=_cparams(("arbitrary",), 40),
        name="compress",
    )(src3, src3, src3, src3, bdk, bdv, pek, pev, knw, gm)


def _cmp_consts(cmp_pe_k, cmp_w_k, cmp_pe_v, cmp_w_v, k_norm_w):
    per = LANES // HEAD_DIM
    eye = jnp.eye(per, dtype=F32)

    def bd(w):
        return jnp.einsum('gh,ide->igdhe', eye, w).reshape(CMP_BLOCK, LANES, LANES).astype(BF16)

    return (bd(cmp_w_k), bd(cmp_w_v), jnp.tile(cmp_pe_k, (1, per)), jnp.tile(cmp_pe_v, (1, per)),
            jnp.tile(k_norm_w, N_KV)[None, :], _group_mean_matrix(KV_W))


def _pair_sum(psum, pair):
    hi = psum.astype(BF16)
    lo = (psum - hi.astype(F32)).astype(BF16)
    return jnp.dot(hi, pair, preferred_element_type=F32) + jnp.dot(lo, pair, preferred_element_type=F32)


def _top_mask(score, n_valid, top_n):
    rows, width = score.shape
    jj = lax.broadcasted_iota(jnp.int32, (rows, width), 1)
    rank = jnp.zeros((rows, width), F32)
    for k in range(n_valid):
        ck = score[:, k:k + 1]
        beats = (ck > score) | ((ck == score) & (jj > k))
        rank = rank + jnp.where(beats, 1.0, 0.0)
    return jnp.where((rank < top_n) & (jj < n_valid), 1.0, 0.0)


def _softmax_rows(s, mask):
    s = jnp.where(mask, s, NEG)
    m = jnp.max(s, axis=-1, keepdims=True)
    p = jnp.where(mask, jnp.exp(s - m), 0.0)
    return p / jnp.maximum(jnp.sum(p, axis=-1, keepdims=True), 1e-30)


def _softmax_cols(s, mask):
    s = jnp.where(mask, s, NEG)
    m = jnp.max(s, axis=0, keepdims=True)
    p = jnp.where(mask, jnp.exp(s - m), 0.0)
    return p / jnp.maximum(jnp.sum(p, axis=0, keepdims=True), 1e-30)


def _nsa_prompt_kernel(slopes_ref, qt_ref, kc_ref, vct_ref, ks_ref, vst_ref, kw_ref, vwt_ref, ngt_ref, o_ref,
                       qbd_sc, sel_sc, m_sc, l_sc, acc_sc, *, t, tk):
    g = pl.program_id(1)
    i = pl.program_id(2)
    nc = t // CMP_BLOCK
    ns = t // SEL_BLOCK
    start = i * Q_BLOCK
    cols = Q_PER_KV * Q_BLOCK
    lane_blk = [slice(r * Q_BLOCK, (r + 1) * Q_BLOCK) for r in range(Q_PER_KV)]
    slopes = [slopes_ref[g, r] for r in range(Q_PER_KV)]
    qpos = start + lax.broadcasted_iota(jnp.int32, (1, Q_BLOCK), 1)

    qbd_sc[...] = jnp.zeros(qbd_sc.shape, BF16)
    qbd_sc[pl.ds(pl.multiple_of(g * HEAD_DIM, HEAD_DIM), HEAD_DIM), :] = (qt_ref[...] * SCALE).astype(BF16)
    qbd = qbd_sc[...]

    half = nc // 2
    st = jnp.dot(kc_ref[...], qbd, preferred_element_type=F32)
    row = lax.broadcasted_iota(jnp.int32, (nc, 1), 0)
    blk_c = jnp.where(row < half, 2 * row, 2 * (row - half) + 1)
    dist_c = qpos - ((blk_c + 1) * CMP_BLOCK - 1)
    mask_c = dist_c >= 0
    dist_cf = dist_c.astype(F32)
    p_heads = [_softmax_cols(st[:, lane_blk[r]] - slopes[r] * dist_cf, mask_c) for r in range(Q_PER_KV)]
    o_c = jnp.dot(vct_ref[...], jnp.concatenate(p_heads, axis=1).astype(BF16), preferred_element_type=F32)
    psum = ((p_heads[0] + p_heads[1]) + p_heads[2]) + p_heads[3]
    imp = psum[:half] + psum[half:]

    jj = lax.broadcasted_iota(jnp.int32, (ns, 1), 0)
    cur = qpos >> 6
    forced = (jj == 0) | (jj == cur) | (jj == cur - 1)
    score = jnp.where(jj <= cur, imp + jnp.where(forced, FORCE, 0.0), NEG)
    rank = jnp.zeros((ns, Q_BLOCK), F32)
    for k in range(ns):
        sk = score[k:k + 1, :]
        beats = (sk > score) | ((sk == score) & (jj > k))
        rank = rank + jnp.where(beats, 1.0, 0.0)
    sel_sc[...] = jnp.where(rank < min(TOP_N, ns), 1.0, 0.0)

    m_sc[...] = jnp.full(m_sc.shape, NEG, F32)
    l_sc[...] = jnp.zeros(l_sc.shape, F32)
    acc_sc[...] = jnp.zeros(acc_sc.shape, F32)
    n_tiles = (start + Q_BLOCK - 1) // tk + 1
    blocks_per_tile = tk // SEL_BLOCK

    def sel_tile(ti, carry):
        k0 = pl.multiple_of(ti * tk, tk)
        s = jnp.dot(ks_ref[pl.ds(k0, tk), :], qbd, preferred_element_type=F32)
        dist = qpos - (k0 + lax.broadcasted_iota(jnp.int32, (tk, 1), 0))
        sel_rows = sel_sc[pl.ds(pl.multiple_of(ti * blocks_per_tile, blocks_per_tile), blocks_per_tile), :]
        chosen = jnp.concatenate(
            [jnp.broadcast_to(sel_rows[jb:jb + 1, :], (SEL_BLOCK, Q_BLOCK)) for jb in range(blocks_per_tile)], axis=0)
        mask = (dist >= 0) & (chosen > 0.5)
        distf = dist.astype(F32)
        m_old = m_sc[...]
        p_parts, m_parts, sum_parts = [], [], []
        for r in range(Q_PER_KV):
            sr = jnp.where(mask, s[:, lane_blk[r]] - slopes[r] * distf, NEG)
            m_new = jnp.maximum(m_old[:, lane_blk[r]], jnp.max(sr, axis=0, keepdims=True))
            p = jnp.where(mask, jnp.exp(sr - m_new), 0.0)
            m_parts.append(m_new)
            sum_parts.append(jnp.sum(p, axis=0, keepdims=True))
            p_parts.append(p.astype(BF16))
        m_new = jnp.concatenate(m_parts, axis=1)
        alpha = jnp.exp(m_old - m_new)
        l_sc[...] = alpha * l_sc[...] + jnp.concatenate(sum_parts, axis=1)
        pv = jnp.dot(vst_ref[:, pl.ds(k0, tk)], jnp.concatenate(p_parts, axis=1), preferred_element_type=F32)
        acc_sc[...] = alpha * acc_sc[...] + pv
        m_sc[...] = m_new
        return carry

    lax.fori_loop(0, n_tiles, sel_tile, 0)
    o_s = acc_sc[...] / jnp.maximum(l_sc[...], 1e-30)

    wk = WINDOW + Q_BLOCK
    k0w = pl.multiple_of(jnp.maximum(start - WINDOW, 0), Q_BLOCK)
    sw = jnp.dot(kw_ref[pl.ds(k0w, wk), :], qbd, preferred_element_type=F32)
    dist_w = qpos - (k0w + lax.broadcasted_iota(jnp.int32, (wk, 1), 0))
    mask_w = (dist_w >= 0) & (dist_w < WINDOW)
    dist_wf = dist_w.astype(F32)
    pw = [_softmax_cols(sw[:, lane_blk[r]] - slopes[r] * dist_wf, mask_w) for r in range(Q_PER_KV)]
    o_w = jnp.dot(vwt_ref[:, pl.ds(k0w, wk)], jnp.concatenate(pw, axis=1).astype(BF16), preferred_element_type=F32)

    ng = ngt_ref[...]
    o_ref[...] = (ng[0:1, :] * o_c + ng[1:2, :] * o_s + ng[2:3, :] * o_w).astype(BF16)


def _pair_matrix(nc, ns_pad):
    n = jnp.arange(nc)[:, None]
    j = jnp.arange(ns_pad)[None, :]
    return jnp.where((n // (SEL_BLOCK // CMP_BLOCK)) == j, 1.0, 0.0).astype(BF16)


def _nsa_prompt(slopes, q_t, kc_perm, vc_t, kv_bf, vs_t, vw_t, ng_t):
    b, _, nqb, _, cols = q_t.shape
    t = nqb * Q_BLOCK
    nc = t // CMP_BLOCK
    ns = t // SEL_BLOCK
    tk = min(512, t)
    blk5 = lambda rows: pl.BlockSpec((None, None, None, rows, cols), lambda bi, g, i: (bi, g, i, 0, 0))
    vt_spec = lambda n: pl.BlockSpec((None, None, HEAD_DIM, n), lambda bi, g, i: (bi, g, 0, 0))
    kv_spec = lambda col: pl.BlockSpec((None, t, KV_W), lambda bi, g, i: (bi, 0, col))
    return pl.pallas_call(
        functools.partial(_nsa_prompt_kernel, t=t, tk=tk),
        grid=(b, N_KV, nqb),
        in_specs=[
            pl.BlockSpec(memory_space=pltpu.SMEM),
            blk5(HEAD_DIM),
            pl.BlockSpec((None, nc, KV_W), lambda bi, g, i: (bi, 0, 0)),
            vt_spec(nc), kv_spec(2), vt_spec(t), kv_spec(4), vt_spec(t),
            blk5(3),
        ],
        out_specs=blk5(HEAD_DIM),
        out_shape=jax.ShapeDtypeStruct((b, N_KV, nqb, HEAD_DIM, cols), BF16),
        scratch_shapes=[pltpu.VMEM((KV_W, cols), BF16), pltpu.VMEM((ns, Q_BLOCK), F32),
                        pltpu.VMEM((1, cols), F32), pltpu.VMEM((1, cols), F32), pltpu.VMEM((HEAD_DIM, cols), F32)],
        compiler_params=_cparams(("arbitrary", "arbitrary", "arbitrary"), 48),
        name="nsa_prompt",
    )(slopes, q_t, kc_perm, vc_t, kv_bf, vs_t, kv_bf, vw_t, ng_t)


def _branch_kernel(oa_ref, ob_ref, w_ref, g0_ref, g1_ref, o_ref, wbf_ref):
    @pl.when(pl.program_id(1) == 0)
    def _():
        wbf_ref[...] = w_ref[...].astype(BF16)

    pa = jnp.dot(oa_ref[...], wbf_ref[:A_WIDTH, :], preferred_element_type=F32)
    pb = jnp.dot(ob_ref[...], wbf_ref[A_WIDTH:, :], preferred_element_type=F32)
    o_ref[...] = (g0_ref[...] * pa + g1_ref[...] * pb).astype(BF16)


def _branch(o_a, o_b, w_branch, proj, l, tm):
    m = o_a.shape[0]
    kb, d = w_branch.shape[1:]
    nj = d // TN
    return pl.pallas_call(
        _branch_kernel,
        grid=(nj, m // tm),
        in_specs=[
            pl.BlockSpec((tm, A_WIDTH), lambda j, i: (i, 0)),
            pl.BlockSpec((tm, B_WIDTH), lambda j, i: (i, 0)),
            pl.BlockSpec((None, kb, TN), lambda j, i: (l, 0, j)),
            pl.BlockSpec((tm, TN), lambda j, i: (i, N_MAIN_BLOCKS + j)),
            pl.BlockSpec((tm, TN), lambda j, i: (i, N_MAIN_BLOCKS + nj + j)),
        ],
        out_specs=pl.BlockSpec((tm, TN), lambda j, i: (i, j)),
        out_shape=jax.ShapeDtypeStruct((m, d), BF16),
        scratch_shapes=[pltpu.VMEM((kb, TN), BF16)],
        compiler_params=_cparams(("arbitrary", "arbitrary"), 48),
        name="branch",
    )(o_a, o_b, w_branch, proj, proj)


def _resid_kernel(a_ref, w_ref, x_ref, g_ref, o_ref, wbf_ref):
    @pl.when(pl.program_id(1) == 0)
    def _():
        wbf_ref[...] = w_ref[...].astype(BF16)

    o_ref[...] = x_ref[...] + g_ref[...] * jnp.dot(a_ref[...], wbf_ref[...], preferred_element_type=F32)


def _resid_matmul(a, w, x2, gate, l, tm, name):
    m, k = a.shape
    d = w.shape[2]
    if gate.shape[0] == m:
        gate_spec = pl.BlockSpec((tm, TN), lambda j, i: (i, j))
    else:
        nb = gate.shape[0]
        per = (m // nb) // tm
        gate = gate.reshape(nb, 1, d)
        gate_spec = pl.BlockSpec((None, 1, TN), lambda j, i: (i // per, 0, j))
    return pl.pallas_call(
        _resid_kernel,
        grid=(d // TN, m // tm),
        in_specs=[
            pl.BlockSpec((tm, k), lambda j, i: (i, 0)),
            pl.BlockSpec((None, k, TN), lambda j, i: (l, 0, j), pipeline_mode=pl.Buffered(1)),
            pl.BlockSpec((tm, TN), lambda j, i: (i, j)),
            gate_spec,
        ],
        out_specs=pl.BlockSpec((tm, TN), lambda j, i: (i, j)),
        out_shape=jax.ShapeDtypeStruct((m, d), F32),
        scratch_shapes=[pltpu.VMEM((k, TN), BF16)],
        compiler_params=_cparams(("arbitrary", "arbitrary"), 56),
        name=name,
    )(a, w, x2, gate)


def _ffn_up_kernel(h_ref, wa_ref, wb_ref, cw_ref, cb_ref, s0_ref, s1_ref, o_ref, tail_ref,
                   wa_bf, wb_bf, abuf, *, tm, seg, per_batch, keep):
    i = pl.program_id(1)

    @pl.when(i == 0)
    def _():
        wa_bf[...] = wa_ref[...].astype(BF16)
        wb_bf[...] = wb_ref[...].astype(BF16)

    h = h_ref[...]
    a = jnp.dot(h, wa_bf[...], preferred_element_type=F32)
    b = jnp.dot(h, wb_bf[...], preferred_element_type=F32)

    @pl.when(i % per_batch == 0)
    def _():
        abuf[0:SUBLANES, :] = jnp.zeros((SUBLANES, TN), F32)

    abuf[SUBLANES:SUBLANES + tm, :] = a
    a1 = abuf[SUBLANES - 1:SUBLANES - 1 + tm, :]
    a2 = abuf[SUBLANES - 2:SUBLANES - 2 + tm, :]
    if seg < tm:
        r = lax.broadcasted_iota(jnp.int32, (tm, 1), 0) & (seg - 1)
        a1 = jnp.where(r == 0, s1_ref[...], a1)
        a2 = jnp.where(r == 0, s0_ref[...], jnp.where(r == 1, s1_ref[...], a2))
    conv = a2 * cw_ref[0:1, :] + a1 * cw_ref[1:2, :] + a * cw_ref[2:3, :] + cb_ref[...]
    o_ref[...] = (jax.nn.gelu(conv) * b).astype(BF16)
    tail_ref[...] = a[tm - keep:, :]
    abuf[0:SUBLANES, :] = a[tm - SUBLANES:, :]


def _ffn_up(h2, w_a, w_b, conv_w, conv_b, s0, s1, t, l, tm):
    m, d = h2.shape
    dff = w_a.shape[2]
    per_batch = max(t // tm, 1)
    seg = min(t, tm)
    keep = SUBLANES if seg == tm else tm
    if s0 is None:
        s0 = s1 = jnp.zeros((SUBLANES, dff), F32)
        s_spec = pl.BlockSpec((SUBLANES, TN), lambda j, i: (0, j))
    else:
        s_spec = pl.BlockSpec((tm, TN), lambda j, i: (i, j))
    w_spec = pl.BlockSpec((None, d, TN), lambda j, i: (l, 0, j))
    return pl.pallas_call(
        functools.partial(_ffn_up_kernel, tm=tm, seg=seg, per_batch=per_batch, keep=keep),
        grid=(dff // TN, m // tm),
        in_specs=[
            pl.BlockSpec((tm, d), lambda j, i: (i, 0)),
            w_spec, w_spec,
            pl.BlockSpec((None, CONV_W, TN), lambda j, i: (l, 0, j)),
            pl.BlockSpec((None, 1, TN), lambda j, i: (l, 0, j)),
            s_spec, s_spec,
        ],
        out_specs=[
            pl.BlockSpec((tm, TN), lambda j, i: (i, j)),
            pl.BlockSpec((None, keep, TN), lambda j, i: (i, 0, j)),
        ],
        out_shape=[jax.ShapeDtypeStruct((m, dff), BF16), jax.ShapeDtypeStruct((m // tm, keep, dff), F32)],
        scratch_shapes=[pltpu.VMEM((d, TN), BF16), pltpu.VMEM((d, TN), BF16), pltpu.VMEM((tm + SUBLANES, TN), F32)],
        compiler_params=_cparams(("arbitrary", "arbitrary"), 56),
        name="ffn_up",
    )(h2, w_a, w_b, conv_w, conv_b.reshape(conv_b.shape[0], 1, dff), s0, s1)


def _gather_pages(pt_ref, b, c, pool_ref, l, buf_ref, sem, split_lanes):
    copies = []
    for p in range(SEL_PAGES):
        page = pt_ref[b, c * SEL_PAGES + p]
        dst_rows = pl.ds(p * PAGE_SIZE, PAGE_SIZE)
        if split_lanes:
            for h in range(KV_W // LANES):
                copies.append(pltpu.make_async_copy(pool_ref.at[l, page, :, pl.ds(h * LANES, LANES)],
                                                    buf_ref.at[h, dst_rows, :], sem))
        else:
            copies.append(pltpu.make_async_copy(pool_ref.at[l, page], buf_ref.at[dst_rows, :], sem))
    return copies


def _gather_compress_kernel(pt_ref, poolk_ref, poolv_ref, bdk_ref, bdv_ref, pek_ref, pev_ref, knw_ref, gm_ref,
                            kc_ref, vc_ref, kbuf, vbuf, sems, *, l):
    b = pl.program_id(0)
    c = pl.program_id(1)
    kcopies = _gather_pages(pt_ref, b, c, poolk_ref, l, kbuf, sems.at[0], True)
    vcopies = _gather_pages(pt_ref, b, c, poolv_ref, l, vbuf, sems.at[1], True)
    for cp in kcopies + vcopies:
        cp.start()
    n_blk = SEL_PAGES * PAGE_SIZE // CMP_BLOCK
    for cp in kcopies:
        cp.wait()
    kc_ref[...] = _group_rms(_compress_rows((kbuf.at[0], kbuf.at[1]), n_blk, bdk_ref, pek_ref),
                             gm_ref[...], knw_ref[...])
    for cp in vcopies:
        cp.wait()
    vc_ref[...] = _compress_rows((vbuf.at[0], vbuf.at[1]), n_blk, bdv_ref, pev_ref)


def _gather_compress(page_table, pool_k, pool_v, cmp_consts, l):
    bd, n_pages = page_table.shape
    n_chunks = n_pages // SEL_PAGES
    n_blk = SEL_PAGES * PAGE_SIZE // CMP_BLOCK
    bdk, bdv, pek, pev, knw, gm = cmp_consts
    const3 = lambda b, c, pt: (0, 0, 0)
    const2 = lambda b, c, pt: (0, 0)
    grid_spec = pltpu.PrefetchScalarGridSpec(
        num_scalar_prefetch=1,
        grid=(bd, n_chunks),
        in_specs=[
            pl.BlockSpec(memory_space=pl.ANY),
            pl.BlockSpec(memory_space=pl.ANY),
            pl.BlockSpec((CMP_BLOCK, LANES, LANES), const3),
            pl.BlockSpec((CMP_BLOCK, LANES, LANES), const3),
            pl.BlockSpec((CMP_BLOCK, LANES), const2),
            pl.BlockSpec((CMP_BLOCK, LANES), const2),
            pl.BlockSpec((1, KV_W), const2),
            pl.BlockSpec((KV_W, KV_W), const2),
        ],
        out_specs=[pl.BlockSpec((None, n_blk, KV_W), lambda b, c, pt: (b, c, 0))] * 2,
        scratch_shapes=[
            pltpu.VMEM((KV_W // LANES, SEL_PAGES * PAGE_SIZE, LANES), F32),
            pltpu.VMEM((KV_W // LANES, SEL_PAGES * PAGE_SIZE, LANES), F32),
            pltpu.SemaphoreType.DMA((2,)),
        ],
    )
    return pl.pallas_call(
        functools.partial(_gather_compress_kernel, l=l),
        grid_spec=grid_spec,
        out_shape=[jax.ShapeDtypeStruct((bd, n_chunks * n_blk, KV_W), F32)] * 2,
        compiler_params=_cparams(("arbitrary", "arbitrary"), 48),
        name="gather_compress",
    )(page_table, pool_k, pool_v, bdk, bdv, pek, pev, knw, gm)


def _fold_groups(x):
    rows = x.shape[0]
    grp = lax.broadcasted_iota(jnp.int32, (rows, 1), 0) >> 5
    out = jnp.zeros((rows, HEAD_DIM), F32)
    for g in range(N_KV):
        out = out + jnp.where(grp == g, x[:, g * HEAD_DIM:(g + 1) * HEAD_DIM], 0.0)
    return out


def _nt_dot(a, b):
    return lax.dot_general(a, b, (((1,), (1,)), ((), ())), preferred_element_type=F32)


def _sample_cmp_kernel(q_ref, kc_ref, vc_ref, slope_ref, pair_ref, oc_ref, sel_ref, *, past, nc, ns, ns_pad):
    rows = N_HEADS * SUBLANES
    q = (q_ref[...] * SCALE).astype(BF16)
    ncp = kc_ref.shape[0]
    s = _nt_dot(q, kc_ref[...].astype(BF16))
    qpos = past + (lax.broadcasted_iota(jnp.int32, (rows, 1), 0) & (SUBLANES - 1))
    n = lax.broadcasted_iota(jnp.int32, (1, ncp), 1)
    dist = qpos - ((n + 1) * CMP_BLOCK - 1)
    mask = (dist >= 0) & (n < nc)
    p = _softmax_rows(s - slope_ref[...] * dist.astype(F32), mask)
    oc_ref[...] = _fold_groups(jnp.dot(p.astype(BF16), vc_ref[...].astype(BF16), preferred_element_type=F32))

    parts = []
    for g in range(N_KV):
        acc = p[(g * Q_PER_KV) * SUBLANES:(g * Q_PER_KV + 1) * SUBLANES]
        for r in range(1, Q_PER_KV):
            acc = acc + p[(g * Q_PER_KV + r) * SUBLANES:(g * Q_PER_KV + r + 1) * SUBLANES]
        parts.append(acc)
    psum = jnp.concatenate(parts, axis=0)
    imp = _pair_sum(psum, pair_ref[...])
    gq = N_KV * SUBLANES
    jj = lax.broadcasted_iota(jnp.int32, (gq, ns_pad), 1)
    cur = (past + (lax.broadcasted_iota(jnp.int32, (gq, 1), 0) & (SUBLANES - 1))) >> 6
    forced = (jj == 0) | (jj == cur) | (jj == cur - 1)
    score = jnp.where(jj <= cur, imp + jnp.where(forced, FORCE, 0.0), NEG)
    sel = _top_mask(score, ns, min(TOP_N, ns))
    pieces = []
    for g in range(N_KV):
        for r in range(Q_PER_KV):
            pieces.append(sel[g * SUBLANES:(g + 1) * SUBLANES])
    sel_rows = jnp.concatenate(pieces, axis=0).astype(BF16)
    per = SEL_PAGES * PAGE_SIZE // SEL_BLOCK
    for c in range(ns_pad // per):
        sel_ref[c] = sel_rows[:, c * per:(c + 1) * per]


def _sample_cmp(qbd, kc_full, vc_full, slope_rows, past, nc, ns):
    bd, rows, _ = qbd.shape
    ncp = kc_full.shape[1]
    per = SEL_PAGES * PAGE_SIZE // SEL_BLOCK
    ns_pad = -(-ns // per) * per
    pair = _pair_matrix(ncp, ns_pad)
    return pl.pallas_call(
        functools.partial(_sample_cmp_kernel, past=past, nc=nc, ns=ns, ns_pad=ns_pad),
        grid=(bd,),
        in_specs=[
            pl.BlockSpec((None, rows, KV_W), lambda b: (b, 0, 0)),
            pl.BlockSpec((None, ncp, KV_W), lambda b: (b, 0, 0)),
            pl.BlockSpec((None, ncp, KV_W), lambda b: (b, 0, 0)),
            pl.BlockSpec((rows, 1), lambda b: (0, 0)),
            pl.BlockSpec((ncp, ns_pad), lambda b: (0, 0)),
        ],
        out_specs=[
            pl.BlockSpec((None, rows, HEAD_DIM), lambda b: (b, 0, 0)),
            pl.BlockSpec((None, ns_pad // per, rows, per), lambda b: (b, 0, 0, 0)),
        ],
        out_shape=[jax.ShapeDtypeStruct((bd, rows, HEAD_DIM), F32),
                   jax.ShapeDtypeStruct((bd, ns_pad // per, rows, per), BF16)],
        compiler_params=_cparams(("arbitrary",), 40),
        name="sample_cmp",
    )(qbd, kc_full, vc_full, slope_rows, pair)


def _sample_sel_kernel(pt_ref, poolk_ref, poolv_ref, q_ref, selc_ref, seln_ref, kn_ref, vn_ref, bk_ref, bv_ref,
                       kwn_ref, vwn_ref, oc_ref, gate_ref, slope_ref, o_ref,
                       kbuf, vbuf, sems, m_sc, l_sc, acc_sc, *, past, n_chunks, l):
    b = pl.program_id(0)
    c = pl.program_id(1)
    rows = N_HEADS * SUBLANES
    kcopies = _gather_pages(pt_ref, b, c, poolk_ref, l, kbuf, sems.at[0], False)
    vcopies = _gather_pages(pt_ref, b, c, poolv_ref, l, vbuf, sems.at[1], False)
    for cp in kcopies + vcopies:
        cp.start()

    @pl.when(c == 0)
    def _():
        m_sc[...] = jnp.full(m_sc.shape, NEG, F32)
        l_sc[...] = jnp.zeros(l_sc.shape, F32)
        acc_sc[...] = jnp.zeros(acc_sc.shape, F32)

    q = (q_ref[...] * SCALE).astype(BF16)
    qpos = past + (lax.broadcasted_iota(jnp.int32, (rows, 1), 0) & (SUBLANES - 1))
    slope = slope_ref[...]

    def online_update(s, mask, v_bf):
        s = jnp.where(mask, s, NEG)
        m_old = m_sc[...]
        m_new = jnp.maximum(m_old, jnp.max(s, axis=-1, keepdims=True))
        alpha = jnp.exp(m_old - m_new)
        p = jnp.where(mask, jnp.exp(s - m_new), 0.0)
        l_sc[...] = alpha * l_sc[...] + jnp.sum(p, axis=-1, keepdims=True)
        acc_sc[...] = alpha * acc_sc[...] + jnp.dot(p.astype(BF16), v_bf, preferred_element_type=F32)
        m_sc[...] = m_new

    def sel_scores(k_bf, pos0, sel_blocks):
        nk = k_bf.shape[0]
        nb = sel_blocks.shape[1]
        kidx = lax.broadcasted_iota(jnp.int32, (1, nk), 1)
        blk = lax.broadcasted_iota(jnp.int32, (nb, nk), 0)
        expand = jnp.where((kidx >> 6) == blk, 1.0, 0.0).astype(BF16)
        chosen = jnp.dot(sel_blocks, expand, preferred_element_type=F32)
        dist = qpos - (pos0 + kidx)
        mask = (dist >= 0) & (chosen > 0.5)
        return _nt_dot(q, k_bf) - slope * dist.astype(F32), mask

    for cp in kcopies:
        cp.wait()
    s, mask = sel_scores(kbuf[...].astype(BF16), c * (SEL_PAGES * PAGE_SIZE), selc_ref[...])
    for cp in vcopies:
        cp.wait()
    online_update(s, mask, vbuf[...].astype(BF16))

    @pl.when(c == n_chunks - 1)
    def _():
        s2, mask2 = sel_scores(kn_ref[...].astype(BF16), past, seln_ref[:, 0:1])
        online_update(s2, mask2, vn_ref[...].astype(BF16))
        o_s = _fold_groups(acc_sc[...] / jnp.maximum(l_sc[...], 1e-30))

        wb = bk_ref.shape[0]
        kw = jnp.concatenate([bk_ref[...], kwn_ref[...]], axis=0).astype(BF16)
        vw = jnp.concatenate([bv_ref[...], vwn_ref[...]], axis=0).astype(BF16)
        nkw = kw.shape[0]
        kidx = lax.broadcasted_iota(jnp.int32, (1, nkw), 1)
        kpos = past - wb + kidx
        dist = qpos - kpos
        mask_w = (dist >= 0) & (dist < WINDOW) & (kpos >= 0)
        pw = _softmax_rows(_nt_dot(q, kw) - slope * dist.astype(F32), mask_w)
        o_w = _fold_groups(jnp.dot(pw.astype(BF16), vw, preferred_element_type=F32))
        gate = gate_ref[...]
        o_ref[...] = gate[:, 0:1] * oc_ref[...] + gate[:, 1:2] * o_s + gate[:, 2:3] * o_w


def _sample_sel(page_table, pool_k, pool_v, qbd, sel_chunks, k_new, v_new, buf_k, buf_v, kw_new, vw_new,
                o_c, gate_rows, slope_rows, past, l):
    bd, n_pages = page_table.shape
    n_chunks = n_pages // SEL_PAGES
    rows = qbd.shape[1]
    per = sel_chunks.shape[3]
    wb = buf_k.shape[2]
    npad = kw_new.shape[1]
    buf_spec = pl.BlockSpec((None, None, wb, KV_W), lambda b, c, pt: (l, b, 0, 0))
    row3 = lambda n, w: pl.BlockSpec((None, n, w), lambda b, c, pt: (b, 0, 0))
    grid_spec = pltpu.PrefetchScalarGridSpec(
        num_scalar_prefetch=1,
        grid=(bd, n_chunks),
        in_specs=[
            pl.BlockSpec(memory_space=pl.ANY),
            pl.BlockSpec(memory_space=pl.ANY),
            row3(rows, KV_W),
            pl.BlockSpec((None, None, rows, per), lambda b, c, pt: (b, c, 0, 0)),
            pl.BlockSpec((None, None, rows, per), lambda b, c, pt: (b, n_chunks, 0, 0)),
            row3(SEL_BLOCK, KV_W), row3(SEL_BLOCK, KV_W),
            buf_spec, buf_spec,
            row3(npad, KV_W), row3(npad, KV_W),
            row3(rows, HEAD_DIM),
            row3(rows, 3),
            pl.BlockSpec((rows, 1), lambda b, c, pt: (0, 0)),
        ],
        out_specs=row3(rows, HEAD_DIM),
        scratch_shapes=[
            pltpu.VMEM((SEL_PAGES * PAGE_SIZE, KV_W), F32),
            pltpu.VMEM((SEL_PAGES * PAGE_SIZE, KV_W), F32),
            pltpu.SemaphoreType.DMA((2,)),
            pltpu.VMEM((rows, 1), F32), pltpu.VMEM((rows, 1), F32), pltpu.VMEM((rows, KV_W), F32),
        ],
    )
    return pl.pallas_call(
        functools.partial(_sample_sel_kernel, past=past, n_chunks=n_chunks, l=l),
        grid_spec=grid_spec,
        out_shape=jax.ShapeDtypeStruct((bd, rows, HEAD_DIM), F32),
        compiler_params=_cparams(("arbitrary", "arbitrary"), 56),
        name="sample_sel",
    )(page_table, pool_k, pool_v, qbd, sel_chunks, sel_chunks, k_new, v_new, buf_k, buf_v, kw_new, vw_new,
      o_c, gate_rows, slope_rows)


def _alibi_slopes():
    h = jnp.arange(1, N_HEADS + 1, dtype=F32)
    return (2.0 ** (-8.0 * h / N_HEADS)).reshape(N_KV, Q_PER_KV)


def _pad_rows(x, n):
    return jnp.pad(x, ((0, 0), (0, n - x.shape[1]), (0, 0)))


def _layer(l, x, mods, p, sample):
    b, t, d = x.shape
    m = b * t
    x2 = x.reshape(m, d)
    tm = min(t, 1024) if sample is None else m
    slopes = _alibi_slopes()

    h1 = _modulate(x2, p['norm1_w'], mods[0], mods[1], t, l)
    proj = _inproj(h1, p['w_in'], p['w_gm'][l], p['qn'][l], p['kn'][l], l, tm)
    proj3 = proj.reshape(b, t, proj.shape[1])

    def kv_slice(idx):
        return proj3[:, :, COL_KV + idx * KV_W:COL_KV + (idx + 1) * KV_W]

    k_cmp, v_cmp, k_sel, v_sel, k_win, v_win = [kv_slice(i) for i in range(6)]
    nsa_col = proj.shape[1] - TN
    ng = proj3[:, :, nsa_col:nsa_col + 3 * N_HEADS]
    cmp_consts = p['cmp'][l]

    if sample is None:
        o_a, _ = _chunkmix(proj, p['a_spatial_w'][l], p['a_spatial_b'][l].T, p['a_norm_w'][l][None, :], CHUNK, CHUNK)
        kc, vc = _compress(proj3, COL_KV // KV_W, COL_KV // KV_W + 1, cmp_consts)
        nc = t // CMP_BLOCK
        nqb = t // Q_BLOCK
        cols = Q_PER_KV * Q_BLOCK

        def vt(xv, n):
            return xv.reshape(b, n, N_KV, HEAD_DIM).transpose(0, 2, 3, 1).astype(BF16)

        def evens_first(xc):
            return xc.reshape(b, nc // 2, 2, KV_W).transpose(0, 2, 1, 3).reshape(b, nc, KV_W)

        q_t = proj3[:, :, COL_Q:COL_Q + B_WIDTH].reshape(b, nqb, Q_BLOCK, N_KV, Q_PER_KV, HEAD_DIM)
        q_t = q_t.transpose(0, 3, 1, 5, 4, 2).reshape(b, N_KV, nqb, HEAD_DIM, cols)
        ng_t = ng.reshape(b, nqb, Q_BLOCK, 3, N_KV, Q_PER_KV).transpose(0, 4, 1, 3, 5, 2).reshape(b, N_KV, nqb, 3, cols)
        kv_bf = proj3[:, :, COL_KV:COL_KV + 6 * KV_W].astype(BF16)
        o_t = _nsa_prompt(slopes, q_t, evens_first(kc).astype(BF16), vt(evens_first(vc), nc), kv_bf,
                          vt(v_sel, t), vt(v_win, t), ng_t)
        o_b = o_t.reshape(b, N_KV, nqb, HEAD_DIM, Q_PER_KV, Q_BLOCK).transpose(0, 2, 5, 1, 4, 3).reshape(m, B_WIDTH)
        v_norm = None
        wp = min(WINDOW, t)
        new_bk, new_bv = k_win[:, -wp:], v_win[:, -wp:]
    else:
        ds = t
        eye = jnp.eye(b, dtype=F32)
        w8 = p['a_spatial_w'][l][:, :ds, :ds]
        w_bd = jnp.einsum('ab,gts->gatbs', eye, w8).reshape(A_GROUPS, m, m)
        b_t = jnp.tile(p['a_spatial_b'][l][:, :ds].T, (b, 1))
        o_a, v_norm = _chunkmix(proj, w_bd, b_t, p['a_norm_w'][l][None, :], m, ds)

        page_table = sample['page_table']
        n_pages = page_table.shape[1]
        past = n_pages * PAGE_SIZE
        new_pad = -(-ds // SEL_BLOCK) * SEL_BLOCK
        nc = (past + new_pad) // CMP_BLOCK
        ns = (past + new_pad) // SEL_BLOCK
        kc_past, vc_past = _gather_compress(page_table, sample['cache_k_cmp'], sample['cache_v_cmp'], cmp_consts, l)
        new_rows = jnp.concatenate([_pad_rows(k_cmp, new_pad), _pad_rows(v_cmp, new_pad)], axis=2)
        kc_new, vc_new = _compress(new_rows, 0, 1, cmp_consts)
        ncp = -(-nc // LANES) * LANES
        kc_full = _pad_rows(jnp.concatenate([kc_past, kc_new], axis=1), ncp)
        vc_full = _pad_rows(jnp.concatenate([vc_past, vc_new], axis=1), ncp)

        q5 = proj3[:, :, COL_Q:COL_Q + B_WIDTH].reshape(b, ds, N_KV, Q_PER_KV, HEAD_DIM).transpose(0, 2, 3, 1, 4)
        qbd = jnp.einsum('bgrqd,gh->bgrqhd', q5, jnp.eye(N_KV, dtype=F32)).reshape(b, N_HEADS * ds, KV_W)
        slope_rows = jnp.repeat(slopes.reshape(N_HEADS), ds)[:, None]
        gate_rows = ng.reshape(b, ds, 3, N_HEADS).transpose(0, 3, 1, 2).reshape(b, N_HEADS * ds, 3)
        o_c, sel_chunks = _sample_cmp(qbd, kc_full, vc_full, slope_rows, past, nc, ns)
        o_rows = _sample_sel(page_table, sample['cache_k_sel'], sample['cache_v_sel'], qbd, sel_chunks,
                             _pad_rows(k_sel, new_pad), _pad_rows(v_sel, new_pad), sample['cache_k_win'],
                             sample['cache_v_win'], _pad_rows(k_win, LANES), _pad_rows(v_win, LANES), o_c,
                             gate_rows, slope_rows, past, l)
        buf_k, buf_v = sample['cache_k_win'][l], sample['cache_v_win'][l]
        o_b = o_rows.reshape(b, N_HEADS, ds, HEAD_DIM).transpose(0, 2, 1, 3).reshape(m, B_WIDTH).astype(BF16)
        keep = min(WINDOW, buf_k.shape[1] + ds)
        new_bk = jnp.concatenate([buf_k, k_win], axis=1)[:, -keep:]
        new_bv = jnp.concatenate([buf_v, v_win], axis=1)[:, -keep:]

    def gate_of(vec):
        return vec if sample is None else jnp.repeat(vec, t, axis=0)

    mix = _branch(o_a, o_b, p['w_branch'], proj, l, tm)
    x_mid = _resid_matmul(mix, p['w_out'], x2, gate_of(mods[2]), l, tm, 'attn_out')
    h2 = _modulate(x_mid, p['norm2_w'], mods[3], mods[4], t, l)
    if sample is None:
        s0 = s1 = None
    else:
        state = sample['state_ffn_conv'][l]
        s0 = jnp.repeat(state[:, 0], t, axis=0)
        s1 = jnp.repeat(state[:, 1], t, axis=0)
    hid, tails = _ffn_up(h2, p['ffn_w_a'], p['ffn_w_b'], p['ffn_conv_w'], p['ffn_conv_b'], s0, s1, t, l, tm)
    x_out = _resid_matmul(hid, p['ffn_w_down'], x_mid, gate_of(mods[5]), l, tm, 'ffn_down')

    if sample is None:
        conv_rows = tails.reshape(b, t // tm, SUBLANES, -1)[:, -1, SUBLANES - (CONV_W - 1):]
    else:
        conv_rows = tails.reshape(b, t, -1)[:, t - (CONV_W - 1):]
    heads = lambda z: z.reshape(b, -1, N_KV, HEAD_DIM)
    state_out = dict(k_cmp=heads(k_cmp), v_cmp=heads(v_cmp), k_sel=heads(k_sel), v_sel=heads(v_sel),
                     k_win=heads(new_bk), v_win=heads(new_bv), conv=conv_rows)
    if sample is not None:
        state_out['chunk_v'] = v_norm.reshape(b, t, A_WIDTH)
    return x_out.reshape(b, t, d), state_out


def kernel(x_prompt, x_sample, cache_k_cmp, cache_v_cmp, cache_k_sel, cache_v_sel, cache_k_win, cache_v_win,
           state_ffn_conv, page_table, c_prompt, c_sample, w_ada, b_ada, norm1_w, norm2_w, w_in, a_norm_w,
           a_spatial_w, a_spatial_b, q_norm_w, k_norm_w, cmp_pe_k, cmp_w_k, cmp_pe_v, cmp_w_v, w_branch, w_out,
           ffn_w_a, ffn_w_b, ffn_conv_w, ffn_conv_b, ffn_w_down):
    depth, d, n_in = w_in.shape
    bp = x_prompt.shape[0]
    bs = x_sample.shape[0]

    rows = -(-(bp + bs) // SUBLANES) * SUBLANES
    c_all = jnp.pad(jnp.concatenate([c_prompt, c_sample], axis=0), ((0, rows - bp - bs), (0, 0)))
    mods_all = _ada(c_all, w_ada, b_ada)

    n_merge = n_in - N_MAIN - 3 * N_HEADS
    pad = (-(n_merge + 3 * N_HEADS)) % TN
    w_gm = [jnp.concatenate([w_in[l, :, N_MAIN + 3 * N_HEADS:], w_in[l, :, N_MAIN:N_MAIN + 3 * N_HEADS],
                             jnp.zeros((d, pad), F32)], axis=1) for l in range(depth)]
    shared = dict(
        norm1_w=norm1_w, norm2_w=norm2_w, w_in=w_in, w_gm=w_gm,
        qn=jnp.tile(q_norm_w, (1, TN // HEAD_DIM))[:, None, :], kn=jnp.tile(k_norm_w, (1, TN // HEAD_DIM))[:, None, :],
        a_norm_w=a_norm_w, a_spatial_w=a_spatial_w, a_spatial_b=a_spatial_b,
        cmp=[_cmp_consts(cmp_pe_k[l], cmp_w_k[l], cmp_pe_v[l], cmp_w_v[l], k_norm_w[l]) for l in range(depth)],
        w_branch=w_branch, w_out=w_out, ffn_w_a=ffn_w_a, ffn_w_b=ffn_w_b, ffn_conv_w=ffn_conv_w,
        ffn_conv_b=ffn_conv_b, ffn_w_down=ffn_w_down)
    sample = dict(page_table=page_table, cache_k_cmp=cache_k_cmp, cache_v_cmp=cache_v_cmp, cache_k_sel=cache_k_sel,
                  cache_v_sel=cache_v_sel, cache_k_win=cache_k_win, cache_v_win=cache_v_win,
                  state_ffn_conv=state_ffn_conv)
    for name in ('cache_k_cmp', 'cache_v_cmp', 'cache_k_sel', 'cache_v_sel'):
        pool = sample[name]
        sample[name] = pool.reshape(pool.shape[0], pool.shape[1], pool.shape[2], KV_W)
    for name in ('cache_k_win', 'cache_v_win'):
        buf = sample[name]
        sample[name] = buf.reshape(buf.shape[0], buf.shape[1], buf.shape[2], KV_W)

    xp, xs = x_prompt, x_sample
    st_p, st_s = [], []
    for l in range(depth):
        mods = mods_all[l]
        mp = [mods[:bp, k * d:(k + 1) * d] for k in range(6)]
        ms = [mods[bp:bp + bs, k * d:(k + 1) * d] for k in range(6)]
        xp, sp = _layer(l, xp, mp, shared, None)
        xs, ss = _layer(l, xs, ms, shared, sample)
        st_p.append(sp)
        st_s.append(ss)

    def stack(sts, key):
        return jnp.stack([s[key] for s in sts], axis=0)

    return (xp, xs,
            stack(st_p, 'k_cmp'), stack(st_p, 'v_cmp'), stack(st_p, 'k_sel'), stack(st_p, 'v_sel'),
            stack(st_p, 'k_win'), stack(st_p, 'v_win'), stack(st_p, 'conv'),
            stack(st_s, 'k_cmp'), stack(st_s, 'v_cmp'), stack(st_s, 'k_sel'), stack(st_s, 'v_sel'),
            stack(st_s, 'k_win'), stack(st_s, 'v_win'), stack(st_s, 'chunk_v'), stack(st_s, 'conv'))
```

```python
import functools

import jax
import jax.numpy as jnp
from jax import lax
from jax.experimental import pallas as pl
from jax.experimental.pallas import tpu as pltpu

F32 = jnp.float32
BF16 = jnp.bfloat16

HEAD_DIM = 64
N_KV = 4
Q_PER_KV = 4
N_HEADS = N_KV * Q_PER_KV
KV_W = N_KV * HEAD_DIM
A_GROUPS = 8
A_GROUP_DIM = 128
A_WIDTH = A_GROUPS * A_GROUP_DIM
B_WIDTH = N_HEADS * HEAD_DIM
CHUNK = 128
CMP_BLOCK = 32
SEL_BLOCK = 64
TOP_N = 16
WINDOW = 512
Q_BLOCK = 128
PAGE_SIZE = 128
CONV_W = 3
SCALE = HEAD_DIM ** -0.5
EPS = 1e-6
NEG = -1e30
FORCE = 1e4
LOG2E = 1.4426950408889634

LANES = 128
SUBLANES = 8
MIB = 1024 * 1024

TN = 512
COL_U = 0
COL_V = A_WIDTH
COL_Q = 2 * A_WIDTH
COL_KV = COL_Q + B_WIDTH
N_MAIN = COL_KV + 6 * KV_W
N_MAIN_BLOCKS = N_MAIN // TN
SEL_PAGES = 32
SUB_ROWS = 256


def _cparams(sem, vmem_mib):
    return pltpu.CompilerParams(dimension_semantics=sem, vmem_limit_bytes=vmem_mib * MIB)


def _group_rms(x, gmat, w):
    y = x * x
    hi = y.astype(BF16)
    lo = (y - hi.astype(F32)).astype(BF16)
    ms = jnp.dot(hi, gmat, preferred_element_type=F32) + jnp.dot(lo, gmat, preferred_element_type=F32)
    return x * lax.rsqrt(ms + EPS) * w


def _group_mean_matrix(n):
    r = jnp.arange(n)
    return jnp.where((r[:, None] // HEAD_DIM) == (r[None, :] // HEAD_DIM), 1.0 / HEAD_DIM, 0.0).astype(BF16)


def _ada_kernel(c_ref, w_ref, b_ref, o_ref):
    c = c_ref[...]
    s = (c * jax.nn.sigmoid(c)).astype(BF16)
    o_ref[...] = jnp.dot(s, w_ref[...].astype(BF16), preferred_element_type=F32) + b_ref[...]


def _ada(c_all, w_ada, b_ada):
    depth, d, n6 = w_ada.shape
    rows = c_all.shape[0]
    tn = 1024
    return pl.pallas_call(
        _ada_kernel,
        grid=(depth, n6 // tn),
        in_specs=[
            pl.BlockSpec((rows, d), lambda l, j: (0, 0)),
            pl.BlockSpec((None, d, tn), lambda l, j: (l, 0, j)),
            pl.BlockSpec((None, 1, tn), lambda l, j: (l, 0, j)),
        ],
        out_specs=pl.BlockSpec((None, rows, tn), lambda l, j: (l, 0, j)),
        out_shape=jax.ShapeDtypeStruct((depth, rows, n6), F32),
        compiler_params=_cparams(("arbitrary", "arbitrary"), 40),
        name="ada",
    )(c_all, w_ada, b_ada.reshape(depth, 1, n6))


def _modulate_kernel(x_ref, g_ref, sh_ref, sc_ref, o_ref):
    x = x_ref[...]
    y = x * lax.rsqrt(jnp.mean(x * x, axis=-1, keepdims=True) + EPS)
    y = y * g_ref[...]
    o_ref[...] = (y * (1.0 + sc_ref[...]) + sh_ref[...]).astype(BF16)


def _modulate(x2, g, shift, scale, t, l):
    m, d = x2.shape
    nb = m // t
    tm = min(t, 512)
    per = t // tm
    return pl.pallas_call(
        _modulate_kernel,
        grid=(m // tm,),
        in_specs=[
            pl.BlockSpec((tm, d), lambda i: (i, 0)),
            pl.BlockSpec((None, 1, d), lambda i: (l, 0, 0)),
            pl.BlockSpec((None, 1, d), lambda i: (i // per, 0, 0)),
            pl.BlockSpec((None, 1, d), lambda i: (i // per, 0, 0)),
        ],
        out_specs=pl.BlockSpec((tm, d), lambda i: (i, 0)),
        out_shape=jax.ShapeDtypeStruct((m, d), BF16),
        compiler_params=_cparams(("arbitrary",), 32),
        name="modulate",
    )(x2, g.reshape(g.shape[0], 1, d), shift.reshape(nb, 1, d), scale.reshape(nb, 1, d))


def _inproj_kernel(h_ref, w1_ref, w2_ref, qn_ref, kn_ref, gm_ref, o_ref, wbf_ref):
    j = pl.program_id(0)
    i = pl.program_id(1)

    @pl.when((i == 0) & (j < N_MAIN_BLOCKS))
    def _():
        wbf_ref[...] = w1_ref[...].T.astype(BF16)

    @pl.when((i == 0) & (j >= N_MAIN_BLOCKS))
    def _():
        wbf_ref[...] = w2_ref[...].T.astype(BF16)

    tm = h_ref.shape[0]
    sub = min(tm, SUB_ROWS)

    def project(epilogue):
        for c in range(tm // sub):
            rows = slice(c * sub, (c + 1) * sub)
            o_ref[rows, :] = epilogue(jnp.dot(h_ref[rows, :], wbf_ref[...], preferred_element_type=F32))

    @pl.when(j < 4)
    def _():
        project(jax.nn.gelu)

    @pl.when((j == 4) | (j == 5))
    def _():
        project(lambda acc: _group_rms(acc, gm_ref[...], qn_ref[...]))

    @pl.when(j == 6)
    def _():
        project(lambda acc: acc)

    @pl.when((j == 7) | (j == 8))
    def _():
        lane = lax.broadcasted_iota(jnp.int32, (sub, TN), 1)
        project(lambda acc: jnp.where(lane < KV_W, _group_rms(acc, gm_ref[...], kn_ref[...]), acc))

    @pl.when(j >= N_MAIN_BLOCKS)
    def _():
        project(jax.nn.sigmoid)


def _inproj(h, w_in_t, w_gm_t, qn, kn, l, tm):
    m, d = h.shape
    n_gm = w_gm_t.shape[0] // TN
    nj = N_MAIN_BLOCKS + n_gm
    gm = _group_mean_matrix(TN)
    return pl.pallas_call(
        _inproj_kernel,
        grid=(nj, m // tm),
        in_specs=[
            pl.BlockSpec((tm, d), lambda j, i: (i, 0)),
            pl.BlockSpec((None, TN, d), lambda j, i: (l, jnp.minimum(j, N_MAIN_BLOCKS - 1), 0)),
            pl.BlockSpec((TN, d), lambda j, i: (jnp.maximum(j - N_MAIN_BLOCKS, 0), 0)),
            pl.BlockSpec((1, TN), lambda j, i: (0, 0)),
            pl.BlockSpec((1, TN), lambda j, i: (0, 0)),
            pl.BlockSpec((TN, TN), lambda j, i: (0, 0)),
        ],
        out_specs=pl.BlockSpec((tm, TN), lambda j, i: (i, j)),
        out_shape=jax.ShapeDtypeStruct((m, nj * TN), F32),
        scratch_shapes=[pltpu.VMEM((d, TN), BF16)],
        compiler_params=_cparams(("arbitrary", "arbitrary"), 48),
        name="inproj",
    )(h, w_in_t, w_gm_t, qn, kn, gm)


def _cast_kernel(x_ref, o_ref):
    o_ref[...] = x_ref[...].astype(BF16)


def _cast_cols(x2, col0, width, tm):
    m = x2.shape[0]
    return pl.pallas_call(
        _cast_kernel,
        grid=(m // tm,),
        in_specs=[pl.BlockSpec((tm, width), lambda i: (i, col0 // width))],
        out_specs=pl.BlockSpec((tm, width), lambda i: (i, 0)),
        out_shape=jax.ShapeDtypeStruct((m, width), BF16),
        compiler_params=_cparams(("arbitrary",), 32),
        name="cast_cols",
    )(x2)


def _chunkmix_kernel(u_ref, gv_ref, w_ref, bt_ref, an_ref, oa_ref, v_ref, *, lc, causal_block):
    gv = gv_ref[...]
    v = gv * lax.rsqrt(jnp.mean(gv * gv, axis=-1, keepdims=True) + EPS) * an_ref[...]
    v_ref[...] = v
    row = lax.broadcasted_iota(jnp.int32, (lc, lc), 0)
    col = lax.broadcasted_iota(jnp.int32, (lc, lc), 1)
    shift = causal_block.bit_length() - 1
    mask = (col <= row) & ((row >> shift) == (col >> shift))
    vb = v.astype(BF16)
    for g in range(A_GROUPS):
        sl = slice(g * A_GROUP_DIM, (g + 1) * A_GROUP_DIM)
        wg = jnp.where(mask, w_ref[g], 0.0).astype(BF16)
        s = jnp.dot(wg, vb[:, sl], preferred_element_type=F32) + bt_ref[:, g:g + 1]
        oa_ref[:, sl] = (u_ref[:, sl] * s).astype(BF16)


def _chunkmix(proj, w_s, b_t, a_norm, lc, causal_block):
    m = proj.shape[0]
    return pl.pallas_call(
        functools.partial(_chunkmix_kernel, lc=lc, causal_block=causal_block),
        grid=(m // lc,),
        in_specs=[
            pl.BlockSpec((lc, A_WIDTH), lambda c: (c, 0)),
            pl.BlockSpec((lc, A_WIDTH), lambda c: (c, 1)),
            pl.BlockSpec((A_GROUPS, lc, lc), lambda c: (0, 0, 0)),
            pl.BlockSpec((lc, A_GROUPS), lambda c: (0, 0)),
            pl.BlockSpec((1, A_WIDTH), lambda c: (0, 0)),
        ],
        out_specs=[
            pl.BlockSpec((lc, A_WIDTH), lambda c: (c, 0)),
            pl.BlockSpec((lc, A_WIDTH), lambda c: (c, 0)),
        ],
        out_shape=[jax.ShapeDtypeStruct((m, A_WIDTH), BF16), jax.ShapeDtypeStruct((m, A_WIDTH), F32)],
        compiler_params=_cparams(("arbitrary",), 32),
        name="chunkmix",
    )(proj, proj, w_s, b_t, a_norm)


def _compress_rows(half_refs, n_blk, bd_ref, pe_ref):
    halves = []
    for ref in half_refs:
        acc = jnp.zeros((n_blk, LANES), F32)
        for i in range(CMP_BLOCK):
            rows = ref[pl.ds(i, n_blk, stride=CMP_BLOCK), :] + pe_ref[i:i + 1, :]
            acc = acc + jnp.dot(rows.astype(BF16), bd_ref[i], preferred_element_type=F32)
        halves.append(acc)
    return jnp.concatenate(halves, axis=1)


def _compress_kernel(k0_ref, k1_ref, v0_ref, v1_ref, bdk_ref, bdv_ref, pek_ref, pev_ref, knw_ref, gm_ref,
                     kc_ref, vc_ref, *, nc):
    kc_ref[...] = _group_rms(_compress_rows((k0_ref, k1_ref), nc, bdk_ref, pek_ref), gm_ref[...], knw_ref[...])
    vc_ref[...] = _compress_rows((v0_ref, v1_ref), nc, bdv_ref, pev_ref)


def _compress(src3, kblk, vblk, cmp_consts):
    b, t, _ = src3.shape
    nc = t // CMP_BLOCK
    bdk, bdv, pek, pev, knw, gm = cmp_consts
    const3 = lambda i: (0, 0, 0)
    const2 = lambda i: (0, 0)
    half = lambda blk, h: pl.BlockSpec((None, t, LANES), lambda i: (i, 0, 2 * blk + h))
    return pl.pallas_call(
        functools.partial(_compress_kernel, nc=nc),
        grid=(b,),
        in_specs=[
            half(kblk, 0), half(kblk, 1), half(vblk, 0), half(vblk, 1),
            pl.BlockSpec((CMP_BLOCK, LANES, LANES), const3),
            pl.BlockSpec((CMP_BLOCK, LANES, LANES), const3),
            pl.BlockSpec((CMP_BLOCK, LANES), const2),
            pl.BlockSpec((CMP_BLOCK, LANES), const2),
            pl.BlockSpec((1, KV_W), const2),
            pl.BlockSpec((KV_W, KV_W), const2),
        ],
        out_specs=[pl.BlockSpec((None, nc, KV_W), lambda i: (i, 0, 0))] * 2,
        out_shape=[jax.ShapeDtypeStruct((b, nc, KV_W), F32)] * 2,
        compiler_params=_cparams(("arbitrary",), 40),
        name="compress",
    )(src3, src3, src3, src3, bdk, bdv, pek, pev, knw, gm)


def _cmp_consts(cmp_pe_k, cmp_w_k, cmp_pe_v, cmp_w_v, k_norm_w):
    per = LANES // HEAD_DIM
    eye = jnp.eye(per, dtype=F32)

    def bd(w):
        return jnp.einsum('gh,ide->igdhe', eye, w).reshape(CMP_BLOCK, LANES, LANES).astype(BF16)

    return (bd(cmp_w_k), bd(cmp_w_v), jnp.tile(cmp_pe_k, (1, per)), jnp.tile(cmp_pe_v, (1, per)),
            jnp.tile(k_norm_w, N_KV)[None, :], _group_mean_matrix(KV_W))


def _pair_sum(psum, pair):
    hi = psum.astype(BF16)
    lo = (psum - hi.astype(F32)).astype(BF16)
    return jnp.dot(hi, pair, preferred_element_type=F32) + jnp.dot(lo, pair, preferred_element_type=F32)


def _top_mask(score, n_valid, top_n):
    rows, width = score.shape
    jj = lax.broadcasted_iota(jnp.int32, (rows, width), 1)
    rank = jnp.zeros((rows, width), F32)
    for k in range(n_valid):
        ck = score[:, k:k + 1]
        beats = (ck > score) | ((ck == score) & (jj > k))
        rank = rank + jnp.where(beats, 1.0, 0.0)
    return jnp.where((rank < top_n) & (jj < n_valid), 1.0, 0.0)


def _softmax_rows(s, mask):
    s = jnp.where(mask, s, NEG)
    m = jnp.max(s, axis=-1, keepdims=True)
    p = jnp.where(mask, jnp.exp(s - m), 0.0)
    return p / jnp.maximum(jnp.sum(p, axis=-1, keepdims=True), 1e-30)


def _softmax2_cols(s2, mask):
    s2 = jnp.where(mask, s2, NEG)
    m = jnp.max(s2, axis=0, keepdims=True)
    p = jnp.where(mask, jnp.exp2(s2 - m), 0.0)
    return p / jnp.maximum(jnp.sum(p, axis=0, keepdims=True), 1e-30)


def _nsa_prompt_kernel(slopes2_ref, qt_ref, kc_ref, vct_ref, ks_ref, vst_ref, kw_ref, vwt_ref, ngt_ref, bias_ref,
                       o_ref, qbd_sc, sel_sc, m_sc, l_sc, acc_sc, *, t, tk):
    i = pl.program_id(1)
    nc = t // CMP_BLOCK
    ns = t // SEL_BLOCK
    half = nc // 2
    start = i * Q_BLOCK
    groups = range(N_KV)
    lane_blk = [slice(r * Q_BLOCK, (r + 1) * Q_BLOCK) for r in range(Q_PER_KV)]
    slopes2 = [[slopes2_ref[g, r] for r in range(Q_PER_KV)] for g in groups]
    qpos = start + lax.broadcasted_iota(jnp.int32, (1, Q_BLOCK), 1)

    qbd_sc[...] = jnp.zeros(qbd_sc.shape, BF16)
    for g in groups:
        qbd_sc[g, g * HEAD_DIM:(g + 1) * HEAD_DIM, :] = (qt_ref[g] * (SCALE * LOG2E)).astype(BF16)

    row = lax.broadcasted_iota(jnp.int32, (nc, 1), 0)
    blk_c = jnp.where(row < half, 2 * row, 2 * (row - half) + 1)
    dist_c = qpos - ((blk_c + 1) * CMP_BLOCK - 1)
    mask_c = dist_c >= 0
    dist_cf = dist_c.astype(F32)
    jj = lax.broadcasted_iota(jnp.int32, (ns, 1), 0)
    cur = qpos >> 6
    forced = (jj == 0) | (jj == cur) | (jj == cur - 1)
    kc = kc_ref[...]
    o_c = []
    for g in groups:
        st = jnp.dot(kc, qbd_sc[g], preferred_element_type=F32)
        p_heads = [_softmax2_cols(st[:, lane_blk[r]] - slopes2[g][r] * dist_cf, mask_c) for r in range(Q_PER_KV)]
        o_c.append(jnp.dot(vct_ref[g], jnp.concatenate(p_heads, axis=1).astype(BF16), preferred_element_type=F32))
        psum = ((p_heads[0] + p_heads[1]) + p_heads[2]) + p_heads[3]
        imp = psum[:half] + psum[half:]
        score = jnp.where(jj <= cur, imp + jnp.where(forced, FORCE, 0.0), NEG)
        rank = jnp.zeros((ns, Q_BLOCK), F32)
        for k in range(ns):
            sk = score[k:k + 1, :]
            lo = (k // SUBLANES) * SUBLANES
            hi = lo + SUBLANES
            one = lambda cond: jnp.where(cond, 1.0, 0.0)
            rank = rank + jnp.concatenate(
                ([one(sk > score[:lo])] if lo else [])
                + [jnp.where(jj[lo:hi] > k, one(sk >= score[lo:hi]), one(sk > score[lo:hi]))]
                + ([one(sk >= score[hi:])] if hi < ns else []), axis=0)
        sel_sc[g] = jnp.where(rank < min(TOP_N, ns), 0.0, NEG)

    m_sc[...] = jnp.full(m_sc.shape, NEG, F32)
    l_sc[...] = jnp.zeros(l_sc.shape, F32)
    acc_sc[...] = jnp.zeros(acc_sc.shape, F32)
    blocks_per_tile = tk // SEL_BLOCK
    last = start // tk

    def sel_tile(ti, causal):
        k0 = pl.multiple_of(ti * tk, tk)
        kt = ks_ref[pl.ds(k0, tk), :]
        off = start - k0
        offf = off.astype(F32)
        if causal:
            rel = lax.broadcasted_iota(jnp.int32, (tk, Q_BLOCK), 1) - lax.broadcasted_iota(jnp.int32, (tk, Q_BLOCK), 0)
            future = rel + off < 0
        for g in groups:
            s = jnp.dot(kt, qbd_sc[g], preferred_element_type=F32)
            sel_rows = sel_sc[g, pl.ds(pl.multiple_of(ti * blocks_per_tile, blocks_per_tile), blocks_per_tile), :]
            mask_bias = jnp.concatenate(
                [jnp.broadcast_to(sel_rows[jb:jb + 1, :], (SEL_BLOCK, Q_BLOCK)) for jb in range(blocks_per_tile)],
                axis=0)
            if causal:
                mask_bias = jnp.where(future, NEG, mask_bias)
            m_old = m_sc[g]
            p_parts, m_parts, sum_parts = [], [], []
            for r in range(Q_PER_KV):
                c_r = slopes2[g][r] * offf
                sr = (s[:, lane_blk[r]] - bias_ref[g, r, 0:tk, :]) + mask_bias
                m_new = jnp.maximum(m_old[:, lane_blk[r]], jnp.max(sr, axis=0, keepdims=True) - c_r)
                p = jnp.exp2(sr - (m_new + c_r))
                m_parts.append(m_new)
                sum_parts.append(jnp.sum(p, axis=0, keepdims=True))
                p_parts.append(p.astype(BF16))
            m_new = jnp.concatenate(m_parts, axis=1)
            alpha = jnp.exp2(m_old - m_new)
            l_sc[g] = alpha * l_sc[g] + jnp.concatenate(sum_parts, axis=1)
            pv = jnp.dot(vst_ref[g, :, pl.ds(k0, tk)], jnp.concatenate(p_parts, axis=1), preferred_element_type=F32)
            acc_sc[g] = alpha * acc_sc[g] + pv
            m_sc[g] = m_new

    def full_tile(ti, carry):
        sel_tile(ti, False)
        return carry

    lax.fori_loop(0, last, full_tile, 0)
    sel_tile(last, True)

    wk = WINDOW + Q_BLOCK
    k0w = pl.multiple_of(jnp.maximum(start - WINDOW, 0), Q_BLOCK)
    kwt = kw_ref[pl.ds(k0w, wk), :]
    dist_w = qpos - (k0w + lax.broadcasted_iota(jnp.int32, (wk, 1), 0))
    mask_w = (dist_w >= 0) & (dist_w < WINDOW)
    for g in groups:
        sw = jnp.dot(kwt, qbd_sc[g], preferred_element_type=F32)
        pw = [_softmax2_cols(sw[:, lane_blk[r]] - bias_ref[g, r, 0:wk, :], mask_w) for r in range(Q_PER_KV)]
        o_w = jnp.dot(vwt_ref[g, :, pl.ds(k0w, wk)], jnp.concatenate(pw, axis=1).astype(BF16),
                      preferred_element_type=F32)
        o_s = acc_sc[g] / jnp.maximum(l_sc[g], 1e-30)
        ng = ngt_ref[g]
        o_ref[g] = (ng[0:1, :] * o_c[g] + ng[1:2, :] * o_s + ng[2:3, :] * o_w).astype(BF16)


def _pair_matrix(nc, ns_pad):
    n = jnp.arange(nc)[:, None]
    j = jnp.arange(ns_pad)[None, :]
    return jnp.where((n // (SEL_BLOCK // CMP_BLOCK)) == j, 1.0, 0.0).astype(BF16)


def _nsa_prompt(slopes, q_t, kc_perm, vc_t, kv_bf, vs_t, vw_t, ng_t):
    b, _, nqb, _, cols = q_t.shape
    t = nqb * Q_BLOCK
    nc = t // CMP_BLOCK
    ns = t // SEL_BLOCK
    tk = min(512, t)
    wk = WINDOW + Q_BLOCK
    slopes2 = slopes * LOG2E
    rel = (jnp.arange(Q_BLOCK)[None, :] - jnp.arange(wk)[:, None]).astype(F32)
    bias_tab = slopes2[:, :, None, None] * rel[None, None]
    blk5 = lambda rows: pl.BlockSpec((None, N_KV, None, rows, cols), lambda bi, i: (bi, 0, i, 0, 0))
    vt_spec = lambda n: pl.BlockSpec((None, N_KV, HEAD_DIM, n), lambda bi, i: (bi, 0, 0, 0))
    kv_spec = lambda col: pl.BlockSpec((None, t, KV_W), lambda bi, i: (bi, 0, col))
    return pl.pallas_call(
        functools.partial(_nsa_prompt_kernel, t=t, tk=tk),
        grid=(b, nqb),
        in_specs=[
            pl.BlockSpec(memory_space=pltpu.SMEM),
            blk5(HEAD_DIM),
            pl.BlockSpec((None, nc, KV_W), lambda bi, i: (bi, 0, 0)),
            vt_spec(nc), kv_spec(2), vt_spec(t), kv_spec(4), vt_spec(t),
            blk5(3),
            pl.BlockSpec((N_KV, Q_PER_KV, wk, Q_BLOCK), lambda bi, i: (0, 0, 0, 0)),
        ],
        out_specs=blk5(HEAD_DIM),
        out_shape=jax.ShapeDtypeStruct((b, N_KV, nqb, HEAD_DIM, cols), BF16),
        scratch_shapes=[pltpu.VMEM((N_KV, KV_W, cols), BF16), pltpu.VMEM((N_KV, ns, Q_BLOCK), F32),
                        pltpu.VMEM((N_KV, 1, cols), F32), pltpu.VMEM((N_KV, 1, cols), F32),
                        pltpu.VMEM((N_KV, HEAD_DIM, cols), F32)],
        compiler_params=_cparams(("arbitrary", "arbitrary"), 56),
        name="nsa_prompt",
    )(slopes2, q_t, kc_perm, vc_t, kv_bf, vs_t, kv_bf, vw_t, ng_t, bias_tab)


def _branch_kernel(oa_ref, ob_ref, w_ref, g0_ref, g1_ref, o_ref, wbf_ref):
    @pl.when(pl.program_id(1) == 0)
    def _():
        wbf_ref[...] = w_ref[...].astype(BF16)

    pa = jnp.dot(oa_ref[...], wbf_ref[:A_WIDTH, :], preferred_element_type=F32)
    pb = jnp.dot(ob_ref[...], wbf_ref[A_WIDTH:, :], preferred_element_type=F32)
    o_ref[...] = (g0_ref[...] * pa + g1_ref[...] * pb).astype(BF16)


def _branch(o_a, o_b, w_branch, proj, l, tm):
    m = o_a.shape[0]
    kb, d = w_branch.shape[1:]
    nj = d // TN
    return pl.pallas_call(
        _branch_kernel,
        grid=(nj, m // tm),
        in_specs=[
            pl.BlockSpec((tm, A_WIDTH), lambda j, i: (i, 0)),
            pl.BlockSpec((tm, B_WIDTH), lambda j, i: (i, 0)),
            pl.BlockSpec((None, kb, TN), lambda j, i: (l, 0, j)),
            pl.BlockSpec((tm, TN), lambda j, i: (i, N_MAIN_BLOCKS + j)),
            pl.BlockSpec((tm, TN), lambda j, i: (i, N_MAIN_BLOCKS + nj + j)),
        ],
        out_specs=pl.BlockSpec((tm, TN), lambda j, i: (i, j)),
        out_shape=jax.ShapeDtypeStruct((m, d), BF16),
        scratch_shapes=[pltpu.VMEM((kb, TN), BF16)],
        compiler_params=_cparams(("arbitrary", "arbitrary"), 48),
        name="branch",
    )(o_a, o_b, w_branch, proj, proj)


def _resid_kernel(a_ref, w_ref, x_ref, g_ref, o_ref, wbf_ref):
    @pl.when(pl.program_id(1) == 0)
    def _():
        wbf_ref[...] = w_ref[...].astype(BF16)

    o_ref[...] = x_ref[...] + g_ref[...] * jnp.dot(a_ref[...], wbf_ref[...], preferred_element_type=F32)


def _resid_matmul(a, w, x2, gate, l, tm, name):
    m, k = a.shape
    d = w.shape[2]
    if gate.shape[0] == m:
        gate_spec = pl.BlockSpec((tm, TN), lambda j, i: (i, j))
    else:
        nb = gate.shape[0]
        per = (m // nb) // tm
        gate = gate.reshape(nb, 1, d)
        gate_spec = pl.BlockSpec((None, 1, TN), lambda j, i: (i // per, 0, j))
    return pl.pallas_call(
        _resid_kernel,
        grid=(d // TN, m // tm),
        in_specs=[
            pl.BlockSpec((tm, k), lambda j, i: (i, 0)),
            pl.BlockSpec((None, k, TN), lambda j, i: (l, 0, j), pipeline_mode=pl.Buffered(1)),
            pl.BlockSpec((tm, TN), lambda j, i: (i, j)),
            gate_spec,
        ],
        out_specs=pl.BlockSpec((tm, TN), lambda j, i: (i, j)),
        out_shape=jax.ShapeDtypeStruct((m, d), F32),
        scratch_shapes=[pltpu.VMEM((k, TN), BF16)],
        compiler_params=_cparams(("arbitrary", "arbitrary"), 56),
        name=name,
    )(a, w, x2, gate)


def _ffn_up_kernel(h_ref, wa_ref, wb_ref, cw_ref, cb_ref, s0_ref, s1_ref, o_ref, tail_ref,
                   wa_bf, wb_bf, abuf, *, tm, seg, per_batch, keep):
    i = pl.program_id(1)

    @pl.when(i == 0)
    def _():
        wa_bf[...] = wa_ref[...].astype(BF16)
        wb_bf[...] = wb_ref[...].astype(BF16)

    @pl.when(i % per_batch == 0)
    def _():
        abuf[0:SUBLANES, :] = jnp.zeros((SUBLANES, TN), F32)

    sub = min(tm, SUB_ROWS)
    for c in range(tm // sub):
        rows = slice(c * sub, (c + 1) * sub)
        h = h_ref[rows, :]
        a = jnp.dot(h, wa_bf[...], preferred_element_type=F32)
        b = jnp.dot(h, wb_bf[...], preferred_element_type=F32)
        abuf[SUBLANES + c * sub:SUBLANES + (c + 1) * sub, :] = a
        a1 = abuf[SUBLANES - 1 + c * sub:SUBLANES - 1 + (c + 1) * sub, :]
        a2 = abuf[SUBLANES - 2 + c * sub:SUBLANES - 2 + (c + 1) * sub, :]
        if seg < tm:
            r = lax.broadcasted_iota(jnp.int32, (sub, 1), 0) & (seg - 1)
            a1 = jnp.where(r == 0, s1_ref[rows, :], a1)
            a2 = jnp.where(r == 0, s0_ref[rows, :], jnp.where(r == 1, s1_ref[rows, :], a2))
        conv = a2 * cw_ref[0:1, :] + a1 * cw_ref[1:2, :] + a * cw_ref[2:3, :] + cb_ref[...]
        o_ref[rows, :] = (jax.nn.gelu(conv) * b).astype(BF16)
    tail_ref[...] = abuf[SUBLANES + tm - keep:SUBLANES + tm, :]
    abuf[0:SUBLANES, :] = abuf[tm:tm + SUBLANES, :]


def _ffn_up(h2, w_a, w_b, conv_w, conv_b, s0, s1, t, l, tm):
    m, d = h2.shape
    dff = w_a.shape[2]
    per_batch = max(t // tm, 1)
    seg = min(t, tm)
    keep = SUBLANES if seg == tm else tm
    if s0 is None:
        s0 = s1 = jnp.zeros((SUBLANES, dff), F32)
        s_spec = pl.BlockSpec((SUBLANES, TN), lambda j, i: (0, j))
    else:
        s_spec = pl.BlockSpec((tm, TN), lambda j, i: (i, j))
    w_spec = pl.BlockSpec((None, d, TN), lambda j, i: (l, 0, j))
    return pl.pallas_call(
        functools.partial(_ffn_up_kernel, tm=tm, seg=seg, per_batch=per_batch, keep=keep),
        grid=(dff // TN, m // tm),
        in_specs=[
            pl.BlockSpec((tm, d), lambda j, i: (i, 0)),
            w_spec, w_spec,
            pl.BlockSpec((None, CONV_W, TN), lambda j, i: (l, 0, j)),
            pl.BlockSpec((None, 1, TN), lambda j, i: (l, 0, j)),
            s_spec, s_spec,
        ],
        out_specs=[
            pl.BlockSpec((tm, TN), lambda j, i: (i, j)),
            pl.BlockSpec((None, keep, TN), lambda j, i: (i, 0, j)),
        ],
        out_shape=[jax.ShapeDtypeStruct((m, dff), BF16), jax.ShapeDtypeStruct((m // tm, keep, dff), F32)],
        scratch_shapes=[pltpu.VMEM((d, TN), BF16), pltpu.VMEM((d, TN), BF16), pltpu.VMEM((tm + SUBLANES, TN), F32)],
        compiler_params=_cparams(("arbitrary", "arbitrary"), 56),
        name="ffn_up",
    )(h2, w_a, w_b, conv_w, conv_b.reshape(conv_b.shape[0], 1, dff), s0, s1)


def _gather_pages(pt_ref, b, c, pool_ref, l, buf_ref, sem, split_lanes):
    copies = []
    for p in range(SEL_PAGES):
        page = pt_ref[b, c * SEL_PAGES + p]
        dst_rows = pl.ds(p * PAGE_SIZE, PAGE_SIZE)
        if split_lanes:
            for h in range(KV_W // LANES):
                copies.append(pltpu.make_async_copy(pool_ref.at[l, page, :, pl.ds(h * LANES, LANES)],
                                                    buf_ref.at[h, dst_rows, :], sem))
        else:
            copies.append(pltpu.make_async_copy(pool_ref.at[l, page], buf_ref.at[dst_rows, :], sem))
    return copies


def _gather_compress_kernel(pt_ref, poolk_ref, poolv_ref, bdk_ref, bdv_ref, pek_ref, pev_ref, knw_ref, gm_ref,
                            kc_ref, vc_ref, kbuf, vbuf, sems, *, l):
    b = pl.program_id(0)
    c = pl.program_id(1)
    kcopies = _gather_pages(pt_ref, b, c, poolk_ref, l, kbuf, sems.at[0], True)
    vcopies = _gather_pages(pt_ref, b, c, poolv_ref, l, vbuf, sems.at[1], True)
    for cp in kcopies + vcopies:
        cp.start()
    n_blk = SEL_PAGES * PAGE_SIZE // CMP_BLOCK
    for cp in kcopies:
        cp.wait()
    kc_ref[...] = _group_rms(_compress_rows((kbuf.at[0], kbuf.at[1]), n_blk, bdk_ref, pek_ref),
                             gm_ref[...], knw_ref[...])
    for cp in vcopies:
        cp.wait()
    vc_ref[...] = _compress_rows((vbuf.at[0], vbuf.at[1]), n_blk, bdv_ref, pev_ref)


def _gather_compress(page_table, pool_k, pool_v, cmp_consts, l):
    bd, n_pages = page_table.shape
    n_chunks = n_pages // SEL_PAGES
    n_blk = SEL_PAGES * PAGE_SIZE // CMP_BLOCK
    bdk, bdv, pek, pev, knw, gm = cmp_consts
    const3 = lambda b, c, pt: (0, 0, 0)
    const2 = lambda b, c, pt: (0, 0)
    grid_spec = pltpu.PrefetchScalarGridSpec(
        num_scalar_prefetch=1,
        grid=(bd, n_chunks),
        in_specs=[
            pl.BlockSpec(memory_space=pl.ANY),
            pl.BlockSpec(memory_space=pl.ANY),
            pl.BlockSpec((CMP_BLOCK, LANES, LANES), const3),
            pl.BlockSpec((CMP_BLOCK, LANES, LANES), const3),
            pl.BlockSpec((CMP_BLOCK, LANES), const2),
            pl.BlockSpec((CMP_BLOCK, LANES), const2),
            pl.BlockSpec((1, KV_W), const2),
            pl.BlockSpec((KV_W, KV_W), const2),
        ],
        out_specs=[pl.BlockSpec((None, n_blk, KV_W), lambda b, c, pt: (b, c, 0))] * 2,
        scratch_shapes=[
            pltpu.VMEM((KV_W // LANES, SEL_PAGES * PAGE_SIZE, LANES), F32),
            pltpu.VMEM((KV_W // LANES, SEL_PAGES * PAGE_SIZE, LANES), F32),
            pltpu.SemaphoreType.DMA((2,)),
        ],
    )
    return pl.pallas_call(
        functools.partial(_gather_compress_kernel, l=l),
        grid_spec=grid_spec,
        out_shape=[jax.ShapeDtypeStruct((bd, n_chunks * n_blk, KV_W), F32)] * 2,
        compiler_params=_cparams(("arbitrary", "arbitrary"), 48),
        name="gather_compress",
    )(page_table, pool_k, pool_v, bdk, bdv, pek, pev, knw, gm)


def _fold_groups(x):
    rows = x.shape[0]
    grp = lax.broadcasted_iota(jnp.int32, (rows, 1), 0) >> 5
    out = jnp.zeros((rows, HEAD_DIM), F32)
    for g in range(N_KV):
        out = out + jnp.where(grp == g, x[:, g * HEAD_DIM:(g + 1) * HEAD_DIM], 0.0)
    return out


def _nt_dot(a, b):
    return lax.dot_general(a, b, (((1,), (1,)), ((), ())), preferred_element_type=F32)


def _sample_cmp_kernel(q_ref, kc_ref, vc_ref, slope_ref, pair_ref, oc_ref, sel_ref, *, past, nc, ns, ns_pad):
    rows = N_HEADS * SUBLANES
    q = (q_ref[...] * SCALE).astype(BF16)
    ncp = kc_ref.shape[0]
    s = _nt_dot(q, kc_ref[...].astype(BF16))
    qpos = past + (lax.broadcasted_iota(jnp.int32, (rows, 1), 0) & (SUBLANES - 1))
    n = lax.broadcasted_iota(jnp.int32, (1, ncp), 1)
    dist = qpos - ((n + 1) * CMP_BLOCK - 1)
    mask = (dist >= 0) & (n < nc)
    p = _softmax_rows(s - slope_ref[...] * dist.astype(F32), mask)
    oc_ref[...] = _fold_groups(jnp.dot(p.astype(BF16), vc_ref[...].astype(BF16), preferred_element_type=F32))

    parts = []
    for g in range(N_KV):
        acc = p[(g * Q_PER_KV) * SUBLANES:(g * Q_PER_KV + 1) * SUBLANES]
        for r in range(1, Q_PER_KV):
            acc = acc + p[(g * Q_PER_KV + r) * SUBLANES:(g * Q_PER_KV + r + 1) * SUBLANES]
        parts.append(acc)
    psum = jnp.concatenate(parts, axis=0)
    imp = _pair_sum(psum, pair_ref[...])
    gq = N_KV * SUBLANES
    jj = lax.broadcasted_iota(jnp.int32, (gq, ns_pad), 1)
    cur = (past + (lax.broadcasted_iota(jnp.int32, (gq, 1), 0) & (SUBLANES - 1))) >> 6
    forced = (jj == 0) | (jj == cur) | (jj == cur - 1)
    score = jnp.where(jj <= cur, imp + jnp.where(forced, FORCE, 0.0), NEG)
    sel = _top_mask(score, ns, min(TOP_N, ns))
    pieces = []
    for g in range(N_KV):
        for r in range(Q_PER_KV):
            pieces.append(sel[g * SUBLANES:(g + 1) * SUBLANES])
    sel_rows = jnp.concatenate(pieces, axis=0).astype(BF16)
    per = SEL_PAGES * PAGE_SIZE // SEL_BLOCK
    for c in range(ns_pad // per):
        sel_ref[c] = sel_rows[:, c * per:(c + 1) * per]


def _sample_cmp(qbd, kc_full, vc_full, slope_rows, past, nc, ns):
    bd, rows, _ = qbd.shape
    ncp = kc_full.shape[1]
    per = SEL_PAGES * PAGE_SIZE // SEL_BLOCK
    ns_pad = -(-ns // per) * per
    pair = _pair_matrix(ncp, ns_pad)
    return pl.pallas_call(
        functools.partial(_sample_cmp_kernel, past=past, nc=nc, ns=ns, ns_pad=ns_pad),
        grid=(bd,),
        in_specs=[
            pl.BlockSpec((None, rows, KV_W), lambda b: (b, 0, 0)),
            pl.BlockSpec((None, ncp, KV_W), lambda b: (b, 0, 0)),
            pl.BlockSpec((None, ncp, KV_W), lambda b: (b, 0, 0)),
            pl.BlockSpec((rows, 1), lambda b: (0, 0)),
            pl.BlockSpec((ncp, ns_pad), lambda b: (0, 0)),
        ],
        out_specs=[
            pl.BlockSpec((None, rows, HEAD_DIM), lambda b: (b, 0, 0)),
            pl.BlockSpec((None, ns_pad // per, rows, per), lambda b: (b, 0, 0, 0)),
        ],
        out_shape=[jax.ShapeDtypeStruct((bd, rows, HEAD_DIM), F32),
                   jax.ShapeDtypeStruct((bd, ns_pad // per, rows, per), BF16)],
        compiler_params=_cparams(("arbitrary",), 40),
        name="sample_cmp",
    )(qbd, kc_full, vc_full, slope_rows, pair)


def _sample_sel_kernel(pt_ref, poolk_ref, poolv_ref, q_ref, selc_ref, seln_ref, kn_ref, vn_ref, bk_ref, bv_ref,
                       kwn_ref, vwn_ref, oc_ref, gate_ref, slope_ref, o_ref,
                       kbuf, vbuf, sems, m_sc, l_sc, acc_sc, *, past, n_chunks, l):
    b = pl.program_id(0)
    c = pl.program_id(1)
    rows = N_HEADS * SUBLANES
    kcopies = _gather_pages(pt_ref, b, c, poolk_ref, l, kbuf, sems.at[0], False)
    vcopies = _gather_pages(pt_ref, b, c, poolv_ref, l, vbuf, sems.at[1], False)
    for cp in kcopies + vcopies:
        cp.start()

    @pl.when(c == 0)
    def _():
        m_sc[...] = jnp.full(m_sc.shape, NEG, F32)
        l_sc[...] = jnp.zeros(l_sc.shape, F32)
        acc_sc[...] = jnp.zeros(acc_sc.shape, F32)

    q = (q_ref[...] * SCALE).astype(BF16)
    qpos = past + (lax.broadcasted_iota(jnp.int32, (rows, 1), 0) & (SUBLANES - 1))
    slope = slope_ref[...]

    def online_update(s, mask, v_bf):
        s = jnp.where(mask, s, NEG)
        m_old = m_sc[...]
        m_new = jnp.maximum(m_old, jnp.max(s, axis=-1, keepdims=True))
        alpha = jnp.exp(m_old - m_new)
        p = jnp.where(mask, jnp.exp(s - m_new), 0.0)
        l_sc[...] = alpha * l_sc[...] + jnp.sum(p, axis=-1, keepdims=True)
        acc_sc[...] = alpha * acc_sc[...] + jnp.dot(p.astype(BF16), v_bf, preferred_element_type=F32)
        m_sc[...] = m_new

    def sel_scores(k_bf, pos0, sel_blocks):
        nk = k_bf.shape[0]
        nb = sel_blocks.shape[1]
        kidx = lax.broadcasted_iota(jnp.int32, (1, nk), 1)
        blk = lax.broadcasted_iota(jnp.int32, (nb, nk), 0)
        expand = jnp.where((kidx >> 6) == blk, 1.0, 0.0).astype(BF16)
        chosen = jnp.dot(sel_blocks, expand, preferred_element_type=F32)
        dist = qpos - (pos0 + kidx)
        mask = (dist >= 0) & (chosen > 0.5)
        return _nt_dot(q, k_bf) - slope * dist.astype(F32), mask

    for cp in kcopies:
        cp.wait()
    s, mask = sel_scores(kbuf[...].astype(BF16), c * (SEL_PAGES * PAGE_SIZE), selc_ref[...])
    for cp in vcopies:
        cp.wait()
    online_update(s, mask, vbuf[...].astype(BF16))

    @pl.when(c == n_chunks - 1)
    def _():
        s2, mask2 = sel_scores(kn_ref[...].astype(BF16), past, seln_ref[:, 0:1])
        online_update(s2, mask2, vn_ref[...].astype(BF16))
        o_s = _fold_groups(acc_sc[...] / jnp.maximum(l_sc[...], 1e-30))

        wb = bk_ref.shape[0]
        kw = jnp.concatenate([bk_ref[...], kwn_ref[...]], axis=0).astype(BF16)
        vw = jnp.concatenate([bv_ref[...], vwn_ref[...]], axis=0).astype(BF16)
        nkw = kw.shape[0]
        kidx = lax.broadcasted_iota(jnp.int32, (1, nkw), 1)
        kpos = past - wb + kidx
        dist = qpos - kpos
        mask_w = (dist >= 0) & (dist < WINDOW) & (kpos >= 0)
        pw = _softmax_rows(_nt_dot(q, kw) - slope * dist.astype(F32), mask_w)
        o_w = _fold_groups(jnp.dot(pw.astype(BF16), vw, preferred_element_type=F32))
        gate = gate_ref[...]
        o_ref[...] = gate[:, 0:1] * oc_ref[...] + gate[:, 1:2] * o_s + gate[:, 2:3] * o_w


def _sample_sel(page_table, pool_k, pool_v, qbd, sel_chunks, k_new, v_new, buf_k, buf_v, kw_new, vw_new,
                o_c, gate_rows, slope_rows, past, l):
    bd, n_pages = page_table.shape
    n_chunks = n_pages // SEL_PAGES
    rows = qbd.shape[1]
    per = sel_chunks.shape[3]
    wb = buf_k.shape[2]
    npad = kw_new.shape[1]
    buf_spec = pl.BlockSpec((None, None, wb, KV_W), lambda b, c, pt: (l, b, 0, 0))
    row3 = lambda n, w: pl.BlockSpec((None, n, w), lambda b, c, pt: (b, 0, 0))
    grid_spec = pltpu.PrefetchScalarGridSpec(
        num_scalar_prefetch=1,
        grid=(bd, n_chunks),
        in_specs=[
            pl.BlockSpec(memory_space=pl.ANY),
            pl.BlockSpec(memory_space=pl.ANY),
            row3(rows, KV_W),
            pl.BlockSpec((None, None, rows, per), lambda b, c, pt: (b, c, 0, 0)),
            pl.BlockSpec((None, None, rows, per), lambda b, c, pt: (b, n_chunks, 0, 0)),
            row3(SEL_BLOCK, KV_W), row3(SEL_BLOCK, KV_W),
            buf_spec, buf_spec,
            row3(npad, KV_W), row3(npad, KV_W),
            row3(rows, HEAD_DIM),
            row3(rows, 3),
            pl.BlockSpec((rows, 1), lambda b, c, pt: (0, 0)),
        ],
        out_specs=row3(rows, HEAD_DIM),
        scratch_shapes=[
            pltpu.VMEM((SEL_PAGES * PAGE_SIZE, KV_W), F32),
            pltpu.VMEM((SEL_PAGES * PAGE_SIZE, KV_W), F32),
            pltpu.SemaphoreType.DMA((2,)),
            pltpu.VMEM((rows, 1), F32), pltpu.VMEM((rows, 1), F32), pltpu.VMEM((rows, KV_W), F32),
        ],
    )
    return pl.pallas_call(
        functools.partial(_sample_sel_kernel, past=past, n_chunks=n_chunks, l=l),
        grid_spec=grid_spec,
        out_shape=jax.ShapeDtypeStruct((bd, rows, HEAD_DIM), F32),
        compiler_params=_cparams(("arbitrary", "arbitrary"), 56),
        name="sample_sel",
    )(page_table, pool_k, pool_v, qbd, sel_chunks, sel_chunks, k_new, v_new, buf_k, buf_v, kw_new, vw_new,
      o_c, gate_rows, slope_rows)


def _alibi_slopes():
    h = jnp.arange(1, N_HEADS + 1, dtype=F32)
    return (2.0 ** (-8.0 * h / N_HEADS)).reshape(N_KV, Q_PER_KV)


def _pad_rows(x, n):
    return jnp.pad(x, ((0, 0), (0, n - x.shape[1]), (0, 0)))


def _layer(l, x, mods, p, sample):
    b, t, d = x.shape
    m = b * t
    x2 = x.reshape(m, d)
    tm = min(t, 1024) if sample is None else m
    slopes = _alibi_slopes()

    h1 = _modulate(x2, p['norm1_w'], mods[0], mods[1], t, l)
    proj = _inproj(h1, p['w_in_t'], p['w_gm_t'][l], p['qn'][l], p['kn'][l], l, tm)
    proj3 = proj.reshape(b, t, proj.shape[1])

    def kv_slice(idx):
        return proj3[:, :, COL_KV + idx * KV_W:COL_KV + (idx + 1) * KV_W]

    k_cmp, v_cmp, k_sel, v_sel, k_win, v_win = [kv_slice(i) for i in range(6)]
    nsa_col = proj.shape[1] - TN
    ng = proj3[:, :, nsa_col:nsa_col + 3 * N_HEADS]
    cmp_consts = p['cmp'][l]

    if sample is None:
        o_a, _ = _chunkmix(proj, p['a_spatial_w'][l], p['a_spatial_b'][l].T, p['a_norm_w'][l][None, :], CHUNK, CHUNK)
        kc, vc = _compress(proj3, COL_KV // KV_W, COL_KV // KV_W + 1, cmp_consts)
        nc = t // CMP_BLOCK
        nqb = t // Q_BLOCK
        cols = Q_PER_KV * Q_BLOCK

        def vt(xv, n):
            return xv.reshape(b, n, N_KV, HEAD_DIM).transpose(0, 2, 3, 1).astype(BF16)

        def evens_first(xc):
            return xc.reshape(b, nc // 2, 2, KV_W).transpose(0, 2, 1, 3).reshape(b, nc, KV_W)

        q_t = proj3[:, :, COL_Q:COL_Q + B_WIDTH].reshape(b, nqb, Q_BLOCK, N_KV, Q_PER_KV, HEAD_DIM)
        q_t = q_t.transpose(0, 3, 1, 5, 4, 2).reshape(b, N_KV, nqb, HEAD_DIM, cols)
        ng_t = ng.reshape(b, nqb, Q_BLOCK, 3, N_KV, Q_PER_KV).transpose(0, 4, 1, 3, 5, 2).reshape(b, N_KV, nqb, 3, cols)
        kv_bf = _cast_cols(proj, COL_KV, 6 * KV_W, tm).reshape(b, t, 6 * KV_W)
        o_t = _nsa_prompt(slopes, q_t, evens_first(kc).astype(BF16), vt(evens_first(vc), nc), kv_bf,
                          vt(v_sel, t), vt(v_win, t), ng_t)
        o_b = o_t.reshape(b, N_KV, nqb, HEAD_DIM, Q_PER_KV, Q_BLOCK).transpose(0, 2, 5, 1, 4, 3).reshape(m, B_WIDTH)
        v_norm = None
        wp = min(WINDOW, t)
        new_bk, new_bv = k_win[:, -wp:], v_win[:, -wp:]
    else:
        ds = t
        eye = jnp.eye(b, dtype=F32)
        w8 = p['a_spatial_w'][l][:, :ds, :ds]
        w_bd = jnp.einsum('ab,gts->gatbs', eye, w8).reshape(A_GROUPS, m, m)
        b_t = jnp.tile(p['a_spatial_b'][l][:, :ds].T, (b, 1))
        o_a, v_norm = _chunkmix(proj, w_bd, b_t, p['a_norm_w'][l][None, :], m, ds)

        page_table = sample['page_table']
        n_pages = page_table.shape[1]
        past = n_pages * PAGE_SIZE
        new_pad = -(-ds // SEL_BLOCK) * SEL_BLOCK
        nc = (past + new_pad) // CMP_BLOCK
        ns = (past + new_pad) // SEL_BLOCK
        kc_past, vc_past = _gather_compress(page_table, sample['cache_k_cmp'], sample['cache_v_cmp'], cmp_consts, l)
        new_rows = jnp.concatenate([_pad_rows(k_cmp, new_pad), _pad_rows(v_cmp, new_pad)], axis=2)
        kc_new, vc_new = _compress(new_rows, 0, 1, cmp_consts)
        ncp = -(-nc // LANES) * LANES
        kc_full = _pad_rows(jnp.concatenate([kc_past, kc_new], axis=1), ncp)
        vc_full = _pad_rows(jnp.concatenate([vc_past, vc_new], axis=1), ncp)

        q5 = proj3[:, :, COL_Q:COL_Q + B_WIDTH].reshape(b, ds, N_KV, Q_PER_KV, HEAD_DIM).transpose(0, 2, 3, 1, 4)
        qbd = jnp.einsum('bgrqd,gh->bgrqhd', q5, jnp.eye(N_KV, dtype=F32)).reshape(b, N_HEADS * ds, KV_W)
        slope_rows = jnp.repeat(slopes.reshape(N_HEADS), ds)[:, None]
        gate_rows = ng.reshape(b, ds, 3, N_HEADS).transpose(0, 3, 1, 2).reshape(b, N_HEADS * ds, 3)
        o_c, sel_chunks = _sample_cmp(qbd, kc_full, vc_full, slope_rows, past, nc, ns)
        o_rows = _sample_sel(page_table, sample['cache_k_sel'], sample['cache_v_sel'], qbd, sel_chunks,
                             _pad_rows(k_sel, new_pad), _pad_rows(v_sel, new_pad), sample['cache_k_win'],
                             sample['cache_v_win'], _pad_rows(k_win, LANES), _pad_rows(v_win, LANES), o_c,
                             gate_rows, slope_rows, past, l)
        buf_k, buf_v = sample['cache_k_win'][l], sample['cache_v_win'][l]
        o_b = o_rows.reshape(b, N_HEADS, ds, HEAD_DIM).transpose(0, 2, 1, 3).reshape(m, B_WIDTH).astype(BF16)
        keep = min(WINDOW, buf_k.shape[1] + ds)
        new_bk = jnp.concatenate([buf_k, k_win], axis=1)[:, -keep:]
        new_bv = jnp.concatenate([buf_v, v_win], axis=1)[:, -keep:]

    def gate_of(vec):
        return vec if sample is None else jnp.repeat(vec, t, axis=0)

    mix = _branch(o_a, o_b, p['w_branch'], proj, l, tm)
    x_mid = _resid_matmul(mix, p['w_out'], x2, gate_of(mods[2]), l, tm, 'attn_out')
    h2 = _modulate(x_mid, p['norm2_w'], mods[3], mods[4], t, l)
    if sample is None:
        s0 = s1 = None
    else:
        state = sample['state_ffn_conv'][l]
        s0 = jnp.repeat(state[:, 0], t, axis=0)
        s1 = jnp.repeat(state[:, 1], t, axis=0)
    hid, tails = _ffn_up(h2, p['ffn_w_a'], p['ffn_w_b'], p['ffn_conv_w'], p['ffn_conv_b'], s0, s1, t, l, tm)
    x_out = _resid_matmul(hid, p['ffn_w_down'], x_mid, gate_of(mods[5]), l, tm, 'ffn_down')

    if sample is None:
        conv_rows = tails.reshape(b, t // tm, SUBLANES, -1)[:, -1, SUBLANES - (CONV_W - 1):]
    else:
        conv_rows = tails.reshape(b, t, -1)[:, t - (CONV_W - 1):]
    heads = lambda z: z.reshape(b, -1, N_KV, HEAD_DIM)
    state_out = dict(k_cmp=heads(k_cmp), v_cmp=heads(v_cmp), k_sel=heads(k_sel), v_sel=heads(v_sel),
                     k_win=heads(new_bk), v_win=heads(new_bv), conv=conv_rows)
    if sample is not None:
        state_out['chunk_v'] = v_norm.reshape(b, t, A_WIDTH)
    return x_out.reshape(b, t, d), state_out


def kernel(x_prompt, x_sample, cache_k_cmp, cache_v_cmp, cache_k_sel, cache_v_sel, cache_k_win, cache_v_win,
           state_ffn_conv, page_table, c_prompt, c_sample, w_ada, b_ada, norm1_w, norm2_w, w_in, a_norm_w,
           a_spatial_w, a_spatial_b, q_norm_w, k_norm_w, cmp_pe_k, cmp_w_k, cmp_pe_v, cmp_w_v, w_branch, w_out,
           ffn_w_a, ffn_w_b, ffn_conv_w, ffn_conv_b, ffn_w_down):
    depth, d, n_in = w_in.shape
    bp = x_prompt.shape[0]
    bs = x_sample.shape[0]

    rows = -(-(bp + bs) // SUBLANES) * SUBLANES
    c_all = jnp.pad(jnp.concatenate([c_prompt, c_sample], axis=0), ((0, rows - bp - bs), (0, 0)))
    mods_all = _ada(c_all, w_ada, b_ada)

    n_merge = n_in - N_MAIN - 3 * N_HEADS
    pad = (-(n_merge + 3 * N_HEADS)) % TN
    w_in_t = jnp.swapaxes(w_in, 1, 2)
    w_gm_t = [jnp.concatenate([w_in_t[l, N_MAIN + 3 * N_HEADS:], w_in_t[l, N_MAIN:N_MAIN + 3 * N_HEADS],
                               jnp.zeros((pad, d), F32)], axis=0) for l in range(depth)]
    shared = dict(
        norm1_w=norm1_w, norm2_w=norm2_w, w_in_t=w_in_t, w_gm_t=w_gm_t,
        qn=jnp.tile(q_norm_w, (1, TN // HEAD_DIM))[:, None, :], kn=jnp.tile(k_norm_w, (1, TN // HEAD_DIM))[:, None, :],
        a_norm_w=a_norm_w, a_spatial_w=a_spatial_w, a_spatial_b=a_spatial_b,
        cmp=[_cmp_consts(cmp_pe_k[l], cmp_w_k[l], cmp_pe_v[l], cmp_w_v[l], k_norm_w[l]) for l in range(depth)],
        w_branch=w_branch, w_out=w_out, ffn_w_a=ffn_w_a, ffn_w_b=ffn_w_b, ffn_conv_w=ffn_conv_w,
        ffn_conv_b=ffn_conv_b, ffn_w_down=ffn_w_down)
    sample = dict(page_table=page_table, cache_k_cmp=cache_k_cmp, cache_v_cmp=cache_v_cmp, cache_k_sel=cache_k_sel,
                  cache_v_sel=cache_v_sel, cache_k_win=cache_k_win, cache_v_win=cache_v_win,
                  state_ffn_conv=state_ffn_conv)
    for name in ('cache_k_cmp', 'cache_v_cmp', 'cache_k_sel', 'cache_v_sel'):
        pool = sample[name]
        sample[name] = pool.reshape(pool.shape[0], pool.shape[1], pool.shape[2], KV_W)
    for name in ('cache_k_win', 'cache_v_win'):
        buf = sample[name]
        sample[name] = buf.reshape(buf.shape[0], buf.shape[1], buf.shape[2], KV_W)

    xp, xs = x_prompt, x_sample
    st_p, st_s = [], []
    for l in range(depth):
        mods = mods_all[l]
        mp = [mods[:bp, k * d:(k + 1) * d] for k in range(6)]
        ms = [mods[bp:bp + bs, k * d:(k + 1) * d] for k in range(6)]
        xp, sp = _layer(l, xp, mp, shared, None)
        xs, ss = _layer(l, xs, ms, shared, sample)
        st_p.append(sp)
        st_s.append(ss)

    def stack(sts, key):
        return jnp.stack([s[key] for s in sts], axis=0)

    return (xp, xs,
            stack(st_p, 'k_cmp'), stack(st_p, 'v_cmp'), stack(st_p, 'k_sel'), stack(st_p, 'v_sel'),
            stack(st_p, 'k_win'), stack(st_p, 'v_win'), stack(st_p, 'conv'),
            stack(st_s, 'k_cmp'), stack(st_s, 'v_cmp'), stack(st_s, 'k_sel'), stack(st_s, 'v_sel'),
            stack(st_s, 'k_win'), stack(st_s, 'v_win'), stack(st_s, 'chunk_v'), stack(st_s, 'conv'))
```

```python
import functools

import jax
import jax.numpy as jnp
from jax import lax
from jax.experimental import pallas as pl
from jax.experimental.pallas import tpu as pltpu

F32 = jnp.float32
BF16 = jnp.bfloat16

HEAD_DIM = 64
N_KV = 4
Q_PER_KV = 4
N_HEADS = N_KV * Q_PER_KV
KV_W = N_KV * HEAD_DIM
A_GROUPS = 8
A_GROUP_DIM = 128
A_WIDTH = A_GROUPS * A_GROUP_DIM
B_WIDTH = N_HEADS * HEAD_DIM
CHUNK = 128
CMP_BLOCK = 32
SEL_BLOCK = 64
TOP_N = 16
WINDOW = 512
Q_BLOCK = 128
PAGE_SIZE = 128
CONV_W = 3
SCALE = HEAD_DIM ** -0.5
EPS = 1e-6
NEG = -1e30
FORCE = 1e4
LOG2E = 1.4426950408889634

LANES = 128
SUBLANES = 8
MIB = 1024 * 1024

TN = 512
COL_U = 0
COL_V = A_WIDTH
COL_Q = 2 * A_WIDTH
COL_KV = COL_Q + B_WIDTH
N_MAIN = COL_KV + 6 * KV_W
N_MAIN_BLOCKS = N_MAIN // TN
SEL_PAGES = 32
CMP_PAGES = 32
SUB_ROWS = 256


def _cparams(sem, vmem_mib):
    return pltpu.CompilerParams(dimension_semantics=sem, vmem_limit_bytes=vmem_mib * MIB)


def _group_rms(x, gmat, w):
    y = x * x
    hi = y.astype(BF16)
    lo = (y - hi.astype(F32)).astype(BF16)
    ms = jnp.dot(hi, gmat, preferred_element_type=F32) + jnp.dot(lo, gmat, preferred_element_type=F32)
    return x * lax.rsqrt(ms + EPS) * w


def _group_mean_matrix(n):
    r = jnp.arange(n)
    return jnp.where((r[:, None] // HEAD_DIM) == (r[None, :] // HEAD_DIM), 1.0 / HEAD_DIM, 0.0).astype(BF16)


def _ada_kernel(c_ref, w_ref, b_ref, o_ref):
    c = c_ref[...]
    s = (c * jax.nn.sigmoid(c)).astype(BF16)
    o_ref[...] = jnp.dot(s, w_ref[...].astype(BF16), preferred_element_type=F32) + b_ref[...]


def _ada(c_all, w_ada, b_ada):
    depth, d, n6 = w_ada.shape
    rows = c_all.shape[0]
    tn = 1024
    return pl.pallas_call(
        _ada_kernel,
        grid=(depth, n6 // tn),
        in_specs=[
            pl.BlockSpec((rows, d), lambda l, j: (0, 0)),
            pl.BlockSpec((None, d, tn), lambda l, j: (l, 0, j)),
            pl.BlockSpec((None, 1, tn), lambda l, j: (l, 0, j)),
        ],
        out_specs=pl.BlockSpec((None, rows, tn), lambda l, j: (l, 0, j)),
        out_shape=jax.ShapeDtypeStruct((depth, rows, n6), F32),
        compiler_params=_cparams(("arbitrary", "arbitrary"), 40),
        name="ada",
    )(c_all, w_ada, b_ada.reshape(depth, 1, n6))


def _modulate_kernel(x_ref, g_ref, sh_ref, sc_ref, o_ref):
    x = x_ref[...]
    y = x * lax.rsqrt(jnp.mean(x * x, axis=-1, keepdims=True) + EPS)
    y = y * g_ref[...]
    o_ref[...] = (y * (1.0 + sc_ref[...]) + sh_ref[...]).astype(BF16)


def _modulate(x2, g, shift, scale, t, l):
    m, d = x2.shape
    nb = m // t
    tm = min(t, 512)
    per = t // tm
    return pl.pallas_call(
        _modulate_kernel,
        grid=(m // tm,),
        in_specs=[
            pl.BlockSpec((tm, d), lambda i: (i, 0)),
            pl.BlockSpec((None, 1, d), lambda i: (l, 0, 0)),
            pl.BlockSpec((None, 1, d), lambda i: (i // per, 0, 0)),
            pl.BlockSpec((None, 1, d), lambda i: (i // per, 0, 0)),
        ],
        out_specs=pl.BlockSpec((tm, d), lambda i: (i, 0)),
        out_shape=jax.ShapeDtypeStruct((m, d), BF16),
        compiler_params=_cparams(("arbitrary",), 32),
        name="modulate",
    )(x2, g.reshape(g.shape[0], 1, d), shift.reshape(nb, 1, d), scale.reshape(nb, 1, d))


def _inproj_kernel(h_ref, w1_ref, w2_ref, qn_ref, kn_ref, gm_ref, o_ref, wbf_ref):
    j = pl.program_id(0)
    i = pl.program_id(1)

    @pl.when((i == 0) & (j < N_MAIN_BLOCKS))
    def _():
        wbf_ref[...] = w1_ref[...].T.astype(BF16)

    @pl.when((i == 0) & (j >= N_MAIN_BLOCKS))
    def _():
        wbf_ref[...] = w2_ref[...].T.astype(BF16)

    tm = h_ref.shape[0]
    sub = min(tm, SUB_ROWS)

    def project(epilogue):
        for c in range(tm // sub):
            rows = slice(c * sub, (c + 1) * sub)
            o_ref[rows, :] = epilogue(jnp.dot(h_ref[rows, :], wbf_ref[...], preferred_element_type=F32))

    @pl.when(j < 4)
    def _():
        project(jax.nn.gelu)

    @pl.when((j == 4) | (j == 5))
    def _():
        project(lambda acc: _group_rms(acc, gm_ref[...], qn_ref[...]))

    @pl.when(j == 6)
    def _():
        project(lambda acc: acc)

    @pl.when((j == 7) | (j == 8))
    def _():
        lane = lax.broadcasted_iota(jnp.int32, (sub, TN), 1)
        project(lambda acc: jnp.where(lane < KV_W, _group_rms(acc, gm_ref[...], kn_ref[...]), acc))

    @pl.when(j >= N_MAIN_BLOCKS)
    def _():
        project(jax.nn.sigmoid)


def _inproj(h, w_in_t, w_gm_t, qn, kn, l, tm):
    m, d = h.shape
    n_gm = w_gm_t.shape[0] // TN
    nj = N_MAIN_BLOCKS + n_gm
    gm = _group_mean_matrix(TN)
    return pl.pallas_call(
        _inproj_kernel,
        grid=(nj, m // tm),
        in_specs=[
            pl.BlockSpec((tm, d), lambda j, i: (i, 0)),
            pl.BlockSpec((None, TN, d), lambda j, i: (l, jnp.minimum(j, N_MAIN_BLOCKS - 1), 0)),
            pl.BlockSpec((TN, d), lambda j, i: (jnp.maximum(j - N_MAIN_BLOCKS, 0), 0)),
            pl.BlockSpec((1, TN), lambda j, i: (0, 0)),
            pl.BlockSpec((1, TN), lambda j, i: (0, 0)),
            pl.BlockSpec((TN, TN), lambda j, i: (0, 0)),
        ],
        out_specs=pl.BlockSpec((tm, TN), lambda j, i: (i, j)),
        out_shape=jax.ShapeDtypeStruct((m, nj * TN), F32),
        scratch_shapes=[pltpu.VMEM((d, TN), BF16)],
        compiler_params=_cparams(("arbitrary", "arbitrary"), 48),
        name="inproj",
    )(h, w_in_t, w_gm_t, qn, kn, gm)


def _cast_kernel(x_ref, o_ref):
    o_ref[...] = x_ref[...].astype(BF16)


def _cast_cols(x2, col0, width, tm):
    m = x2.shape[0]
    return pl.pallas_call(
        _cast_kernel,
        grid=(m // tm,),
        in_specs=[pl.BlockSpec((tm, width), lambda i: (i, col0 // width))],
        out_specs=pl.BlockSpec((tm, width), lambda i: (i, 0)),
        out_shape=jax.ShapeDtypeStruct((m, width), BF16),
        compiler_params=_cparams(("arbitrary",), 32),
        name="cast_cols",
    )(x2)


def _chunkmix_kernel(u_ref, gv_ref, w_ref, bt_ref, an_ref, oa_ref, v_ref, *, lc, causal_block):
    gv = gv_ref[...]
    v = gv * lax.rsqrt(jnp.mean(gv * gv, axis=-1, keepdims=True) + EPS) * an_ref[...]
    v_ref[...] = v
    row = lax.broadcasted_iota(jnp.int32, (lc, lc), 0)
    col = lax.broadcasted_iota(jnp.int32, (lc, lc), 1)
    shift = causal_block.bit_length() - 1
    mask = (col <= row) & ((row >> shift) == (col >> shift))
    vb = v.astype(BF16)
    for g in range(A_GROUPS):
        sl = slice(g * A_GROUP_DIM, (g + 1) * A_GROUP_DIM)
        wg = jnp.where(mask, w_ref[g], 0.0).astype(BF16)
        s = jnp.dot(wg, vb[:, sl], preferred_element_type=F32) + bt_ref[:, g:g + 1]
        oa_ref[:, sl] = (u_ref[:, sl] * s).astype(BF16)


def _chunkmix(proj, w_s, b_t, a_norm, lc, causal_block):
    m = proj.shape[0]
    return pl.pallas_call(
        functools.partial(_chunkmix_kernel, lc=lc, causal_block=causal_block),
        grid=(m // lc,),
        in_specs=[
            pl.BlockSpec((lc, A_WIDTH), lambda c: (c, 0)),
            pl.BlockSpec((lc, A_WIDTH), lambda c: (c, 1)),
            pl.BlockSpec((A_GROUPS, lc, lc), lambda c: (0, 0, 0)),
            pl.BlockSpec((lc, A_GROUPS), lambda c: (0, 0)),
            pl.BlockSpec((1, A_WIDTH), lambda c: (0, 0)),
        ],
        out_specs=[
            pl.BlockSpec((lc, A_WIDTH), lambda c: (c, 0)),
            pl.BlockSpec((lc, A_WIDTH), lambda c: (c, 0)),
        ],
        out_shape=[jax.ShapeDtypeStruct((m, A_WIDTH), BF16), jax.ShapeDtypeStruct((m, A_WIDTH), F32)],
        compiler_params=_cparams(("arbitrary",), 32),
        name="chunkmix",
    )(proj, proj, w_s, b_t, a_norm)


def _compress_rows(half_refs, n_blk, bd_ref, pe_ref):
    halves = []
    for ref in half_refs:
        acc = jnp.zeros((n_blk, LANES), F32)
        for i in range(CMP_BLOCK):
            rows = ref[pl.ds(i, n_blk, stride=CMP_BLOCK), :] + pe_ref[i:i + 1, :]
            acc = acc + jnp.dot(rows.astype(BF16), bd_ref[i], preferred_element_type=F32)
        halves.append(acc)
    return jnp.concatenate(halves, axis=1)


def _compress_kernel(k0_ref, k1_ref, v0_ref, v1_ref, bdk_ref, bdv_ref, pek_ref, pev_ref, knw_ref, gm_ref,
                     kc_ref, vc_ref, *, nc):
    kc_ref[...] = _group_rms(_compress_rows((k0_ref, k1_ref), nc, bdk_ref, pek_ref), gm_ref[...], knw_ref[...])
    vc_ref[...] = _compress_rows((v0_ref, v1_ref), nc, bdv_ref, pev_ref)


def _compress(src3, kblk, vblk, cmp_consts):
    b, t, _ = src3.shape
    nc = t // CMP_BLOCK
    bdk, bdv, pek, pev, knw, gm = cmp_consts
    const3 = lambda i: (0, 0, 0)
    const2 = lambda i: (0, 0)
    half = lambda blk, h: pl.BlockSpec((None, t, LANES), lambda i: (i, 0, 2 * blk + h))
    return pl.pallas_call(
        functools.partial(_compress_kernel, nc=nc),
        grid=(b,),
        in_specs=[
            half(kblk, 0), half(kblk, 1), half(vblk, 0), half(vblk, 1),
            pl.BlockSpec((CMP_BLOCK, LANES, LANES), const3),
            pl.BlockSpec((CMP_BLOCK, LANES, LANES), const3),
            pl.BlockSpec((CMP_BLOCK, LANES), const2),
            pl.BlockSpec((CMP_BLOCK, LANES), const2),
            pl.BlockSpec((1, KV_W), const2),
            pl.BlockSpec((KV_W, KV_W), const2),
        ],
        out_specs=[pl.BlockSpec((None, nc, KV_W), lambda i: (i, 0, 0))] * 2,
        out_shape=[jax.ShapeDtypeStruct((b, nc, KV_W), F32)] * 2,
        compiler_params=_cparams(("arbitrary",), 40),
        name="compress",
    )(src3, src3, src3, src3, bdk, bdv, pek, pev, knw, gm)


def _cmp_consts(cmp_pe_k, cmp_w_k, cmp_pe_v, cmp_w_v, k_norm_w):
    per = LANES // HEAD_DIM
    eye = jnp.eye(per, dtype=F32)

    def bd(w):
        return jnp.einsum('gh,ide->igdhe', eye, w).reshape(CMP_BLOCK, LANES, LANES).astype(BF16)

    return (bd(cmp_w_k), bd(cmp_w_v), jnp.tile(cmp_pe_k, (1, per)), jnp.tile(cmp_pe_v, (1, per)),
            jnp.tile(k_norm_w, N_KV)[None, :], _group_mean_matrix(KV_W))


def _pair_sum(psum, pair):
    hi = psum.astype(BF16)
    lo = (psum - hi.astype(F32)).astype(BF16)
    return jnp.dot(hi, pair, preferred_element_type=F32) + jnp.dot(lo, pair, preferred_element_type=F32)


def _top_mask(score, n_valid, top_n):
    rows, width = score.shape
    jj = lax.broadcasted_iota(jnp.int32, (rows, width), 1)
    rank = jnp.zeros((rows, width), F32)
    for k in range(n_valid):
        ck = score[:, k:k + 1]
        beats = (ck > score) | ((ck == score) & (jj > k))
        rank = rank + jnp.where(beats, 1.0, 0.0)
    return jnp.where((rank < top_n) & (jj < n_valid), 1.0, 0.0)


def _softmax_rows(s, mask):
    s = jnp.where(mask, s, NEG)
    m = jnp.max(s, axis=-1, keepdims=True)
    p = jnp.where(mask, jnp.exp(s - m), 0.0)
    return p / jnp.maximum(jnp.sum(p, axis=-1, keepdims=True), 1e-30)


def _softmax2_cols(s2, mask):
    s2 = jnp.where(mask, s2, NEG)
    m = jnp.max(s2, axis=0, keepdims=True)
    p = jnp.where(mask, jnp.exp2(s2 - m), 0.0)
    return p / jnp.maximum(jnp.sum(p, axis=0, keepdims=True), 1e-30)


def _nsa_prompt_kernel(slopes2_ref, qt_ref, kc_ref, vct_ref, ks_ref, vst_ref, kw_ref, vwt_ref, ngt_ref, bias_ref,
                       o_ref, qbd_sc, sel_sc, m_sc, l_sc, acc_sc, *, t, tk):
    i = pl.program_id(1)
    nc = t // CMP_BLOCK
    ns = t // SEL_BLOCK
    half = nc // 2
    start = i * Q_BLOCK
    groups = range(N_KV)
    lane_blk = [slice(r * Q_BLOCK, (r + 1) * Q_BLOCK) for r in range(Q_PER_KV)]
    slopes2 = [[slopes2_ref[g, r] for r in range(Q_PER_KV)] for g in groups]
    qpos = start + lax.broadcasted_iota(jnp.int32, (1, Q_BLOCK), 1)

    qbd_sc[...] = jnp.zeros(qbd_sc.shape, BF16)
    for g in groups:
        qbd_sc[g, g * HEAD_DIM:(g + 1) * HEAD_DIM, :] = (qt_ref[g] * (SCALE * LOG2E)).astype(BF16)

    row = lax.broadcasted_iota(jnp.int32, (nc, 1), 0)
    blk_c = jnp.where(row < half, 2 * row, 2 * (row - half) + 1)
    dist_c = qpos - ((blk_c + 1) * CMP_BLOCK - 1)
    mask_c = dist_c >= 0
    dist_cf = dist_c.astype(F32)
    jj = lax.broadcasted_iota(jnp.int32, (ns, 1), 0)
    cur = qpos >> 6
    forced = (jj == 0) | (jj == cur) | (jj == cur - 1)
    kc = kc_ref[...]
    o_c = []
    for g in groups:
        st = jnp.dot(kc, qbd_sc[g], preferred_element_type=F32)
        p_heads = [_softmax2_cols(st[:, lane_blk[r]] - slopes2[g][r] * dist_cf, mask_c) for r in range(Q_PER_KV)]
        o_c.append(jnp.dot(vct_ref[g], jnp.concatenate(p_heads, axis=1).astype(BF16), preferred_element_type=F32))
        psum = ((p_heads[0] + p_heads[1]) + p_heads[2]) + p_heads[3]
        imp = psum[:half] + psum[half:]
        score = jnp.where(jj <= cur, imp + jnp.where(forced, FORCE, 0.0), NEG)
        rank = jnp.zeros((ns, Q_BLOCK), F32)
        for k in range(ns):
            sk = score[k:k + 1, :]
            lo = (k // SUBLANES) * SUBLANES
            hi = lo + SUBLANES
            one = lambda cond: jnp.where(cond, 1.0, 0.0)
            rank = rank + jnp.concatenate(
                ([one(sk > score[:lo])] if lo else [])
                + [jnp.where(jj[lo:hi] > k, one(sk >= score[lo:hi]), one(sk > score[lo:hi]))]
                + ([one(sk >= score[hi:])] if hi < ns else []), axis=0)
        sel_sc[g] = jnp.where(rank < min(TOP_N, ns), 0.0, NEG)

    m_sc[...] = jnp.full(m_sc.shape, NEG, F32)
    l_sc[...] = jnp.zeros(l_sc.shape, F32)
    acc_sc[...] = jnp.zeros(acc_sc.shape, F32)
    blocks_per_tile = tk // SEL_BLOCK
    last = start // tk

    def sel_tile(ti, causal):
        k0 = pl.multiple_of(ti * tk, tk)
        kt = ks_ref[pl.ds(k0, tk), :]
        off = start - k0
        offf = off.astype(F32)
        if causal:
            rel = lax.broadcasted_iota(jnp.int32, (tk, Q_BLOCK), 1) - lax.broadcasted_iota(jnp.int32, (tk, Q_BLOCK), 0)
            future = rel + off < 0
        for g in groups:
            s = jnp.dot(kt, qbd_sc[g], preferred_element_type=F32)
            sel_rows = sel_sc[g, pl.ds(pl.multiple_of(ti * blocks_per_tile, blocks_per_tile), blocks_per_tile), :]
            mask_bias = jnp.concatenate(
                [jnp.broadcast_to(sel_rows[jb:jb + 1, :], (SEL_BLOCK, Q_BLOCK)) for jb in range(blocks_per_tile)],
                axis=0)
            if causal:
                mask_bias = jnp.where(future, NEG, mask_bias)
            m_old = m_sc[g]
            p_parts, m_parts, sum_parts = [], [], []
            for r in range(Q_PER_KV):
                c_r = slopes2[g][r] * offf
                sr = (s[:, lane_blk[r]] - bias_ref[g, r, 0:tk, :]) + mask_bias
                m_new = jnp.maximum(m_old[:, lane_blk[r]], jnp.max(sr, axis=0, keepdims=True) - c_r)
                p = jnp.exp2(sr - (m_new + c_r))
                m_parts.append(m_new)
                sum_parts.append(jnp.sum(p, axis=0, keepdims=True))
                p_parts.append(p.astype(BF16))
            m_new = jnp.concatenate(m_parts, axis=1)
            alpha = jnp.exp2(m_old - m_new)
            l_sc[g] = alpha * l_sc[g] + jnp.concatenate(sum_parts, axis=1)
            pv = jnp.dot(vst_ref[g, :, pl.ds(k0, tk)], jnp.concatenate(p_parts, axis=1), preferred_element_type=F32)
            acc_sc[g] = alpha * acc_sc[g] + pv
            m_sc[g] = m_new

    def full_tile(ti, carry):
        sel_tile(ti, False)
        return carry

    lax.fori_loop(0, last, full_tile, 0)
    sel_tile(last, True)

    wk = WINDOW + Q_BLOCK
    k0w = pl.multiple_of(jnp.maximum(start - WINDOW, 0), Q_BLOCK)
    kwt = kw_ref[pl.ds(k0w, wk), :]
    dist_w = qpos - (k0w + lax.broadcasted_iota(jnp.int32, (wk, 1), 0))
    mask_w = (dist_w >= 0) & (dist_w < WINDOW)
    for g in groups:
        sw = jnp.dot(kwt, qbd_sc[g], preferred_element_type=F32)
        pw = [_softmax2_cols(sw[:, lane_blk[r]] - bias_ref[g, r, 0:wk, :], mask_w) for r in range(Q_PER_KV)]
        o_w = jnp.dot(vwt_ref[g, :, pl.ds(k0w, wk)], jnp.concatenate(pw, axis=1).astype(BF16),
                      preferred_element_type=F32)
        o_s = acc_sc[g] / jnp.maximum(l_sc[g], 1e-30)
        ng = ngt_ref[g]
        o_ref[g] = (ng[0:1, :] * o_c[g] + ng[1:2, :] * o_s + ng[2:3, :] * o_w).astype(BF16)


def _pair_matrix(nc, ns_pad):
    n = jnp.arange(nc)[:, None]
    j = jnp.arange(ns_pad)[None, :]
    return jnp.where((n // (SEL_BLOCK // CMP_BLOCK)) == j, 1.0, 0.0).astype(BF16)


def _nsa_prompt(slopes, q_t, kc_perm, vc_t, kv_bf, vs_t, vw_t, ng_t):
    b, _, nqb, _, cols = q_t.shape
    t = nqb * Q_BLOCK
    nc = t // CMP_BLOCK
    ns = t // SEL_BLOCK
    tk = min(512, t)
    wk = WINDOW + Q_BLOCK
    slopes2 = slopes * LOG2E
    rel = (jnp.arange(Q_BLOCK)[None, :] - jnp.arange(wk)[:, None]).astype(F32)
    bias_tab = slopes2[:, :, None, None] * rel[None, None]
    blk5 = lambda rows: pl.BlockSpec((None, N_KV, None, rows, cols), lambda bi, i: (bi, 0, i, 0, 0))
    vt_spec = lambda n: pl.BlockSpec((None, N_KV, HEAD_DIM, n), lambda bi, i: (bi, 0, 0, 0))
    kv_spec = lambda col: pl.BlockSpec((None, t, KV_W), lambda bi, i: (bi, 0, col))
    return pl.pallas_call(
        functools.partial(_nsa_prompt_kernel, t=t, tk=tk),
        grid=(b, nqb),
        in_specs=[
            pl.BlockSpec(memory_space=pltpu.SMEM),
            blk5(HEAD_DIM),
            pl.BlockSpec((None, nc, KV_W), lambda bi, i: (bi, 0, 0)),
            vt_spec(nc), kv_spec(2), vt_spec(t), kv_spec(4), vt_spec(t),
            blk5(3),
            pl.BlockSpec((N_KV, Q_PER_KV, wk, Q_BLOCK), lambda bi, i: (0, 0, 0, 0)),
        ],
        out_specs=blk5(HEAD_DIM),
        out_shape=jax.ShapeDtypeStruct((b, N_KV, nqb, HEAD_DIM, cols), BF16),
        scratch_shapes=[pltpu.VMEM((N_KV, KV_W, cols), BF16), pltpu.VMEM((N_KV, ns, Q_BLOCK), F32),
                        pltpu.VMEM((N_KV, 1, cols), F32), pltpu.VMEM((N_KV, 1, cols), F32),
                        pltpu.VMEM((N_KV, HEAD_DIM, cols), F32)],
        compiler_params=_cparams(("arbitrary", "arbitrary"), 56),
        name="nsa_prompt",
    )(slopes2, q_t, kc_perm, vc_t, kv_bf, vs_t, kv_bf, vw_t, ng_t, bias_tab)


def _branch_kernel(oa_ref, ob_ref, w_ref, g0_ref, g1_ref, o_ref, wbf_ref):
    @pl.when(pl.program_id(1) == 0)
    def _():
        wbf_ref[...] = w_ref[...].astype(BF16)

    pa = jnp.dot(oa_ref[...], wbf_ref[:A_WIDTH, :], preferred_element_type=F32)
    pb = jnp.dot(ob_ref[...], wbf_ref[A_WIDTH:, :], preferred_element_type=F32)
    o_ref[...] = (g0_ref[...] * pa + g1_ref[...] * pb).astype(BF16)


def _branch(o_a, o_b, w_branch, proj, l, tm):
    m = o_a.shape[0]
    kb, d = w_branch.shape[1:]
    nj = d // TN
    return pl.pallas_call(
        _branch_kernel,
        grid=(nj, m // tm),
        in_specs=[
            pl.BlockSpec((tm, A_WIDTH), lambda j, i: (i, 0)),
            pl.BlockSpec((tm, B_WIDTH), lambda j, i: (i, 0)),
            pl.BlockSpec((None, kb, TN), lambda j, i: (l, 0, j)),
            pl.BlockSpec((tm, TN), lambda j, i: (i, N_MAIN_BLOCKS + j)),
            pl.BlockSpec((tm, TN), lambda j, i: (i, N_MAIN_BLOCKS + nj + j)),
        ],
        out_specs=pl.BlockSpec((tm, TN), lambda j, i: (i, j)),
        out_shape=jax.ShapeDtypeStruct((m, d), BF16),
        scratch_shapes=[pltpu.VMEM((kb, TN), BF16)],
        compiler_params=_cparams(("arbitrary", "arbitrary"), 48),
        name="branch",
    )(o_a, o_b, w_branch, proj, proj)


def _resid_kernel(a_ref, w_ref, x_ref, g_ref, o_ref, wbf_ref):
    @pl.when(pl.program_id(1) == 0)
    def _():
        wbf_ref[...] = w_ref[...].astype(BF16)

    o_ref[...] = x_ref[...] + g_ref[...] * jnp.dot(a_ref[...], wbf_ref[...], preferred_element_type=F32)


def _resid_matmul(a, w, x2, gate, l, tm, name):
    m, k = a.shape
    d = w.shape[2]
    if gate.shape[0] == m:
        gate_spec = pl.BlockSpec((tm, TN), lambda j, i: (i, j))
    else:
        nb = gate.shape[0]
        per = (m // nb) // tm
        gate = gate.reshape(nb, 1, d)
        gate_spec = pl.BlockSpec((None, 1, TN), lambda j, i: (i // per, 0, j))
    return pl.pallas_call(
        _resid_kernel,
        grid=(d // TN, m // tm),
        in_specs=[
            pl.BlockSpec((tm, k), lambda j, i: (i, 0)),
            pl.BlockSpec((None, k, TN), lambda j, i: (l, 0, j), pipeline_mode=pl.Buffered(1)),
            pl.BlockSpec((tm, TN), lambda j, i: (i, j)),
            gate_spec,
        ],
        out_specs=pl.BlockSpec((tm, TN), lambda j, i: (i, j)),
        out_shape=jax.ShapeDtypeStruct((m, d), F32),
        scratch_shapes=[pltpu.VMEM((k, TN), BF16)],
        compiler_params=_cparams(("arbitrary", "arbitrary"), 56),
        name=name,
    )(a, w, x2, gate)


def _ffn_up_kernel(h_ref, wa_ref, wb_ref, cw_ref, cb_ref, s0_ref, s1_ref, o_ref, tail_ref,
                   wa_bf, wb_bf, abuf, *, tm, seg, per_batch, keep):
    i = pl.program_id(1)

    @pl.when(i == 0)
    def _():
        wa_bf[...] = wa_ref[...].astype(BF16)
        wb_bf[...] = wb_ref[...].astype(BF16)

    @pl.when(i % per_batch == 0)
    def _():
        abuf[0:SUBLANES, :] = jnp.zeros((SUBLANES, TN), F32)

    sub = min(tm, SUB_ROWS)
    for c in range(tm // sub):
        rows = slice(c * sub, (c + 1) * sub)
        h = h_ref[rows, :]
        a = jnp.dot(h, wa_bf[...], preferred_element_type=F32)
        b = jnp.dot(h, wb_bf[...], preferred_element_type=F32)
        abuf[SUBLANES + c * sub:SUBLANES + (c + 1) * sub, :] = a
        a1 = abuf[SUBLANES - 1 + c * sub:SUBLANES - 1 + (c + 1) * sub, :]
        a2 = abuf[SUBLANES - 2 + c * sub:SUBLANES - 2 + (c + 1) * sub, :]
        if seg < tm:
            r = lax.broadcasted_iota(jnp.int32, (sub, 1), 0) & (seg - 1)
            a1 = jnp.where(r == 0, s1_ref[rows, :], a1)
            a2 = jnp.where(r == 0, s0_ref[rows, :], jnp.where(r == 1, s1_ref[rows, :], a2))
        conv = a2 * cw_ref[0:1, :] + a1 * cw_ref[1:2, :] + a * cw_ref[2:3, :] + cb_ref[...]
        o_ref[rows, :] = (jax.nn.gelu(conv) * b).astype(BF16)
    tail_ref[...] = abuf[SUBLANES + tm - keep:SUBLANES + tm, :]
    abuf[0:SUBLANES, :] = abuf[tm:tm + SUBLANES, :]


def _ffn_up(h2, w_a, w_b, conv_w, conv_b, s0, s1, t, l, tm):
    m, d = h2.shape
    dff = w_a.shape[2]
    per_batch = max(t // tm, 1)
    seg = min(t, tm)
    keep = SUBLANES if seg == tm else tm
    if s0 is None:
        s0 = s1 = jnp.zeros((SUBLANES, dff), F32)
        s_spec = pl.BlockSpec((SUBLANES, TN), lambda j, i: (0, j))
    else:
        s_spec = pl.BlockSpec((tm, TN), lambda j, i: (i, j))
    w_spec = pl.BlockSpec((None, d, TN), lambda j, i: (l, 0, j))
    return pl.pallas_call(
        functools.partial(_ffn_up_kernel, tm=tm, seg=seg, per_batch=per_batch, keep=keep),
        grid=(dff // TN, m // tm),
        in_specs=[
            pl.BlockSpec((tm, d), lambda j, i: (i, 0)),
            w_spec, w_spec,
            pl.BlockSpec((None, CONV_W, TN), lambda j, i: (l, 0, j)),
            pl.BlockSpec((None, 1, TN), lambda j, i: (l, 0, j)),
            s_spec, s_spec,
        ],
        out_specs=[
            pl.BlockSpec((tm, TN), lambda j, i: (i, j)),
            pl.BlockSpec((None, keep, TN), lambda j, i: (i, 0, j)),
        ],
        out_shape=[jax.ShapeDtypeStruct((m, dff), BF16), jax.ShapeDtypeStruct((m // tm, keep, dff), F32)],
        scratch_shapes=[pltpu.VMEM((d, TN), BF16), pltpu.VMEM((d, TN), BF16), pltpu.VMEM((tm + SUBLANES, TN), F32)],
        compiler_params=_cparams(("arbitrary", "arbitrary"), 56),
        name="ffn_up",
    )(h2, w_a, w_b, conv_w, conv_b.reshape(conv_b.shape[0], 1, dff), s0, s1)


def _gather_pages(pt_ref, b, c, pool_ref, l, buf_ref, sem, n_pages):
    copies = []
    for p in range(n_pages):
        page = pt_ref[b, c * n_pages + p]
        copies.append(pltpu.make_async_copy(pool_ref.at[l, page], buf_ref.at[:, pl.ds(p * PAGE_SIZE, PAGE_SIZE)], sem))
    return copies


def _gather_compress_kernel(pt_ref, poolk_ref, poolv_ref, bdk_ref, bdv_ref, pek_ref, pev_ref, knw_ref, gm_ref,
                            kc_ref, vc_ref, kbuf, vbuf, tbuf, sems, *, l, n_chunks, n_steps):
    b = pl.program_id(0)
    c = pl.program_id(1)
    step = b * n_chunks + c
    slot = step % 2

    def chunk_copies(bb, cc, sl):
        return (_gather_pages(pt_ref, bb, cc, poolk_ref, l, kbuf.at[sl], sems.at[0, sl], CMP_PAGES),
                _gather_pages(pt_ref, bb, cc, poolv_ref, l, vbuf.at[sl], sems.at[1, sl], CMP_PAGES))

    def token_major(buf):
        for h in range(KV_W // LANES):
            tbuf[h] = buf[h * LANES:(h + 1) * LANES, :].T
        return (tbuf.at[0], tbuf.at[1])

    @pl.when(step == 0)
    def _():
        for cps in chunk_copies(b, c, slot):
            for cp in cps:
                cp.start()

    @pl.when(step + 1 < n_steps)
    def _():
        wrap = c + 1 == n_chunks
        for cps in chunk_copies(jnp.where(wrap, b + 1, b), jnp.where(wrap, 0, c + 1), 1 - slot):
            for cp in cps:
                cp.start()

    kcopies, vcopies = chunk_copies(b, c, slot)
    n_blk = CMP_PAGES * PAGE_SIZE // CMP_BLOCK
    for cp in kcopies:
        cp.wait()
    kc_ref[...] = _group_rms(_compress_rows(token_major(kbuf.at[slot]), n_blk, bdk_ref, pek_ref),
                             gm_ref[...], knw_ref[...])
    for cp in vcopies:
        cp.wait()
    vc_ref[...] = _compress_rows(token_major(vbuf.at[slot]), n_blk, bdv_ref, pev_ref)


def _gather_compress(page_table, pool_k, pool_v, cmp_consts, l):
    bd, n_pages = page_table.shape
    n_chunks = n_pages // CMP_PAGES
    n_blk = CMP_PAGES * PAGE_SIZE // CMP_BLOCK
    bdk, bdv, pek, pev, knw, gm = cmp_consts
    const3 = lambda b, c, pt: (0, 0, 0)
    const2 = lambda b, c, pt: (0, 0)
    grid_spec = pltpu.PrefetchScalarGridSpec(
        num_scalar_prefetch=1,
        grid=(bd, n_chunks),
        in_specs=[
            pl.BlockSpec(memory_space=pl.ANY),
            pl.BlockSpec(memory_space=pl.ANY),
            pl.BlockSpec((CMP_BLOCK, LANES, LANES), const3),
            pl.BlockSpec((CMP_BLOCK, LANES, LANES), const3),
            pl.BlockSpec((CMP_BLOCK, LANES), const2),
            pl.BlockSpec((CMP_BLOCK, LANES), const2),
            pl.BlockSpec((1, KV_W), const2),
            pl.BlockSpec((KV_W, KV_W), const2),
        ],
        out_specs=[pl.BlockSpec((None, n_blk, KV_W), lambda b, c, pt: (b, c, 0))] * 2,
        scratch_shapes=[
            pltpu.VMEM((2, KV_W, CMP_PAGES * PAGE_SIZE), F32),
            pltpu.VMEM((2, KV_W, CMP_PAGES * PAGE_SIZE), F32),
            pltpu.VMEM((KV_W // LANES, CMP_PAGES * PAGE_SIZE, LANES), F32),
            pltpu.SemaphoreType.DMA((2, 2)),
        ],
    )
    return pl.pallas_call(
        functools.partial(_gather_compress_kernel, l=l, n_chunks=n_chunks, n_steps=bd * n_chunks),
        grid_spec=grid_spec,
        out_shape=[jax.ShapeDtypeStruct((bd, n_chunks * n_blk, KV_W), F32)] * 2,
        compiler_params=_cparams(("arbitrary", "arbitrary"), 48),
        name="gather_compress",
    )(page_table, pool_k, pool_v, bdk, bdv, pek, pev, knw, gm)


def _fold_groups(x):
    rows = x.shape[0]
    grp = lax.broadcasted_iota(jnp.int32, (rows, 1), 0) >> 5
    out = jnp.zeros((rows, HEAD_DIM), F32)
    for g in range(N_KV):
        out = out + jnp.where(grp == g, x[:, g * HEAD_DIM:(g + 1) * HEAD_DIM], 0.0)
    return out


def _nt_dot(a, b):
    return lax.dot_general(a, b, (((1,), (1,)), ((), ())), preferred_element_type=F32)


def _sample_cmp_kernel(q_ref, kc_ref, vc_ref, slope_ref, pair_ref, oc_ref, sel_ref, *, past, nc, ns, ns_pad):
    rows = N_HEADS * SUBLANES
    q = (q_ref[...] * SCALE).astype(BF16)
    ncp = kc_ref.shape[0]
    s = _nt_dot(q, kc_ref[...].astype(BF16))
    qpos = past + (lax.broadcasted_iota(jnp.int32, (rows, 1), 0) & (SUBLANES - 1))
    n = lax.broadcasted_iota(jnp.int32, (1, ncp), 1)
    dist = qpos - ((n + 1) * CMP_BLOCK - 1)
    mask = (dist >= 0) & (n < nc)
    p = _softmax_rows(s - slope_ref[...] * dist.astype(F32), mask)
    oc_ref[...] = _fold_groups(jnp.dot(p.astype(BF16), vc_ref[...].astype(BF16), preferred_element_type=F32))

    parts = []
    for g in range(N_KV):
        acc = p[(g * Q_PER_KV) * SUBLANES:(g * Q_PER_KV + 1) * SUBLANES]
        for r in range(1, Q_PER_KV):
            acc = acc + p[(g * Q_PER_KV + r) * SUBLANES:(g * Q_PER_KV + r + 1) * SUBLANES]
        parts.append(acc)
    psum = jnp.concatenate(parts, axis=0)
    imp = _pair_sum(psum, pair_ref[...])
    gq = N_KV * SUBLANES
    jj = lax.broadcasted_iota(jnp.int32, (gq, ns_pad), 1)
    cur = (past + (lax.broadcasted_iota(jnp.int32, (gq, 1), 0) & (SUBLANES - 1))) >> 6
    forced = (jj == 0) | (jj == cur) | (jj == cur - 1)
    score = jnp.where(jj <= cur, imp + jnp.where(forced, FORCE, 0.0), NEG)
    sel = _top_mask(score, ns, min(TOP_N, ns))
    pieces = []
    for g in range(N_KV):
        for r in range(Q_PER_KV):
            pieces.append(sel[g * SUBLANES:(g + 1) * SUBLANES])
    sel_rows = jnp.concatenate(pieces, axis=0).astype(BF16)
    per = SEL_PAGES * PAGE_SIZE // SEL_BLOCK
    for c in range(ns_pad // per):
        sel_ref[c] = sel_rows[:, c * per:(c + 1) * per]


def _sample_cmp(qbd, kc_full, vc_full, slope_rows, past, nc, ns):
    bd, rows, _ = qbd.shape
    ncp = kc_full.shape[1]
    per = SEL_PAGES * PAGE_SIZE // SEL_BLOCK
    ns_pad = -(-ns // per) * per
    pair = _pair_matrix(ncp, ns_pad)
    return pl.pallas_call(
        functools.partial(_sample_cmp_kernel, past=past, nc=nc, ns=ns, ns_pad=ns_pad),
        grid=(bd,),
        in_specs=[
            pl.BlockSpec((None, rows, KV_W), lambda b: (b, 0, 0)),
            pl.BlockSpec((None, ncp, KV_W), lambda b: (b, 0, 0)),
            pl.BlockSpec((None, ncp, KV_W), lambda b: (b, 0, 0)),
            pl.BlockSpec((rows, 1), lambda b: (0, 0)),
            pl.BlockSpec((ncp, ns_pad), lambda b: (0, 0)),
        ],
        out_specs=[
            pl.BlockSpec((None, rows, HEAD_DIM), lambda b: (b, 0, 0)),
            pl.BlockSpec((None, ns_pad // per, rows, per), lambda b: (b, 0, 0, 0)),
        ],
        out_shape=[jax.ShapeDtypeStruct((bd, rows, HEAD_DIM), F32),
                   jax.ShapeDtypeStruct((bd, ns_pad // per, rows, per), BF16)],
        compiler_params=_cparams(("arbitrary",), 40),
        name="sample_cmp",
    )(qbd, kc_full, vc_full, slope_rows, pair)


def _sample_sel_kernel(pt_ref, poolk_ref, poolv_ref, q_ref, selc_ref, seln_ref, kn_ref, vn_ref, bk_ref, bv_ref,
                       kwn_ref, vwn_ref, oc_ref, gate_ref, slope_ref, o_ref,
                       kbuf, vbuf, sems, m_sc, l_sc, acc_sc, *, past, n_chunks, n_steps, l):
    b = pl.program_id(0)
    c = pl.program_id(1)
    rows = N_HEADS * SUBLANES
    step = b * n_chunks + c
    slot = step % 2

    def chunk_copies(bb, cc, sl):
        return (_gather_pages(pt_ref, bb, cc, poolk_ref, l, kbuf.at[sl], sems.at[0, sl], SEL_PAGES),
                _gather_pages(pt_ref, bb, cc, poolv_ref, l, vbuf.at[sl], sems.at[1, sl], SEL_PAGES))

    @pl.when(step == 0)
    def _():
        for cps in chunk_copies(b, c, slot):
            for cp in cps:
                cp.start()

    @pl.when(step + 1 < n_steps)
    def _():
        wrap = c + 1 == n_chunks
        for cps in chunk_copies(jnp.where(wrap, b + 1, b), jnp.where(wrap, 0, c + 1), 1 - slot):
            for cp in cps:
                cp.start()

    kcopies, vcopies = chunk_copies(b, c, slot)

    @pl.when(c == 0)
    def _():
        m_sc[...] = jnp.full(m_sc.shape, NEG, F32)
        l_sc[...] = jnp.zeros(l_sc.shape, F32)
        acc_sc[...] = jnp.zeros(acc_sc.shape, F32)

    q = (q_ref[...] * SCALE).astype(BF16)
    qpos = past + (lax.broadcasted_iota(jnp.int32, (rows, 1), 0) & (SUBLANES - 1))
    slope = slope_ref[...]

    def online_update(s, mask, v_bf, keys_on_lanes):
        s = jnp.where(mask, s, NEG)
        m_old = m_sc[...]
        m_new = jnp.maximum(m_old, jnp.max(s, axis=-1, keepdims=True))
        alpha = jnp.exp(m_old - m_new)
        p = jnp.where(mask, jnp.exp(s - m_new), 0.0)
        p_bf = p.astype(BF16)
        pv = _nt_dot(p_bf, v_bf) if keys_on_lanes else jnp.dot(p_bf, v_bf, preferred_element_type=F32)
        l_sc[...] = alpha * l_sc[...] + jnp.sum(p, axis=-1, keepdims=True)
        acc_sc[...] = alpha * acc_sc[...] + pv
        m_sc[...] = m_new

    def sel_scores(k_bf, pos0, sel_blocks, keys_on_lanes):
        nk = k_bf.shape[1] if keys_on_lanes else k_bf.shape[0]
        nb = sel_blocks.shape[1]
        kidx = lax.broadcasted_iota(jnp.int32, (1, nk), 1)
        blk = lax.broadcasted_iota(jnp.int32, (nb, nk), 0)
        expand = jnp.where((kidx >> 6) == blk, 1.0, 0.0).astype(BF16)
        chosen = jnp.dot(sel_blocks, expand, preferred_element_type=F32)
        dist = qpos - (pos0 + kidx)
        mask = (dist >= 0) & (chosen > 0.5)
        qk = jnp.dot(q, k_bf, preferred_element_type=F32) if keys_on_lanes else _nt_dot(q, k_bf)
        return qk - slope * dist.astype(F32), mask

    for cp in kcopies:
        cp.wait()
    s, mask = sel_scores(kbuf[slot].astype(BF16), c * (SEL_PAGES * PAGE_SIZE), selc_ref[...], True)
    for cp in vcopies:
        cp.wait()
    online_update(s, mask, vbuf[slot].astype(BF16), True)

    @pl.when(c == n_chunks - 1)
    def _():
        s2, mask2 = sel_scores(kn_ref[...].astype(BF16), past, seln_ref[:, 0:1], False)
        online_update(s2, mask2, vn_ref[...].astype(BF16), False)
        o_s = _fold_groups(acc_sc[...] / jnp.maximum(l_sc[...], 1e-30))

        wb = bk_ref.shape[0]
        kw = jnp.concatenate([bk_ref[...], kwn_ref[...]], axis=0).astype(BF16)
        vw = jnp.concatenate([bv_ref[...], vwn_ref[...]], axis=0).astype(BF16)
        nkw = kw.shape[0]
        kidx = lax.broadcasted_iota(jnp.int32, (1, nkw), 1)
        kpos = past - wb + kidx
        dist = qpos - kpos
        mask_w = (dist >= 0) & (dist < WINDOW) & (kpos >= 0)
        pw = _softmax_rows(_nt_dot(q, kw) - slope * dist.astype(F32), mask_w)
        o_w = _fold_groups(jnp.dot(pw.astype(BF16), vw, preferred_element_type=F32))
        gate = gate_ref[...]
        o_ref[...] = gate[:, 0:1] * oc_ref[...] + gate[:, 1:2] * o_s + gate[:, 2:3] * o_w


def _sample_sel(page_table, pool_k, pool_v, qbd, sel_chunks, k_new, v_new, buf_k, buf_v, kw_new, vw_new,
                o_c, gate_rows, slope_rows, past, l):
    bd, n_pages = page_table.shape
    n_chunks = n_pages // SEL_PAGES
    rows = qbd.shape[1]
    per = sel_chunks.shape[3]
    wb = buf_k.shape[2]
    npad = kw_new.shape[1]
    buf_spec = pl.BlockSpec((None, None, wb, KV_W), lambda b, c, pt: (l, b, 0, 0))
    row3 = lambda n, w: pl.BlockSpec((None, n, w), lambda b, c, pt: (b, 0, 0))
    grid_spec = pltpu.PrefetchScalarGridSpec(
        num_scalar_prefetch=1,
        grid=(bd, n_chunks),
        in_specs=[
            pl.BlockSpec(memory_space=pl.ANY),
            pl.BlockSpec(memory_space=pl.ANY),
            row3(rows, KV_W),
            pl.BlockSpec((None, None, rows, per), lambda b, c, pt: (b, c, 0, 0)),
            pl.BlockSpec((None, None, rows, per), lambda b, c, pt: (b, n_chunks, 0, 0)),
            row3(SEL_BLOCK, KV_W), row3(SEL_BLOCK, KV_W),
            buf_spec, buf_spec,
            row3(npad, KV_W), row3(npad, KV_W),
            row3(rows, HEAD_DIM),
            row3(rows, 3),
            pl.BlockSpec((rows, 1), lambda b, c, pt: (0, 0)),
        ],
        out_specs=row3(rows, HEAD_DIM),
        scratch_shapes=[
            pltpu.VMEM((2, KV_W, SEL_PAGES * PAGE_SIZE), F32),
            pltpu.VMEM((2, KV_W, SEL_PAGES * PAGE_SIZE), F32),
            pltpu.SemaphoreType.DMA((2, 2)),
            pltpu.VMEM((rows, 1), F32), pltpu.VMEM((rows, 1), F32), pltpu.VMEM((rows, KV_W), F32),
        ],
    )
    return pl.pallas_call(
        functools.partial(_sample_sel_kernel, past=past, n_chunks=n_chunks, n_steps=bd * n_chunks, l=l),
        grid_spec=grid_spec,
        out_shape=jax.ShapeDtypeStruct((bd, rows, HEAD_DIM), F32),
        compiler_params=_cparams(("arbitrary", "arbitrary"), 56),
        name="sample_sel",
    )(page_table, pool_k, pool_v, qbd, sel_chunks, sel_chunks, k_new, v_new, buf_k, buf_v, kw_new, vw_new,
      o_c, gate_rows, slope_rows)


def _alibi_slopes():
    h = jnp.arange(1, N_HEADS + 1, dtype=F32)
    return (2.0 ** (-8.0 * h / N_HEADS)).reshape(N_KV, Q_PER_KV)


def _pad_rows(x, n):
    return jnp.pad(x, ((0, 0), (0, n - x.shape[1]), (0, 0)))


def _layer(l, x, mods, p, sample):
    b, t, d = x.shape
    m = b * t
    x2 = x.reshape(m, d)
    tm = min(t, 1024) if sample is None else m
    slopes = _alibi_slopes()

    h1 = _modulate(x2, p['norm1_w'], mods[0], mods[1], t, l)
    proj = _inproj(h1, p['w_in_t'], p['w_gm_t'][l], p['qn'][l], p['kn'][l], l, tm)
    proj3 = proj.reshape(b, t, proj.shape[1])

    def kv_slice(idx):
        return proj3[:, :, COL_KV + idx * KV_W:COL_KV + (idx + 1) * KV_W]

    k_cmp, v_cmp, k_sel, v_sel, k_win, v_win = [kv_slice(i) for i in range(6)]
    nsa_col = proj.shape[1] - TN
    ng = proj3[:, :, nsa_col:nsa_col + 3 * N_HEADS]
    cmp_consts = p['cmp'][l]

    if sample is None:
        o_a, _ = _chunkmix(proj, p['a_spatial_w'][l], p['a_spatial_b'][l].T, p['a_norm_w'][l][None, :], CHUNK, CHUNK)
        kc, vc = _compress(proj3, COL_KV // KV_W, COL_KV // KV_W + 1, cmp_consts)
        nc = t // CMP_BLOCK
        nqb = t // Q_BLOCK
        cols = Q_PER_KV * Q_BLOCK

        def vt(xv, n):
            return xv.reshape(b, n, N_KV, HEAD_DIM).transpose(0, 2, 3, 1).astype(BF16)

        def evens_first(xc):
            return xc.reshape(b, nc // 2, 2, KV_W).transpose(0, 2, 1, 3).reshape(b, nc, KV_W)

        q_t = proj3[:, :, COL_Q:COL_Q + B_WIDTH].reshape(b, nqb, Q_BLOCK, N_KV, Q_PER_KV, HEAD_DIM)
        q_t = q_t.transpose(0, 3, 1, 5, 4, 2).reshape(b, N_KV, nqb, HEAD_DIM, cols)
        ng_t = ng.reshape(b, nqb, Q_BLOCK, 3, N_KV, Q_PER_KV).transpose(0, 4, 1, 3, 5, 2).reshape(b, N_KV, nqb, 3, cols)
        kv_bf = _cast_cols(proj, COL_KV, 6 * KV_W, tm).reshape(b, t, 6 * KV_W)
        o_t = _nsa_prompt(slopes, q_t, evens_first(kc).astype(BF16), vt(evens_first(vc), nc), kv_bf,
                          vt(v_sel, t), vt(v_win, t), ng_t)
        o_b = o_t.reshape(b, N_KV, nqb, HEAD_DIM, Q_PER_KV, Q_BLOCK).transpose(0, 2, 5, 1, 4, 3).reshape(m, B_WIDTH)
        v_norm = None
        wp = min(WINDOW, t)
        new_bk, new_bv = k_win[:, -wp:], v_win[:, -wp:]
    else:
        ds = t
        eye = jnp.eye(b, dtype=F32)
        w8 = p['a_spatial_w'][l][:, :ds, :ds]
        w_bd = jnp.einsum('ab,gts->gatbs', eye, w8).reshape(A_GROUPS, m, m)
        b_t = jnp.tile(p['a_spatial_b'][l][:, :ds].T, (b, 1))
        o_a, v_norm = _chunkmix(proj, w_bd, b_t, p['a_norm_w'][l][None, :], m, ds)

        page_table = sample['page_table']
        n_pages = page_table.shape[1]
        past = n_pages * PAGE_SIZE
        new_pad = -(-ds // SEL_BLOCK) * SEL_BLOCK
        nc = (past + new_pad) // CMP_BLOCK
        ns = (past + new_pad) // SEL_BLOCK
        kc_past, vc_past = _gather_compress(page_table, sample['cache_k_cmp'], sample['cache_v_cmp'], cmp_consts, l)
        new_rows = jnp.concatenate([_pad_rows(k_cmp, new_pad), _pad_rows(v_cmp, new_pad)], axis=2)
        kc_new, vc_new = _compress(new_rows, 0, 1, cmp_consts)
        ncp = -(-nc // LANES) * LANES
        kc_full = _pad_rows(jnp.concatenate([kc_past, kc_new], axis=1), ncp)
        vc_full = _pad_rows(jnp.concatenate([vc_past, vc_new], axis=1), ncp)

        q5 = proj3[:, :, COL_Q:COL_Q + B_WIDTH].reshape(b, ds, N_KV, Q_PER_KV, HEAD_DIM).transpose(0, 2, 3, 1, 4)
        qbd = jnp.einsum('bgrqd,gh->bgrqhd', q5, jnp.eye(N_KV, dtype=F32)).reshape(b, N_HEADS * ds, KV_W)
        slope_rows = jnp.repeat(slopes.reshape(N_HEADS), ds)[:, None]
        gate_rows = ng.reshape(b, ds, 3, N_HEADS).transpose(0, 3, 1, 2).reshape(b, N_HEADS * ds, 3)
        o_c, sel_chunks = _sample_cmp(qbd, kc_full, vc_full, slope_rows, past, nc, ns)
        o_rows = _sample_sel(page_table, sample['cache_k_sel'], sample['cache_v_sel'], qbd, sel_chunks,
                             _pad_rows(k_sel, new_pad), _pad_rows(v_sel, new_pad), sample['cache_k_win'],
                             sample['cache_v_win'], _pad_rows(k_win, LANES), _pad_rows(v_win, LANES), o_c,
                             gate_rows, slope_rows, past, l)
        buf_k, buf_v = sample['cache_k_win'][l], sample['cache_v_win'][l]
        o_b = o_rows.reshape(b, N_HEADS, ds, HEAD_DIM).transpose(0, 2, 1, 3).reshape(m, B_WIDTH).astype(BF16)
        keep = min(WINDOW, buf_k.shape[1] + ds)
        new_bk = jnp.concatenate([buf_k, k_win], axis=1)[:, -keep:]
        new_bv = jnp.concatenate([buf_v, v_win], axis=1)[:, -keep:]

    def gate_of(vec):
        return vec if sample is None else jnp.repeat(vec, t, axis=0)

    mix = _branch(o_a, o_b, p['w_branch'], proj, l, tm)
    x_mid = _resid_matmul(mix, p['w_out'], x2, gate_of(mods[2]), l, tm, 'attn_out')
    h2 = _modulate(x_mid, p['norm2_w'], mods[3], mods[4], t, l)
    if sample is None:
        s0 = s1 = None
    else:
        state = sample['state_ffn_conv'][l]
        s0 = jnp.repeat(state[:, 0], t, axis=0)
        s1 = jnp.repeat(state[:, 1], t, axis=0)
    hid, tails = _ffn_up(h2, p['ffn_w_a'], p['ffn_w_b'], p['ffn_conv_w'], p['ffn_conv_b'], s0, s1, t, l, tm)
    x_out = _resid_matmul(hid, p['ffn_w_down'], x_mid, gate_of(mods[5]), l, tm, 'ffn_down')

    if sample is None:
        conv_rows = tails.reshape(b, t // tm, SUBLANES, -1)[:, -1, SUBLANES - (CONV_W - 1):]
    else:
        conv_rows = tails.reshape(b, t, -1)[:, t - (CONV_W - 1):]
    heads = lambda z: z.reshape(b, -1, N_KV, HEAD_DIM)
    state_out = dict(k_cmp=heads(k_cmp), v_cmp=heads(v_cmp), k_sel=heads(k_sel), v_sel=heads(v_sel),
                     k_win=heads(new_bk), v_win=heads(new_bv), conv=conv_rows)
    if sample is not None:
        state_out['chunk_v'] = v_norm.reshape(b, t, A_WIDTH)
    return x_out.reshape(b, t, d), state_out


def kernel(x_prompt, x_sample, cache_k_cmp, cache_v_cmp, cache_k_sel, cache_v_sel, cache_k_win, cache_v_win,
           state_ffn_conv, page_table, c_prompt, c_sample, w_ada, b_ada, norm1_w, norm2_w, w_in, a_norm_w,
           a_spatial_w, a_spatial_b, q_norm_w, k_norm_w, cmp_pe_k, cmp_w_k, cmp_pe_v, cmp_w_v, w_branch, w_out,
           ffn_w_a, ffn_w_b, ffn_conv_w, ffn_conv_b, ffn_w_down):
    depth, d, n_in = w_in.shape
    bp = x_prompt.shape[0]
    bs = x_sample.shape[0]

    rows = -(-(bp + bs) // SUBLANES) * SUBLANES
    c_all = jnp.pad(jnp.concatenate([c_prompt, c_sample], axis=0), ((0, rows - bp - bs), (0, 0)))
    mods_all = _ada(c_all, w_ada, b_ada)

    n_merge = n_in - N_MAIN - 3 * N_HEADS
    pad = (-(n_merge + 3 * N_HEADS)) % TN
    w_in_t = jnp.swapaxes(w_in, 1, 2)
    w_gm_t = [jnp.concatenate([w_in_t[l, N_MAIN + 3 * N_HEADS:], w_in_t[l, N_MAIN:N_MAIN + 3 * N_HEADS],
                               jnp.zeros((pad, d), F32)], axis=0) for l in range(depth)]
    shared = dict(
        norm1_w=norm1_w, norm2_w=norm2_w, w_in_t=w_in_t, w_gm_t=w_gm_t,
        qn=jnp.tile(q_norm_w, (1, TN // HEAD_DIM))[:, None, :], kn=jnp.tile(k_norm_w, (1, TN // HEAD_DIM))[:, None, :],
        a_norm_w=a_norm_w, a_spatial_w=a_spatial_w, a_spatial_b=a_spatial_b,
        cmp=[_cmp_consts(cmp_pe_k[l], cmp_w_k[l], cmp_pe_v[l], cmp_w_v[l], k_norm_w[l]) for l in range(depth)],
        w_branch=w_branch, w_out=w_out, ffn_w_a=ffn_w_a, ffn_w_b=ffn_w_b, ffn_conv_w=ffn_conv_w,
        ffn_conv_b=ffn_conv_b, ffn_w_down=ffn_w_down)
    sample = dict(page_table=page_table, cache_k_cmp=cache_k_cmp, cache_v_cmp=cache_v_cmp, cache_k_sel=cache_k_sel,
                  cache_v_sel=cache_v_sel, cache_k_win=cache_k_win, cache_v_win=cache_v_win,
                  state_ffn_conv=state_ffn_conv)
    for name in ('cache_k_cmp', 'cache_v_cmp', 'cache_k_sel', 'cache_v_sel'):
        pool = sample[name]
        sample[name] = pool.transpose(0, 1, 3, 4, 2).reshape(pool.shape[0], pool.shape[1], KV_W, pool.shape[2])
    for name in ('cache_k_win', 'cache_v_win'):
        buf = sample[name]
        sample[name] = buf.reshape(buf.shape[0], buf.shape[1], buf.shape[2], KV_W)

    xp, xs = x_prompt, x_sample
    st_p, st_s = [], []
    for l in range(depth):
        mods = mods_all[l]
        mp = [mods[:bp, k * d:(k + 1) * d] for k in range(6)]
        ms = [mods[bp:bp + bs, k * d:(k + 1) * d] for k in range(6)]
        xp, sp = _layer(l, xp, mp, shared, None)
        xs, ss = _layer(l, xs, ms, shared, sample)
        st_p.append(sp)
        st_s.append(ss)

    def stack(sts, key):
        return jnp.stack([s[key] for s in sts], axis=0)

    return (xp, xs,
            stack(st_p, 'k_cmp'), stack(st_p, 'v_cmp'), stack(st_p, 'k_sel'), stack(st_p, 'v_sel'),
            stack(st_p, 'k_win'), stack(st_p, 'v_win'), stack(st_p, 'conv'),
            stack(st_s, 'k_cmp'), stack(st_s, 'v_cmp'), stack(st_s, 'k_sel'), stack(st_s, 'v_sel'),
            stack(st_s, 'k_win'), stack(st_s, 'v_win'), stack(st_s, 'chunk_v'), stack(st_s, 'conv'))
```

```python
import functools

import jax
import jax.numpy as jnp
from jax import lax
from jax.experimental import pallas as pl
from jax.experimental.pallas import tpu as pltpu

F32 = jnp.float32
BF16 = jnp.bfloat16

HEAD_DIM = 64
N_KV = 4
Q_PER_KV = 4
N_HEADS = N_KV * Q_PER_KV
KV_W = N_KV * HEAD_DIM
A_GROUPS = 8
A_GROUP_DIM = 128
A_WIDTH = A_GROUPS * A_GROUP_DIM
B_WIDTH = N_HEADS * HEAD_DIM
CHUNK = 128
CMP_BLOCK = 32
SEL_BLOCK = 64
TOP_N = 16
WINDOW = 512
Q_BLOCK = 128
PAGE_SIZE = 128
CONV_W = 3
SCALE = HEAD_DIM ** -0.5
EPS = 1e-6
NEG = -1e30
FORCE = 1e4
LOG2E = 1.4426950408889634

LANES = 128
SUBLANES = 8
MIB = 1024 * 1024

TN = 512
COL_U = 0
COL_V = A_WIDTH
COL_Q = 2 * A_WIDTH
COL_KV = COL_Q + B_WIDTH
N_MAIN = COL_KV + 6 * KV_W
N_MAIN_BLOCKS = N_MAIN // TN
SEL_PAGES = 32
CMP_PAGES = 32
SUB_ROWS = 256


def _cparams(sem, vmem_mib):
    return pltpu.CompilerParams(dimension_semantics=sem, vmem_limit_bytes=vmem_mib * MIB)


def _group_rms(x, gmat, w):
    y = x * x
    hi = y.astype(BF16)
    lo = (y - hi.astype(F32)).astype(BF16)
    ms = jnp.dot(hi, gmat, preferred_element_type=F32) + jnp.dot(lo, gmat, preferred_element_type=F32)
    return x * lax.rsqrt(ms + EPS) * w


def _group_mean_matrix(n):
    r = jnp.arange(n)
    return jnp.where((r[:, None] // HEAD_DIM) == (r[None, :] // HEAD_DIM), 1.0 / HEAD_DIM, 0.0).astype(BF16)


def _ada_kernel(c_ref, w_ref, b_ref, o_ref):
    c = c_ref[...]
    s = (c * jax.nn.sigmoid(c)).astype(BF16)
    o_ref[...] = jnp.dot(s, w_ref[...].astype(BF16), preferred_element_type=F32) + b_ref[...]


def _ada(c_all, w_ada, b_ada):
    depth, d, n6 = w_ada.shape
    rows = c_all.shape[0]
    tn = 1024
    return pl.pallas_call(
        _ada_kernel,
        grid=(depth, n6 // tn),
        in_specs=[
            pl.BlockSpec((rows, d), lambda l, j: (0, 0)),
            pl.BlockSpec((None, d, tn), lambda l, j: (l, 0, j)),
            pl.BlockSpec((None, 1, tn), lambda l, j: (l, 0, j)),
        ],
        out_specs=pl.BlockSpec((None, rows, tn), lambda l, j: (l, 0, j)),
        out_shape=jax.ShapeDtypeStruct((depth, rows, n6), F32),
        compiler_params=_cparams(("arbitrary", "arbitrary"), 40),
        name="ada",
    )(c_all, w_ada, b_ada.reshape(depth, 1, n6))


def _modulate_kernel(x_ref, g_ref, sh_ref, sc_ref, o_ref):
    x = x_ref[...]
    y = x * lax.rsqrt(jnp.mean(x * x, axis=-1, keepdims=True) + EPS)
    y = y * g_ref[...]
    o_ref[...] = (y * (1.0 + sc_ref[...]) + sh_ref[...]).astype(BF16)


def _modulate(x2, g, shift, scale, t, l):
    m, d = x2.shape
    nb = m // t
    tm = min(t, 512)
    per = t // tm
    return pl.pallas_call(
        _modulate_kernel,
        grid=(m // tm,),
        in_specs=[
            pl.BlockSpec((tm, d), lambda i: (i, 0)),
            pl.BlockSpec((None, 1, d), lambda i: (l, 0, 0)),
            pl.BlockSpec((None, 1, d), lambda i: (i // per, 0, 0)),
            pl.BlockSpec((None, 1, d), lambda i: (i // per, 0, 0)),
        ],
        out_specs=pl.BlockSpec((tm, d), lambda i: (i, 0)),
        out_shape=jax.ShapeDtypeStruct((m, d), BF16),
        compiler_params=_cparams(("arbitrary",), 32),
        name="modulate",
    )(x2, g.reshape(g.shape[0], 1, d), shift.reshape(nb, 1, d), scale.reshape(nb, 1, d))


def _inproj_kernel(h_ref, w1_ref, w2_ref, qn_ref, kn_ref, gm_ref, o_ref, wbf_ref):
    j = pl.program_id(0)
    i = pl.program_id(1)

    @pl.when((i == 0) & (j < N_MAIN_BLOCKS))
    def _():
        wbf_ref[...] = w1_ref[...].T.astype(BF16)

    @pl.when((i == 0) & (j >= N_MAIN_BLOCKS))
    def _():
        wbf_ref[...] = w2_ref[0].T.astype(BF16)

    tm = h_ref.shape[0]
    sub = min(tm, SUB_ROWS)

    def project(epilogue):
        for c in range(tm // sub):
            rows = slice(c * sub, (c + 1) * sub)
            o_ref[rows, :] = epilogue(jnp.dot(h_ref[rows, :], wbf_ref[...], preferred_element_type=F32))

    @pl.when(j < 4)
    def _():
        project(jax.nn.gelu)

    @pl.when((j == 4) | (j == 5))
    def _():
        project(lambda acc: _group_rms(acc, gm_ref[...], qn_ref[...]))

    @pl.when(j == 6)
    def _():
        project(lambda acc: acc)

    @pl.when((j == 7) | (j == 8))
    def _():
        lane = lax.broadcasted_iota(jnp.int32, (sub, TN), 1)
        project(lambda acc: jnp.where(lane < KV_W, _group_rms(acc, gm_ref[...], kn_ref[...]), acc))

    @pl.when(j >= N_MAIN_BLOCKS)
    def _():
        project(jax.nn.sigmoid)


def _inproj(h, w_in_t, qn, kn, l, tm):
    m, d = h.shape
    n_in = w_in_t.shape[1]
    n_merge = n_in - N_MAIN - 3 * N_HEADS
    n_gm = n_merge // TN + 1
    nj = N_MAIN_BLOCKS + n_gm
    gm = _group_mean_matrix(TN)

    def gate_rows(j, i):
        k = jnp.clip(j - N_MAIN_BLOCKS, 0, n_gm - 1)
        return (l, pl.multiple_of(jnp.where(k == n_gm - 1, N_MAIN, N_MAIN + 3 * N_HEADS + k * TN), SUBLANES), 0)

    return pl.pallas_call(
        _inproj_kernel,
        grid=(nj, m // tm),
        in_specs=[
            pl.BlockSpec((tm, d), lambda j, i: (i, 0)),
            pl.BlockSpec((None, TN, d), lambda j, i: (l, jnp.minimum(j, N_MAIN_BLOCKS - 1), 0)),
            pl.BlockSpec((pl.Element(1), pl.Element(TN), pl.Element(d)), gate_rows),
            pl.BlockSpec((1, TN), lambda j, i: (0, 0)),
            pl.BlockSpec((1, TN), lambda j, i: (0, 0)),
            pl.BlockSpec((TN, TN), lambda j, i: (0, 0)),
        ],
        out_specs=pl.BlockSpec((tm, TN), lambda j, i: (i, j)),
        out_shape=jax.ShapeDtypeStruct((m, nj * TN), F32),
        scratch_shapes=[pltpu.VMEM((d, TN), BF16)],
        compiler_params=_cparams(("arbitrary", "arbitrary"), 48),
        name="inproj",
    )(h, w_in_t, w_in_t, qn, kn, gm)


def _kv_prep_kernel(x_ref, kv_ref, vst_ref, vwt_ref):
    x = x_ref[...]
    kv_ref[...] = x.astype(BF16)
    vst_ref[...] = x[:, 3 * KV_W:4 * KV_W].T.astype(BF16)
    vwt_ref[...] = x[:, 5 * KV_W:6 * KV_W].T.astype(BF16)


def _kv_prep(proj, b, t, tm):
    m = proj.shape[0]
    width = 6 * KV_W
    per = t // tm
    vt_spec = pl.BlockSpec((None, KV_W, tm), lambda i: (i // per, 0, i % per))
    return pl.pallas_call(
        _kv_prep_kernel,
        grid=(m // tm,),
        in_specs=[pl.BlockSpec((tm, width), lambda i: (i, COL_KV // width))],
        out_specs=[pl.BlockSpec((tm, width), lambda i: (i, 0)), vt_spec, vt_spec],
        out_shape=[jax.ShapeDtypeStruct((m, width), BF16), jax.ShapeDtypeStruct((b, KV_W, t), BF16),
                   jax.ShapeDtypeStruct((b, KV_W, t), BF16)],
        compiler_params=_cparams(("arbitrary",), 40),
        name="kv_prep",
    )(proj)


def _chunkmix_kernel(u_ref, gv_ref, w_ref, bt_ref, an_ref, oa_ref, v_ref, *, lc, causal_block):
    gv = gv_ref[...]
    v = gv * lax.rsqrt(jnp.mean(gv * gv, axis=-1, keepdims=True) + EPS) * an_ref[...]
    v_ref[...] = v
    row = lax.broadcasted_iota(jnp.int32, (lc, lc), 0)
    col = lax.broadcasted_iota(jnp.int32, (lc, lc), 1)
    shift = causal_block.bit_length() - 1
    mask = (col <= row) & ((row >> shift) == (col >> shift))
    vb = v.astype(BF16)
    for g in range(A_GROUPS):
        sl = slice(g * A_GROUP_DIM, (g + 1) * A_GROUP_DIM)
        wg = jnp.where(mask, w_ref[g], 0.0).astype(BF16)
        s = jnp.dot(wg, vb[:, sl], preferred_element_type=F32) + bt_ref[:, g:g + 1]
        oa_ref[:, sl] = (u_ref[:, sl] * s).astype(BF16)


def _chunkmix(proj, w_s, b_t, a_norm, lc, causal_block):
    m = proj.shape[0]
    return pl.pallas_call(
        functools.partial(_chunkmix_kernel, lc=lc, causal_block=causal_block),
        grid=(m // lc,),
        in_specs=[
            pl.BlockSpec((lc, A_WIDTH), lambda c: (c, 0)),
            pl.BlockSpec((lc, A_WIDTH), lambda c: (c, 1)),
            pl.BlockSpec((A_GROUPS, lc, lc), lambda c: (0, 0, 0)),
            pl.BlockSpec((lc, A_GROUPS), lambda c: (0, 0)),
            pl.BlockSpec((1, A_WIDTH), lambda c: (0, 0)),
        ],
        out_specs=[
            pl.BlockSpec((lc, A_WIDTH), lambda c: (c, 0)),
            pl.BlockSpec((lc, A_WIDTH), lambda c: (c, 0)),
        ],
        out_shape=[jax.ShapeDtypeStruct((m, A_WIDTH), BF16), jax.ShapeDtypeStruct((m, A_WIDTH), F32)],
        compiler_params=_cparams(("arbitrary",), 32),
        name="chunkmix",
    )(proj, proj, w_s, b_t, a_norm)


def _compress_rows(half_refs, n_blk, bd_ref, pe_ref):
    halves = []
    for ref in half_refs:
        acc = jnp.zeros((n_blk, LANES), F32)
        for i in range(CMP_BLOCK):
            rows = ref[pl.ds(i, n_blk, stride=CMP_BLOCK), :] + pe_ref[i:i + 1, :]
            acc = acc + jnp.dot(rows.astype(BF16), bd_ref[i], preferred_element_type=F32)
        halves.append(acc)
    return jnp.concatenate(halves, axis=1)


def _compress_kernel(k0_ref, k1_ref, v0_ref, v1_ref, bdk_ref, bdv_ref, pek_ref, pev_ref, knw_ref, gm_ref,
                     kc_ref, vc_ref, *, nc):
    kc_ref[...] = _group_rms(_compress_rows((k0_ref, k1_ref), nc, bdk_ref, pek_ref), gm_ref[...], knw_ref[...])
    vc_ref[...] = _compress_rows((v0_ref, v1_ref), nc, bdv_ref, pev_ref)


def _compress(src3, kblk, vblk, cmp_consts):
    b, t, _ = src3.shape
    nc = t // CMP_BLOCK
    bdk, bdv, pek, pev, knw, gm = cmp_consts
    const3 = lambda i: (0, 0, 0)
    const2 = lambda i: (0, 0)
    half = lambda blk, h: pl.BlockSpec((None, t, LANES), lambda i: (i, 0, 2 * blk + h))
    return pl.pallas_call(
        functools.partial(_compress_kernel, nc=nc),
        grid=(b,),
        in_specs=[
            half(kblk, 0), half(kblk, 1), half(vblk, 0), half(vblk, 1),
            pl.BlockSpec((CMP_BLOCK, LANES, LANES), const3),
            pl.BlockSpec((CMP_BLOCK, LANES, LANES), const3),
            pl.BlockSpec((CMP_BLOCK, LANES), const2),
            pl.BlockSpec((CMP_BLOCK, LANES), const2),
            pl.BlockSpec((1, KV_W), const2),
            pl.BlockSpec((KV_W, KV_W), const2),
        ],
        out_specs=[pl.BlockSpec((None, nc, KV_W), lambda i: (i, 0, 0))] * 2,
        out_shape=[jax.ShapeDtypeStruct((b, nc, KV_W), F32)] * 2,
        compiler_params=_cparams(("arbitrary",), 40),
        name="compress",
    )(src3, src3, src3, src3, bdk, bdv, pek, pev, knw, gm)


def _cmp_consts(cmp_pe_k, cmp_w_k, cmp_pe_v, cmp_w_v, k_norm_w):
    per = LANES // HEAD_DIM
    eye = jnp.eye(per, dtype=F32)

    def bd(w):
        return jnp.einsum('gh,ide->igdhe', eye, w).reshape(CMP_BLOCK, LANES, LANES).astype(BF16)

    return (bd(cmp_w_k), bd(cmp_w_v), jnp.tile(cmp_pe_k, (1, per)), jnp.tile(cmp_pe_v, (1, per)),
            jnp.tile(k_norm_w, N_KV)[None, :], _group_mean_matrix(KV_W))


def _pair_sum(psum, pair):
    hi = psum.astype(BF16)
    lo = (psum - hi.astype(F32)).astype(BF16)
    return jnp.dot(hi, pair, preferred_element_type=F32) + jnp.dot(lo, pair, preferred_element_type=F32)


def _top_mask(score, n_valid, top_n):
    rows, width = score.shape
    jj = lax.broadcasted_iota(jnp.int32, (rows, width), 1)
    rank = jnp.zeros((rows, width), F32)
    for k in range(n_valid):
        ck = score[:, k:k + 1]
        beats = (ck > score) | ((ck == score) & (jj > k))
        rank = rank + jnp.where(beats, 1.0, 0.0)
    return jnp.where((rank < top_n) & (jj < n_valid), 1.0, 0.0)


def _softmax_rows(s, mask):
    s = jnp.where(mask, s, NEG)
    m = jnp.max(s, axis=-1, keepdims=True)
    p = jnp.where(mask, jnp.exp(s - m), 0.0)
    return p / jnp.maximum(jnp.sum(p, axis=-1, keepdims=True), 1e-30)


def _softmax2_cols(s2, mask):
    s2 = jnp.where(mask, s2, NEG)
    m = jnp.max(s2, axis=0, keepdims=True)
    p = jnp.where(mask, jnp.exp2(s2 - m), 0.0)
    return p / jnp.maximum(jnp.sum(p, axis=0, keepdims=True), 1e-30)


def _nsa_prompt_kernel(slopes2_ref, q_ref, ng_ref, kc_ref, vct_ref, ks_ref, vst_ref, kw_ref, vwt_ref, bias_ref,
                       o_ref, qbd_sc, sel_sc, m_sc, l_sc, acc_sc, *, t, tk):
    i = pl.program_id(1)
    nc = t // CMP_BLOCK
    ns = t // SEL_BLOCK
    half = nc // 2
    start = i * Q_BLOCK
    groups = range(N_KV)
    lane_blk = [slice(r * Q_BLOCK, (r + 1) * Q_BLOCK) for r in range(Q_PER_KV)]
    slopes2 = [[slopes2_ref[g, r] for r in range(Q_PER_KV)] for g in groups]
    qpos = start + lax.broadcasted_iota(jnp.int32, (1, Q_BLOCK), 1)

    qbd_sc[...] = jnp.zeros(qbd_sc.shape, BF16)
    qt = (q_ref[...].T * (SCALE * LOG2E)).astype(BF16)
    for g in groups:
        for r in range(Q_PER_KV):
            h = g * Q_PER_KV + r
            qbd_sc[g, g * HEAD_DIM:(g + 1) * HEAD_DIM, lane_blk[r]] = qt[h * HEAD_DIM:(h + 1) * HEAD_DIM, :]
    ngt = ng_ref[...].T

    def gate(branch, g):
        r0 = branch * N_HEADS + g * Q_PER_KV
        return jnp.concatenate([ngt[r0 + r:r0 + r + 1, :] for r in range(Q_PER_KV)], axis=1)

    row = lax.broadcasted_iota(jnp.int32, (nc, 1), 0)
    blk_c = jnp.where(row < half, 2 * row, 2 * (row - half) + 1)
    dist_c = qpos - ((blk_c + 1) * CMP_BLOCK - 1)
    mask_c = dist_c >= 0
    dist_cf = dist_c.astype(F32)
    jj = lax.broadcasted_iota(jnp.int32, (ns, 1), 0)
    cur = qpos >> 6
    forced = (jj == 0) | (jj == cur) | (jj == cur - 1)
    kc = kc_ref[...]
    o_c = []
    for g in groups:
        st = jnp.dot(kc, qbd_sc[g], preferred_element_type=F32)
        p_heads = [_softmax2_cols(st[:, lane_blk[r]] - slopes2[g][r] * dist_cf, mask_c) for r in range(Q_PER_KV)]
        o_c.append(jnp.dot(vct_ref[g], jnp.concatenate(p_heads, axis=1).astype(BF16), preferred_element_type=F32))
        psum = ((p_heads[0] + p_heads[1]) + p_heads[2]) + p_heads[3]
        imp = psum[:half] + psum[half:]
        score = jnp.where(jj <= cur, imp + jnp.where(forced, FORCE, 0.0), NEG)
        rank = jnp.zeros((ns, Q_BLOCK), F32)
        for k in range(ns):
            sk = score[k:k + 1, :]
            lo = (k // SUBLANES) * SUBLANES
            hi = lo + SUBLANES
            one = lambda cond: jnp.where(cond, 1.0, 0.0)
            rank = rank + jnp.concatenate(
                ([one(sk > score[:lo])] if lo else [])
                + [jnp.where(jj[lo:hi] > k, one(sk >= score[lo:hi]), one(sk > score[lo:hi]))]
                + ([one(sk >= score[hi:])] if hi < ns else []), axis=0)
        sel_sc[g] = jnp.where(rank < min(TOP_N, ns), 0.0, NEG)

    m_sc[...] = jnp.full(m_sc.shape, NEG, F32)
    l_sc[...] = jnp.zeros(l_sc.shape, F32)
    acc_sc[...] = jnp.zeros(acc_sc.shape, F32)
    blocks_per_tile = tk // SEL_BLOCK
    last = start // tk

    def sel_tile(ti, causal):
        k0 = pl.multiple_of(ti * tk, tk)
        kt = ks_ref[pl.ds(k0, tk), :]
        off = start - k0
        offf = off.astype(F32)
        if causal:
            rel = lax.broadcasted_iota(jnp.int32, (tk, Q_BLOCK), 1) - lax.broadcasted_iota(jnp.int32, (tk, Q_BLOCK), 0)
            future = rel + off < 0
        for g in groups:
            s = jnp.dot(kt, qbd_sc[g], preferred_element_type=F32)
            sel_rows = sel_sc[g, pl.ds(pl.multiple_of(ti * blocks_per_tile, blocks_per_tile), blocks_per_tile), :]
            mask_bias = jnp.concatenate(
                [jnp.broadcast_to(sel_rows[jb:jb + 1, :], (SEL_BLOCK, Q_BLOCK)) for jb in range(blocks_per_tile)],
                axis=0)
            if causal:
                mask_bias = jnp.where(future, NEG, mask_bias)
            m_old = m_sc[g]
            p_parts, m_parts, sum_parts = [], [], []
            for r in range(Q_PER_KV):
                c_r = slopes2[g][r] * offf
                sr = (s[:, lane_blk[r]] - bias_ref[g, r, 0:tk, :]) + mask_bias
                m_new = jnp.maximum(m_old[:, lane_blk[r]], jnp.max(sr, axis=0, keepdims=True) - c_r)
                p = jnp.exp2(sr - (m_new + c_r))
                m_parts.append(m_new)
                sum_parts.append(jnp.sum(p, axis=0, keepdims=True))
                p_parts.append(p.astype(BF16))
            m_new = jnp.concatenate(m_parts, axis=1)
            alpha = jnp.exp2(m_old - m_new)
            l_sc[g] = alpha * l_sc[g] + jnp.concatenate(sum_parts, axis=1)
            pv = jnp.dot(vst_ref[g * HEAD_DIM:(g + 1) * HEAD_DIM, pl.ds(k0, tk)], jnp.concatenate(p_parts, axis=1),
                         preferred_element_type=F32)
            acc_sc[g] = alpha * acc_sc[g] + pv
            m_sc[g] = m_new

    def full_tile(ti, carry):
        sel_tile(ti, False)
        return carry

    lax.fori_loop(0, last, full_tile, 0)
    sel_tile(last, True)

    wk = WINDOW + Q_BLOCK
    k0w = pl.multiple_of(jnp.maximum(start - WINDOW, 0), Q_BLOCK)
    kwt = kw_ref[pl.ds(k0w, wk), :]
    dist_w = qpos - (k0w + lax.broadcasted_iota(jnp.int32, (wk, 1), 0))
    wbias = jnp.where((dist_w >= 0) & (dist_w < WINDOW), 0.0, NEG)
    out_rows = []
    for g in groups:
        sw = jnp.dot(kwt, qbd_sc[g], preferred_element_type=F32)
        pw, lw = [], []
        for r in range(Q_PER_KV):
            sr = (sw[:, lane_blk[r]] - bias_ref[g, r, 0:wk, :]) + wbias
            p = jnp.exp2(sr - jnp.max(sr, axis=0, keepdims=True))
            lw.append(jnp.sum(p, axis=0, keepdims=True))
            pw.append(p.astype(BF16))
        o_w = jnp.dot(vwt_ref[g * HEAD_DIM:(g + 1) * HEAD_DIM, pl.ds(k0w, wk)], jnp.concatenate(pw, axis=1),
                      preferred_element_type=F32)
        o_w = o_w / jnp.maximum(jnp.concatenate(lw, axis=1), 1e-30)
        o_s = acc_sc[g] / jnp.maximum(l_sc[g], 1e-30)
        o_g = gate(0, g) * o_c[g] + gate(1, g) * o_s + gate(2, g) * o_w
        out_rows += [o_g[:, lane_blk[r]] for r in range(Q_PER_KV)]
    o_ref[...] = jnp.concatenate(out_rows, axis=0).T.astype(BF16)


def _pair_matrix(nc, ns_pad):
    n = jnp.arange(nc)[:, None]
    j = jnp.arange(ns_pad)[None, :]
    return jnp.where((n // (SEL_BLOCK // CMP_BLOCK)) == j, 1.0, 0.0).astype(BF16)


def _nsa_prompt(slopes, proj3, kc_perm, vc_t, kv_bf, vs_t, vw_t):
    b, t, c = proj3.shape
    nqb = t // Q_BLOCK
    cols = Q_PER_KV * Q_BLOCK
    nc = t // CMP_BLOCK
    ns = t // SEL_BLOCK
    tk = min(512, t)
    wk = WINDOW + Q_BLOCK
    slopes2 = slopes * LOG2E
    rel = (jnp.arange(Q_BLOCK)[None, :] - jnp.arange(wk)[:, None]).astype(F32)
    bias_tab = slopes2[:, :, None, None] * rel[None, None]
    vt_spec = pl.BlockSpec((None, KV_W, t), lambda bi, i: (bi, 0, 0))
    kv_spec = lambda col: pl.BlockSpec((None, t, KV_W), lambda bi, i: (bi, 0, col))
    return pl.pallas_call(
        functools.partial(_nsa_prompt_kernel, t=t, tk=tk),
        grid=(b, nqb),
        in_specs=[
            pl.BlockSpec(memory_space=pltpu.SMEM),
            pl.BlockSpec((None, Q_BLOCK, B_WIDTH), lambda bi, i: (bi, i, COL_Q // B_WIDTH)),
            pl.BlockSpec((None, Q_BLOCK, TN), lambda bi, i: (bi, i, c // TN - 1)),
            pl.BlockSpec((None, nc, KV_W), lambda bi, i: (bi, 0, 0)),
            pl.BlockSpec((None, N_KV, HEAD_DIM, nc), lambda bi, i: (bi, 0, 0, 0)),
            kv_spec(2), vt_spec, kv_spec(4), vt_spec,
            pl.BlockSpec((N_KV, Q_PER_KV, wk, Q_BLOCK), lambda bi, i: (0, 0, 0, 0)),
        ],
        out_specs=pl.BlockSpec((None, Q_BLOCK, B_WIDTH), lambda bi, i: (bi, i, 0)),
        out_shape=jax.ShapeDtypeStruct((b, t, B_WIDTH), BF16),
        scratch_shapes=[pltpu.VMEM((N_KV, KV_W, cols), BF16), pltpu.VMEM((N_KV, ns, Q_BLOCK), F32),
                        pltpu.VMEM((N_KV, 1, cols), F32), pltpu.VMEM((N_KV, 1, cols), F32),
                        pltpu.VMEM((N_KV, HEAD_DIM, cols), F32)],
        compiler_params=_cparams(("arbitrary", "arbitrary"), 56),
        name="nsa_prompt",
    )(slopes2, proj3, proj3, kc_perm, vc_t, kv_bf, vs_t, kv_bf, vw_t, bias_tab)


def _branch_kernel(oa_ref, ob_ref, w_ref, g0_ref, g1_ref, o_ref, wbf_ref):
    @pl.when(pl.program_id(1) == 0)
    def _():
        wbf_ref[...] = w_ref[...].astype(BF16)

    tm = oa_ref.shape[0]
    sub = min(tm, SUB_ROWS)
    for c in range(tm // sub):
        rows = slice(c * sub, (c + 1) * sub)
        pa = jnp.dot(oa_ref[rows, :], wbf_ref[:A_WIDTH, :], preferred_element_type=F32)
        pb = jnp.dot(ob_ref[rows, :], wbf_ref[A_WIDTH:, :], preferred_element_type=F32)
        o_ref[rows, :] = (g0_ref[rows, :] * pa + g1_ref[rows, :] * pb).astype(BF16)


def _branch(o_a, o_b, w_branch, proj, l, tm):
    m = o_a.shape[0]
    kb, d = w_branch.shape[1:]
    nj = d // TN
    return pl.pallas_call(
        _branch_kernel,
        grid=(nj, m // tm),
        in_specs=[
            pl.BlockSpec((tm, A_WIDTH), lambda j, i: (i, 0)),
            pl.BlockSpec((tm, B_WIDTH), lambda j, i: (i, 0)),
            pl.BlockSpec((None, kb, TN), lambda j, i: (l, 0, j)),
            pl.BlockSpec((tm, TN), lambda j, i: (i, N_MAIN_BLOCKS + j)),
            pl.BlockSpec((tm, TN), lambda j, i: (i, N_MAIN_BLOCKS + nj + j)),
        ],
        out_specs=pl.BlockSpec((tm, TN), lambda j, i: (i, j)),
        out_shape=jax.ShapeDtypeStruct((m, d), BF16),
        scratch_shapes=[pltpu.VMEM((kb, TN), BF16)],
        compiler_params=_cparams(("arbitrary", "arbitrary"), 48),
        name="branch",
    )(o_a, o_b, w_branch, proj, proj)


def _resid_kernel(a_ref, w_ref, x_ref, g_ref, o_ref, wbf_ref):
    @pl.when(pl.program_id(1) == 0)
    def _():
        wbf_ref[...] = w_ref[...].astype(BF16)

    tm = a_ref.shape[0]
    sub = min(tm, SUB_ROWS)
    for c in range(tm // sub):
        rows = slice(c * sub, (c + 1) * sub)
        g = g_ref[...] if g_ref.shape[0] == 1 else g_ref[rows, :]
        o_ref[rows, :] = x_ref[rows, :] + g * jnp.dot(a_ref[rows, :], wbf_ref[...], preferred_element_type=F32)


def _resid_matmul(a, w, x2, gate, l, tm, name):
    m, k = a.shape
    d = w.shape[2]
    if gate.shape[0] == m:
        gate_spec = pl.BlockSpec((tm, TN), lambda j, i: (i, j))
    else:
        nb = gate.shape[0]
        per = (m // nb) // tm
        gate = gate.reshape(nb, 1, d)
        gate_spec = pl.BlockSpec((None, 1, TN), lambda j, i: (i // per, 0, j))
    return pl.pallas_call(
        _resid_kernel,
        grid=(d // TN, m // tm),
        in_specs=[
            pl.BlockSpec((tm, k), lambda j, i: (i, 0)),
            pl.BlockSpec((None, k, TN), lambda j, i: (l, 0, j), pipeline_mode=pl.Buffered(1)),
            pl.BlockSpec((tm, TN), lambda j, i: (i, j)),
            gate_spec,
        ],
        out_specs=pl.BlockSpec((tm, TN), lambda j, i: (i, j)),
        out_shape=jax.ShapeDtypeStruct((m, d), F32),
        scratch_shapes=[pltpu.VMEM((k, TN), BF16)],
        compiler_params=_cparams(("arbitrary", "arbitrary"), 56),
        name=name,
    )(a, w, x2, gate)


def _ffn_up_kernel(h_ref, wa_ref, wb_ref, cw_ref, cb_ref, s0_ref, s1_ref, o_ref, tail_ref,
                   wa_bf, wb_bf, abuf, *, tm, seg, per_batch, keep):
    i = pl.program_id(1)

    @pl.when(i == 0)
    def _():
        wa_bf[...] = wa_ref[...].astype(BF16)
        wb_bf[...] = wb_ref[...].astype(BF16)

    @pl.when(i % per_batch == 0)
    def _():
        abuf[0:SUBLANES, :] = jnp.zeros((SUBLANES, TN), F32)

    sub = min(tm, SUB_ROWS)
    for c in range(tm // sub):
        rows = slice(c * sub, (c + 1) * sub)
        h = h_ref[rows, :]
        a = jnp.dot(h, wa_bf[...], preferred_element_type=F32)
        b = jnp.dot(h, wb_bf[...], preferred_element_type=F32)
        abuf[SUBLANES + c * sub:SUBLANES + (c + 1) * sub, :] = a
        a1 = abuf[SUBLANES - 1 + c * sub:SUBLANES - 1 + (c + 1) * sub, :]
        a2 = abuf[SUBLANES - 2 + c * sub:SUBLANES - 2 + (c + 1) * sub, :]
        if seg < tm:
            r = lax.broadcasted_iota(jnp.int32, (sub, 1), 0) & (seg - 1)
            a1 = jnp.where(r == 0, s1_ref[rows, :], a1)
            a2 = jnp.where(r == 0, s0_ref[rows, :], jnp.where(r == 1, s1_ref[rows, :], a2))
        conv = a2 * cw_ref[0:1, :] + a1 * cw_ref[1:2, :] + a * cw_ref[2:3, :] + cb_ref[...]
        o_ref[rows, :] = (jax.nn.gelu(conv) * b).astype(BF16)
    tail_ref[...] = abuf[SUBLANES + tm - keep:SUBLANES + tm, :]
    abuf[0:SUBLANES, :] = abuf[tm:tm + SUBLANES, :]


def _ffn_up(h2, w_a, w_b, conv_w, conv_b, s0, s1, t, l, tm):
    m, d = h2.shape
    dff = w_a.shape[2]
    per_batch = max(t // tm, 1)
    seg = min(t, tm)
    keep = SUBLANES if seg == tm else tm
    if s0 is None:
        s0 = s1 = jnp.zeros((SUBLANES, dff), F32)
        s_spec = pl.BlockSpec((SUBLANES, TN), lambda j, i: (0, j))
    else:
        s_spec = pl.BlockSpec((tm, TN), lambda j, i: (i, j))
    w_spec = pl.BlockSpec((None, d, TN), lambda j, i: (l, 0, j))
    return pl.pallas_call(
        functools.partial(_ffn_up_kernel, tm=tm, seg=seg, per_batch=per_batch, keep=keep),
        grid=(dff // TN, m // tm),
        in_specs=[
            pl.BlockSpec((tm, d), lambda j, i: (i, 0)),
            w_spec, w_spec,
            pl.BlockSpec((None, CONV_W, TN), lambda j, i: (l, 0, j)),
            pl.BlockSpec((None, 1, TN), lambda j, i: (l, 0, j)),
            s_spec, s_spec,
        ],
        out_specs=[
            pl.BlockSpec((tm, TN), lambda j, i: (i, j)),
            pl.BlockSpec((None, keep, TN), lambda j, i: (i, 0, j)),
        ],
        out_shape=[jax.ShapeDtypeStruct((m, dff), BF16), jax.ShapeDtypeStruct((m // tm, keep, dff), F32)],
        scratch_shapes=[pltpu.VMEM((d, TN), BF16), pltpu.VMEM((d, TN), BF16), pltpu.VMEM((tm + SUBLANES, TN), F32)],
        compiler_params=_cparams(("arbitrary", "arbitrary"), 56),
        name="ffn_up",
    )(h2, w_a, w_b, conv_w, conv_b.reshape(conv_b.shape[0], 1, dff), s0, s1)


def _gather_pages(pt_ref, b, c, pool_ref, l, buf_ref, sem, n_pages):
    copies = []
    for p in range(n_pages):
        page = pt_ref[b, c * n_pages + p]
        copies.append(pltpu.make_async_copy(pool_ref.at[l, page], buf_ref.at[:, pl.ds(p * PAGE_SIZE, PAGE_SIZE)], sem))
    return copies


def _gather_compress_kernel(pt_ref, poolk_ref, poolv_ref, bdk_ref, bdv_ref, pek_ref, pev_ref, knw_ref, gm_ref,
                            kc_ref, vc_ref, kbuf, vbuf, tbuf, sems, *, l, n_chunks, n_steps):
    b = pl.program_id(0)
    c = pl.program_id(1)
    step = b * n_chunks + c
    slot = step % 2

    def chunk_copies(bb, cc, sl):
        return (_gather_pages(pt_ref, bb, cc, poolk_ref, l, kbuf.at[sl], sems.at[0, sl], CMP_PAGES),
                _gather_pages(pt_ref, bb, cc, poolv_ref, l, vbuf.at[sl], sems.at[1, sl], CMP_PAGES))

    def token_major(buf):
        for h in range(KV_W // LANES):
            tbuf[h] = buf[h * LANES:(h + 1) * LANES, :].T
        return (tbuf.at[0], tbuf.at[1])

    @pl.when(step == 0)
    def _():
        for cps in chunk_copies(b, c, slot):
            for cp in cps:
                cp.start()

    @pl.when(step + 1 < n_steps)
    def _():
        wrap = c + 1 == n_chunks
        for cps in chunk_copies(jnp.where(wrap, b + 1, b), jnp.where(wrap, 0, c + 1), 1 - slot):
            for cp in cps:
                cp.start()

    kcopies, vcopies = chunk_copies(b, c, slot)
    n_blk = CMP_PAGES * PAGE_SIZE // CMP_BLOCK
    for cp in kcopies:
        cp.wait()
    kc_ref[...] = _group_rms(_compress_rows(token_major(kbuf.at[slot]), n_blk, bdk_ref, pek_ref),
                             gm_ref[...], knw_ref[...])
    for cp in vcopies:
        cp.wait()
    vc_ref[...] = _compress_rows(token_major(vbuf.at[slot]), n_blk, bdv_ref, pev_ref)


def _gather_compress(page_table, pool_k, pool_v, cmp_consts, l):
    bd, n_pages = page_table.shape
    n_chunks = n_pages // CMP_PAGES
    n_blk = CMP_PAGES * PAGE_SIZE // CMP_BLOCK
    bdk, bdv, pek, pev, knw, gm = cmp_consts
    const3 = lambda b, c, pt: (0, 0, 0)
    const2 = lambda b, c, pt: (0, 0)
    grid_spec = pltpu.PrefetchScalarGridSpec(
        num_scalar_prefetch=1,
        grid=(bd, n_chunks),
        in_specs=[
            pl.BlockSpec(memory_space=pl.ANY),
            pl.BlockSpec(memory_space=pl.ANY),
            pl.BlockSpec((CMP_BLOCK, LANES, LANES), const3),
            pl.BlockSpec((CMP_BLOCK, LANES, LANES), const3),
            pl.BlockSpec((CMP_BLOCK, LANES), const2),
            pl.BlockSpec((CMP_BLOCK, LANES), const2),
            pl.BlockSpec((1, KV_W), const2),
            pl.BlockSpec((KV_W, KV_W), const2),
        ],
        out_specs=[pl.BlockSpec((None, n_blk, KV_W), lambda b, c, pt: (b, c, 0))] * 2,
        scratch_shapes=[
            pltpu.VMEM((2, KV_W, CMP_PAGES * PAGE_SIZE), F32),
            pltpu.VMEM((2, KV_W, CMP_PAGES * PAGE_SIZE), F32),
            pltpu.VMEM((KV_W // LANES, CMP_PAGES * PAGE_SIZE, LANES), F32),
            pltpu.SemaphoreType.DMA((2, 2)),
        ],
    )
    return pl.pallas_call(
        functools.partial(_gather_compress_kernel, l=l, n_chunks=n_chunks, n_steps=bd * n_chunks),
        grid_spec=grid_spec,
        out_shape=[jax.ShapeDtypeStruct((bd, n_chunks * n_blk, KV_W), F32)] * 2,
        compiler_params=_cparams(("arbitrary", "arbitrary"), 48),
        name="gather_compress",
    )(page_table, pool_k, pool_v, bdk, bdv, pek, pev, knw, gm)


def _fold_groups(x):
    rows = x.shape[0]
    grp = lax.broadcasted_iota(jnp.int32, (rows, 1), 0) >> 5
    out = jnp.zeros((rows, HEAD_DIM), F32)
    for g in range(N_KV):
        out = out + jnp.where(grp == g, x[:, g * HEAD_DIM:(g + 1) * HEAD_DIM], 0.0)
    return out


def _nt_dot(a, b):
    return lax.dot_general(a, b, (((1,), (1,)), ((), ())), preferred_element_type=F32)


def _sample_cmp_kernel(q_ref, kc_ref, vc_ref, slope_ref, pair_ref, oc_ref, sel_ref, *, past, nc, ns, ns_pad):
    rows = N_HEADS * SUBLANES
    q = (q_ref[...] * SCALE).astype(BF16)
    ncp = kc_ref.shape[0]
    s = _nt_dot(q, kc_ref[...].astype(BF16))
    qpos = past + (lax.broadcasted_iota(jnp.int32, (rows, 1), 0) & (SUBLANES - 1))
    n = lax.broadcasted_iota(jnp.int32, (1, ncp), 1)
    dist = qpos - ((n + 1) * CMP_BLOCK - 1)
    mask = (dist >= 0) & (n < nc)
    p = _softmax_rows(s - slope_ref[...] * dist.astype(F32), mask)
    oc_ref[...] = _fold_groups(jnp.dot(p.astype(BF16), vc_ref[...].astype(BF16), preferred_element_type=F32))

    parts = []
    for g in range(N_KV):
        acc = p[(g * Q_PER_KV) * SUBLANES:(g * Q_PER_KV + 1) * SUBLANES]
        for r in range(1, Q_PER_KV):
            acc = acc + p[(g * Q_PER_KV + r) * SUBLANES:(g * Q_PER_KV + r + 1) * SUBLANES]
        parts.append(acc)
    psum = jnp.concatenate(parts, axis=0)
    imp = _pair_sum(psum, pair_ref[...])
    gq = N_KV * SUBLANES
    jj = lax.broadcasted_iota(jnp.int32, (gq, ns_pad), 1)
    cur = (past + (lax.broadcasted_iota(jnp.int32, (gq, 1), 0) & (SUBLANES - 1))) >> 6
    forced = (jj == 0) | (jj == cur) | (jj == cur - 1)
    score = jnp.where(jj <= cur, imp + jnp.where(forced, FORCE, 0.0), NEG)
    sel = _top_mask(score, ns, min(TOP_N, ns))
    pieces = []
    for g in range(N_KV):
        for r in range(Q_PER_KV):
            pieces.append(sel[g * SUBLANES:(g + 1) * SUBLANES])
    sel_rows = jnp.concatenate(pieces, axis=0).astype(BF16)
    per = SEL_PAGES * PAGE_SIZE // SEL_BLOCK
    for c in range(ns_pad // per):
        sel_ref[c] = sel_rows[:, c * per:(c + 1) * per]


def _sample_cmp(qbd, kc_full, vc_full, slope_rows, past, nc, ns):
    bd, rows, _ = qbd.shape
    ncp = kc_full.shape[1]
    per = SEL_PAGES * PAGE_SIZE // SEL_BLOCK
    ns_pad = -(-ns // per) * per
    pair = _pair_matrix(ncp, ns_pad)
    return pl.pallas_call(
        functools.partial(_sample_cmp_kernel, past=past, nc=nc, ns=ns, ns_pad=ns_pad),
        grid=(bd,),
        in_specs=[
            pl.BlockSpec((None, rows, KV_W), lambda b: (b, 0, 0)),
            pl.BlockSpec((None, ncp, KV_W), lambda b: (b, 0, 0)),
            pl.BlockSpec((None, ncp, KV_W), lambda b: (b, 0, 0)),
            pl.BlockSpec((rows, 1), lambda b: (0, 0)),
            pl.BlockSpec((ncp, ns_pad), lambda b: (0, 0)),
        ],
        out_specs=[
            pl.BlockSpec((None, rows, HEAD_DIM), lambda b: (b, 0, 0)),
            pl.BlockSpec((None, ns_pad // per, rows, per), lambda b: (b, 0, 0, 0)),
        ],
        out_shape=[jax.ShapeDtypeStruct((bd, rows, HEAD_DIM), F32),
                   jax.ShapeDtypeStruct((bd, ns_pad // per, rows, per), BF16)],
        compiler_params=_cparams(("arbitrary",), 40),
        name="sample_cmp",
    )(qbd, kc_full, vc_full, slope_rows, pair)


def _sample_sel_kernel(pt_ref, poolk_ref, poolv_ref, q_ref, selc_ref, seln_ref, kn_ref, vn_ref, bk_ref, bv_ref,
                       kwn_ref, vwn_ref, oc_ref, gate_ref, slope_ref, o_ref,
                       kbuf, vbuf, sems, m_sc, l_sc, acc_sc, *, past, n_chunks, n_steps, l):
    b = pl.program_id(0)
    c = pl.program_id(1)
    rows = N_HEADS * SUBLANES
    step = b * n_chunks + c
    slot = step % 2

    def chunk_copies(bb, cc, sl):
        return (_gather_pages(pt_ref, bb, cc, poolk_ref, l, kbuf.at[sl], sems.at[0, sl], SEL_PAGES),
                _gather_pages(pt_ref, bb, cc, poolv_ref, l, vbuf.at[sl], sems.at[1, sl], SEL_PAGES))

    @pl.when(step == 0)
    def _():
        for cps in chunk_copies(b, c, slot):
            for cp in cps:
                cp.start()

    @pl.when(step + 1 < n_steps)
    def _():
        wrap = c + 1 == n_chunks
        for cps in chunk_copies(jnp.where(wrap, b + 1, b), jnp.where(wrap, 0, c + 1), 1 - slot):
            for cp in cps:
                cp.start()

    kcopies, vcopies = chunk_copies(b, c, slot)

    @pl.when(c == 0)
    def _():
        m_sc[...] = jnp.full(m_sc.shape, NEG, F32)
        l_sc[...] = jnp.zeros(l_sc.shape, F32)
        acc_sc[...] = jnp.zeros(acc_sc.shape, F32)

    q = (q_ref[...] * SCALE).astype(BF16)
    qpos = past + (lax.broadcasted_iota(jnp.int32, (rows, 1), 0) & (SUBLANES - 1))
    slope = slope_ref[...]

    def online_update(s, mask, v_bf, keys_on_lanes):
        s = jnp.where(mask, s, NEG)
        m_old = m_sc[...]
        m_new = jnp.maximum(m_old, jnp.max(s, axis=-1, keepdims=True))
        alpha = jnp.exp(m_old - m_new)
        p = jnp.where(mask, jnp.exp(s - m_new), 0.0)
        p_bf = p.astype(BF16)
        pv = _nt_dot(p_bf, v_bf) if keys_on_lanes else jnp.dot(p_bf, v_bf, preferred_element_type=F32)
        l_sc[...] = alpha * l_sc[...] + jnp.sum(p, axis=-1, keepdims=True)
        acc_sc[...] = alpha * acc_sc[...] + pv
        m_sc[...] = m_new

    def sel_scores(k_bf, pos0, sel_blocks, keys_on_lanes):
        nk = k_bf.shape[1] if keys_on_lanes else k_bf.shape[0]
        nb = sel_blocks.shape[1]
        kidx = lax.broadcasted_iota(jnp.int32, (1, nk), 1)
        blk = lax.broadcasted_iota(jnp.int32, (nb, nk), 0)
        expand = jnp.where((kidx >> 6) == blk, 1.0, 0.0).astype(BF16)
        chosen = jnp.dot(sel_blocks, expand, preferred_element_type=F32)
        dist = qpos - (pos0 + kidx)
        mask = (dist >= 0) & (chosen > 0.5)
        qk = jnp.dot(q, k_bf, preferred_element_type=F32) if keys_on_lanes else _nt_dot(q, k_bf)
        return qk - slope * dist.astype(F32), mask

    for cp in kcopies:
        cp.wait()
    s, mask = sel_scores(kbuf[slot].astype(BF16), c * (SEL_PAGES * PAGE_SIZE), selc_ref[...], True)
    for cp in vcopies:
        cp.wait()
    online_update(s, mask, vbuf[slot].astype(BF16), True)

    @pl.when(c == n_chunks - 1)
    def _():
        s2, mask2 = sel_scores(kn_ref[...].astype(BF16), past, seln_ref[:, 0:1], False)
        online_update(s2, mask2, vn_ref[...].astype(BF16), False)
        o_s = _fold_groups(acc_sc[...] / jnp.maximum(l_sc[...], 1e-30))

        wb = bk_ref.shape[0]
        kw = jnp.concatenate([bk_ref[...], kwn_ref[...]], axis=0).astype(BF16)
        vw = jnp.concatenate([bv_ref[...], vwn_ref[...]], axis=0).astype(BF16)
        nkw = kw.shape[0]
        kidx = lax.broadcasted_iota(jnp.int32, (1, nkw), 1)
        kpos = past - wb + kidx
        dist = qpos - kpos
        mask_w = (dist >= 0) & (dist < WINDOW) & (kpos >= 0)
        pw = _softmax_rows(_nt_dot(q, kw) - slope * dist.astype(F32), mask_w)
        o_w = _fold_groups(jnp.dot(pw.astype(BF16), vw, preferred_element_type=F32))
        gate = gate_ref[...]
        o_ref[...] = gate[:, 0:1] * oc_ref[...] + gate[:, 1:2] * o_s + gate[:, 2:3] * o_w


def _sample_sel(page_table, pool_k, pool_v, qbd, sel_chunks, k_new, v_new, buf_k, buf_v, kw_new, vw_new,
                o_c, gate_rows, slope_rows, past, l):
    bd, n_pages = page_table.shape
    n_chunks = n_pages // SEL_PAGES
    rows = qbd.shape[1]
    per = sel_chunks.shape[3]
    wb = buf_k.shape[2]
    npad = kw_new.shape[1]
    buf_spec = pl.BlockSpec((None, None, wb, KV_W), lambda b, c, pt: (l, b, 0, 0))
    row3 = lambda n, w: pl.BlockSpec((None, n, w), lambda b, c, pt: (b, 0, 0))
    grid_spec = pltpu.PrefetchScalarGridSpec(
        num_scalar_prefetch=1,
        grid=(bd, n_chunks),
        in_specs=[
            pl.BlockSpec(memory_space=pl.ANY),
            pl.BlockSpec(memory_space=pl.ANY),
            row3(rows, KV_W),
            pl.BlockSpec((None, None, rows, per), lambda b, c, pt: (b, c, 0, 0)),
            pl.BlockSpec((None, None, rows, per), lambda b, c, pt: (b, n_chunks, 0, 0)),
            row3(SEL_BLOCK, KV_W), row3(SEL_BLOCK, KV_W),
            buf_spec, buf_spec,
            row3(npad, KV_W), row3(npad, KV_W),
            row3(rows, HEAD_DIM),
            row3(rows, 3),
            pl.BlockSpec((rows, 1), lambda b, c, pt: (0, 0)),
        ],
        out_specs=row3(rows, HEAD_DIM),
        scratch_shapes=[
            pltpu.VMEM((2, KV_W, SEL_PAGES * PAGE_SIZE), F32),
            pltpu.VMEM((2, KV_W, SEL_PAGES * PAGE_SIZE), F32),
            pltpu.SemaphoreType.DMA((2, 2)),
            pltpu.VMEM((rows, 1), F32), pltpu.VMEM((rows, 1), F32), pltpu.VMEM((rows, KV_W), F32),
        ],
    )
    return pl.pallas_call(
        functools.partial(_sample_sel_kernel, past=past, n_chunks=n_chunks, n_steps=bd * n_chunks, l=l),
        grid_spec=grid_spec,
        out_shape=jax.ShapeDtypeStruct((bd, rows, HEAD_DIM), F32),
        compiler_params=_cparams(("arbitrary", "arbitrary"), 56),
        name="sample_sel",
    )(page_table, pool_k, pool_v, qbd, sel_chunks, sel_chunks, k_new, v_new, buf_k, buf_v, kw_new, vw_new,
      o_c, gate_rows, slope_rows)


def _alibi_slopes():
    h = jnp.arange(1, N_HEADS + 1, dtype=F32)
    return (2.0 ** (-8.0 * h / N_HEADS)).reshape(N_KV, Q_PER_KV)


def _pad_rows(x, n):
    return jnp.pad(x, ((0, 0), (0, n - x.shape[1]), (0, 0)))


def _layer(l, x, mods, p, sample):
    b, t, d = x.shape
    m = b * t
    x2 = x.reshape(m, d)
    tm = min(t, 1024) if sample is None else m
    slopes = _alibi_slopes()

    h1 = _modulate(x2, p['norm1_w'], mods[0], mods[1], t, l)
    proj = _inproj(h1, p['w_in_t'], p['qn'][l], p['kn'][l], l, tm)
    proj3 = proj.reshape(b, t, proj.shape[1])

    def kv_slice(idx):
        return proj3[:, :, COL_KV + idx * KV_W:COL_KV + (idx + 1) * KV_W]

    k_cmp, v_cmp, k_sel, v_sel, k_win, v_win = [kv_slice(i) for i in range(6)]
    nsa_col = proj.shape[1] - TN
    ng = proj3[:, :, nsa_col:nsa_col + 3 * N_HEADS]
    cmp_consts = p['cmp'][l]

    if sample is None:
        o_a, _ = _chunkmix(proj, p['a_spatial_w'][l], p['a_spatial_b'][l].T, p['a_norm_w'][l][None, :], CHUNK, CHUNK)
        kc, vc = _compress(proj3, COL_KV // KV_W, COL_KV // KV_W + 1, cmp_consts)
        nc = t // CMP_BLOCK

        def vt(xv, n):
            return xv.reshape(b, n, N_KV, HEAD_DIM).transpose(0, 2, 3, 1).astype(BF16)

        def evens_first(xc):
            return xc.reshape(b, nc // 2, 2, KV_W).transpose(0, 2, 1, 3).reshape(b, nc, KV_W)

        kv_bf, vs_t, vw_t = _kv_prep(proj, b, t, tm)
        o_b = _nsa_prompt(slopes, proj3, evens_first(kc).astype(BF16), vt(evens_first(vc), nc),
                          kv_bf.reshape(b, t, 6 * KV_W), vs_t, vw_t).reshape(m, B_WIDTH)
        v_norm = None
        wp = min(WINDOW, t)
        new_bk, new_bv = k_win[:, -wp:], v_win[:, -wp:]
    else:
        ds = t
        eye = jnp.eye(b, dtype=F32)
        w8 = p['a_spatial_w'][l][:, :ds, :ds]
        w_bd = jnp.einsum('ab,gts->gatbs', eye, w8).reshape(A_GROUPS, m, m)
        b_t = jnp.tile(p['a_spatial_b'][l][:, :ds].T, (b, 1))
        o_a, v_norm = _chunkmix(proj, w_bd, b_t, p['a_norm_w'][l][None, :], m, ds)

        page_table = sample['page_table']
        n_pages = page_table.shape[1]
        past = n_pages * PAGE_SIZE
        new_pad = -(-ds // SEL_BLOCK) * SEL_BLOCK
        nc = (past + new_pad) // CMP_BLOCK
        ns = (past + new_pad) // SEL_BLOCK
        kc_past, vc_past = _gather_compress(page_table, sample['cache_k_cmp'], sample['cache_v_cmp'], cmp_consts, l)
        new_rows = jnp.concatenate([_pad_rows(k_cmp, new_pad), _pad_rows(v_cmp, new_pad)], axis=2)
        kc_new, vc_new = _compress(new_rows, 0, 1, cmp_consts)
        ncp = -(-nc // LANES) * LANES
        kc_full = _pad_rows(jnp.concatenate([kc_past, kc_new], axis=1), ncp)
        vc_full = _pad_rows(jnp.concatenate([vc_past, vc_new], axis=1), ncp)

        q5 = proj3[:, :, COL_Q:COL_Q + B_WIDTH].reshape(b, ds, N_KV, Q_PER_KV, HEAD_DIM).transpose(0, 2, 3, 1, 4)
        qbd = jnp.einsum('bgrqd,gh->bgrqhd', q5, jnp.eye(N_KV, dtype=F32)).reshape(b, N_HEADS * ds, KV_W)
        slope_rows = jnp.repeat(slopes.reshape(N_HEADS), ds)[:, None]
        gate_rows = ng.reshape(b, ds, 3, N_HEADS).transpose(0, 3, 1, 2).reshape(b, N_HEADS * ds, 3)
        o_c, sel_chunks = _sample_cmp(qbd, kc_full, vc_full, slope_rows, past, nc, ns)
        o_rows = _sample_sel(page_table, sample['cache_k_sel'], sample['cache_v_sel'], qbd, sel_chunks,
                             _pad_rows(k_sel, new_pad), _pad_rows(v_sel, new_pad), sample['cache_k_win'],
                             sample['cache_v_win'], _pad_rows(k_win, LANES), _pad_rows(v_win, LANES), o_c,
                             gate_rows, slope_rows, past, l)
        buf_k, buf_v = sample['cache_k_win'][l], sample['cache_v_win'][l]
        o_b = o_rows.reshape(b, N_HEADS, ds, HEAD_DIM).transpose(0, 2, 1, 3).reshape(m, B_WIDTH).astype(BF16)
        keep = min(WINDOW, buf_k.shape[1] + ds)
        new_bk = jnp.concatenate([buf_k, k_win], axis=1)[:, -keep:]
        new_bv = jnp.concatenate([buf_v, v_win], axis=1)[:, -keep:]

    def gate_of(vec):
        return vec if sample is None else jnp.repeat(vec, t, axis=0)

    mix = _branch(o_a, o_b, p['w_branch'], proj, l, tm)
    x_mid = _resid_matmul(mix, p['w_out'], x2, gate_of(mods[2]), l, tm, 'attn_out')
    h2 = _modulate(x_mid, p['norm2_w'], mods[3], mods[4], t, l)
    if sample is None:
        s0 = s1 = None
    else:
        state = sample['state_ffn_conv'][l]
        s0 = jnp.repeat(state[:, 0], t, axis=0)
        s1 = jnp.repeat(state[:, 1], t, axis=0)
    hid, tails = _ffn_up(h2, p['ffn_w_a'], p['ffn_w_b'], p['ffn_conv_w'], p['ffn_conv_b'], s0, s1, t, l, tm)
    x_out = _resid_matmul(hid, p['ffn_w_down'], x_mid, gate_of(mods[5]), l, tm, 'ffn_down')

    if sample is None:
        conv_rows = tails.reshape(b, t // tm, SUBLANES, -1)[:, -1, SUBLANES - (CONV_W - 1):]
    else:
        conv_rows = tails.reshape(b, t, -1)[:, t - (CONV_W - 1):]
    heads = lambda z: z.reshape(b, -1, N_KV, HEAD_DIM)
    state_out = dict(k_cmp=heads(k_cmp), v_cmp=heads(v_cmp), k_sel=heads(k_sel), v_sel=heads(v_sel),
                     k_win=heads(new_bk), v_win=heads(new_bv), conv=conv_rows)
    if sample is not None:
        state_out['chunk_v'] = v_norm.reshape(b, t, A_WIDTH)
    return x_out.reshape(b, t, d), state_out


def kernel(x_prompt, x_sample, cache_k_cmp, cache_v_cmp, cache_k_sel, cache_v_sel, cache_k_win, cache_v_win,
           state_ffn_conv, page_table, c_prompt, c_sample, w_ada, b_ada, norm1_w, norm2_w, w_in, a_norm_w,
           a_spatial_w, a_spatial_b, q_norm_w, k_norm_w, cmp_pe_k, cmp_w_k, cmp_pe_v, cmp_w_v, w_branch, w_out,
           ffn_w_a, ffn_w_b, ffn_conv_w, ffn_conv_b, ffn_w_down):
    depth, d, n_in = w_in.shape
    bp = x_prompt.shape[0]
    bs = x_sample.shape[0]

    rows = -(-(bp + bs) // SUBLANES) * SUBLANES
    c_all = jnp.pad(jnp.concatenate([c_prompt, c_sample], axis=0), ((0, rows - bp - bs), (0, 0)))
    mods_all = _ada(c_all, w_ada, b_ada)

    w_in_t = jnp.swapaxes(w_in, 1, 2)
    shared = dict(
        norm1_w=norm1_w, norm2_w=norm2_w, w_in_t=w_in_t,
        qn=jnp.tile(q_norm_w, (1, TN // HEAD_DIM))[:, None, :], kn=jnp.tile(k_norm_w, (1, TN // HEAD_DIM))[:, None, :],
        a_norm_w=a_norm_w, a_spatial_w=a_spatial_w, a_spatial_b=a_spatial_b,
        cmp=[_cmp_consts(cmp_pe_k[l], cmp_w_k[l], cmp_pe_v[l], cmp_w_v[l], k_norm_w[l]) for l in range(depth)],
        w_branch=w_branch, w_out=w_out, ffn_w_a=ffn_w_a, ffn_w_b=ffn_w_b, ffn_conv_w=ffn_conv_w,
        ffn_conv_b=ffn_conv_b, ffn_w_down=ffn_w_down)
    sample = dict(page_table=page_table, cache_k_cmp=cache_k_cmp, cache_v_cmp=cache_v_cmp, cache_k_sel=cache_k_sel,
                  cache_v_sel=cache_v_sel, cache_k_win=cache_k_win, cache_v_win=cache_v_win,
                  state_ffn_conv=state_ffn_conv)
    for name in ('cache_k_cmp', 'cache_v_cmp', 'cache_k_sel', 'cache_v_sel'):
        pool = sample[name]
        sample[name] = pool.transpose(0, 1, 3, 4, 2).reshape(pool.shape[0], pool.shape[1], KV_W, pool.shape[2])
    for name in ('cache_k_win', 'cache_v_win'):
        buf = sample[name]
        sample[name] = buf.reshape(buf.shape[0], buf.shape[1], buf.shape[2], KV_W)

    xp, xs = x_prompt, x_sample
    st_p, st_s = [], []
    for l in range(depth):
        mods = mods_all[l]
        mp = [mods[:bp, k * d:(k + 1) * d] for k in range(6)]
        ms = [mods[bp:bp + bs, k * d:(k + 1) * d] for k in range(6)]
        xp, sp = _layer(l, xp, mp, shared, None)
        xs, ss = _layer(l, xs, ms, shared, sample)
        st_p.append(sp)
        st_s.append(ss)

    def stack(sts, key):
        return jnp.stack([s[key] for s in sts], axis=0)

    return (xp, xs,
            stack(st_p, 'k_cmp'), stack(st_p, 'v_cmp'), stack(st_p, 'k_sel'), stack(st_p, 'v_sel'),
            stack(st_p, 'k_win'), stack(st_p, 'v_win'), stack(st_p, 'conv'),
            stack(st_s, 'k_cmp'), stack(st_s, 'v_cmp'), stack(st_s, 'k_sel'), stack(st_s, 'v_sel'),
            stack(st_s, 'k_win'), stack(st_s, 'v_win'), stack(st_s, 'chunk_v'), stack(st_s, 'conv'))
```

```python
import functools

import jax
import jax.numpy as jnp
from jax import lax
from jax.experimental import pallas as pl
from jax.experimental.pallas import tpu as pltpu

F32 = jnp.float32
BF16 = jnp.bfloat16

HEAD_DIM = 64
N_KV = 4
Q_PER_KV = 4
N_HEADS = N_KV * Q_PER_KV
KV_W = N_KV * HEAD_DIM
A_GROUPS = 8
A_GROUP_DIM = 128
A_WIDTH = A_GROUPS * A_GROUP_DIM
B_WIDTH = N_HEADS * HEAD_DIM
CHUNK = 128
CMP_BLOCK = 32
SEL_BLOCK = 64
TOP_N = 16
WINDOW = 512
Q_BLOCK = 128
PAGE_SIZE = 128
CONV_W = 3
SCALE = HEAD_DIM ** -0.5
EPS = 1e-6
NEG = -1e30
FORCE = 1e4
LOG2E = 1.4426950408889634

LANES = 128
SUBLANES = 8
MIB = 1024 * 1024

TN = 512
COL_U = 0
COL_V = A_WIDTH
COL_Q = 2 * A_WIDTH
COL_KV = COL_Q + B_WIDTH
N_MAIN = COL_KV + 6 * KV_W
N_MAIN_BLOCKS = N_MAIN // TN
SEL_PAGES = 32
CMP_PAGES = 32
SUB_ROWS = 256
FFN_SUB_ROWS = 128


def _cparams(sem, vmem_mib):
    return pltpu.CompilerParams(dimension_semantics=sem, vmem_limit_bytes=vmem_mib * MIB)


def _group_rms(x, gmat, w):
    y = x * x
    hi = y.astype(BF16)
    lo = (y - hi.astype(F32)).astype(BF16)
    ms = jnp.dot(hi, gmat, preferred_element_type=F32) + jnp.dot(lo, gmat, preferred_element_type=F32)
    return x * lax.rsqrt(ms + EPS) * w


def _group_mean_matrix(n):
    r = jnp.arange(n)
    return jnp.where((r[:, None] // HEAD_DIM) == (r[None, :] // HEAD_DIM), 1.0 / HEAD_DIM, 0.0).astype(BF16)


def _ada_kernel(c_ref, w_ref, b_ref, o_ref):
    c = c_ref[...]
    s = (c * jax.nn.sigmoid(c)).astype(BF16)
    o_ref[...] = jnp.dot(s, w_ref[...].astype(BF16), preferred_element_type=F32) + b_ref[...]


def _ada(c_all, w_ada, b_ada):
    depth, d, n6 = w_ada.shape
    rows = c_all.shape[0]
    tn = 1024
    return pl.pallas_call(
        _ada_kernel,
        grid=(depth, n6 // tn),
        in_specs=[
            pl.BlockSpec((rows, d), lambda l, j: (0, 0)),
            pl.BlockSpec((None, d, tn), lambda l, j: (l, 0, j)),
            pl.BlockSpec((None, 1, tn), lambda l, j: (l, 0, j)),
        ],
        out_specs=pl.BlockSpec((None, rows, tn), lambda l, j: (l, 0, j)),
        out_shape=jax.ShapeDtypeStruct((depth, rows, n6), F32),
        compiler_params=_cparams(("arbitrary", "arbitrary"), 40),
        name="ada",
    )(c_all, w_ada, b_ada.reshape(depth, 1, n6))


def _modulate_kernel(x_ref, g_ref, sh_ref, sc_ref, o_ref):
    x = x_ref[...]
    y = x * lax.rsqrt(jnp.mean(x * x, axis=-1, keepdims=True) + EPS)
    y = y * g_ref[...]
    o_ref[...] = (y * (1.0 + sc_ref[...]) + sh_ref[...]).astype(BF16)


def _modulate(x2, g, shift, scale, t, l):
    m, d = x2.shape
    nb = m // t
    tm = min(t, 512)
    per = t // tm
    return pl.pallas_call(
        _modulate_kernel,
        grid=(m // tm,),
        in_specs=[
            pl.BlockSpec((tm, d), lambda i: (i, 0)),
            pl.BlockSpec((None, 1, d), lambda i: (l, 0, 0)),
            pl.BlockSpec((None, 1, d), lambda i: (i // per, 0, 0)),
            pl.BlockSpec((None, 1, d), lambda i: (i // per, 0, 0)),
        ],
        out_specs=pl.BlockSpec((tm, d), lambda i: (i, 0)),
        out_shape=jax.ShapeDtypeStruct((m, d), BF16),
        compiler_params=_cparams(("arbitrary",), 32),
        name="modulate",
    )(x2, g.reshape(g.shape[0], 1, d), shift.reshape(nb, 1, d), scale.reshape(nb, 1, d))


def _inproj_kernel(h_ref, w1_ref, w2_ref, qn_ref, kn_ref, gm_ref, o_ref, wbf_ref):
    j = pl.program_id(0)
    i = pl.program_id(1)

    @pl.when((i == 0) & (j < N_MAIN_BLOCKS))
    def _():
        wbf_ref[...] = w1_ref[...].T.astype(BF16)

    @pl.when((i == 0) & (j >= N_MAIN_BLOCKS))
    def _():
        wbf_ref[...] = w2_ref[0].T.astype(BF16)

    tm = h_ref.shape[0]
    sub = min(tm, SUB_ROWS)

    def project(epilogue):
        for c in range(tm // sub):
            rows = slice(c * sub, (c + 1) * sub)
            o_ref[rows, :] = epilogue(jnp.dot(h_ref[rows, :], wbf_ref[...], preferred_element_type=F32))

    @pl.when(j < 4)
    def _():
        project(jax.nn.gelu)

    @pl.when((j == 4) | (j == 5))
    def _():
        project(lambda acc: _group_rms(acc, gm_ref[...], qn_ref[...]))

    @pl.when(j == 6)
    def _():
        project(lambda acc: acc)

    @pl.when((j == 7) | (j == 8))
    def _():
        lane = lax.broadcasted_iota(jnp.int32, (sub, TN), 1)
        project(lambda acc: jnp.where(lane < KV_W, _group_rms(acc, gm_ref[...], kn_ref[...]), acc))

    @pl.when(j >= N_MAIN_BLOCKS)
    def _():
        project(jax.nn.sigmoid)


def _inproj(h, w_in_t, qn, kn, l, tm):
    m, d = h.shape
    n_in = w_in_t.shape[1]
    n_merge = n_in - N_MAIN - 3 * N_HEADS
    n_gm = n_merge // TN + 1
    nj = N_MAIN_BLOCKS + n_gm
    gm = _group_mean_matrix(TN)

    def gate_rows(j, i):
        k = jnp.clip(j - N_MAIN_BLOCKS, 0, n_gm - 1)
        return (l, pl.multiple_of(jnp.where(k == n_gm - 1, N_MAIN, N_MAIN + 3 * N_HEADS + k * TN), SUBLANES), 0)

    return pl.pallas_call(
        _inproj_kernel,
        grid=(nj, m // tm),
        in_specs=[
            pl.BlockSpec((tm, d), lambda j, i: (i, 0)),
            pl.BlockSpec((None, TN, d), lambda j, i: (l, jnp.minimum(j, N_MAIN_BLOCKS - 1), 0)),
            pl.BlockSpec((pl.Element(1), pl.Element(TN), pl.Element(d)), gate_rows),
            pl.BlockSpec((1, TN), lambda j, i: (0, 0)),
            pl.BlockSpec((1, TN), lambda j, i: (0, 0)),
            pl.BlockSpec((TN, TN), lambda j, i: (0, 0)),
        ],
        out_specs=pl.BlockSpec((tm, TN), lambda j, i: (i, j)),
        out_shape=jax.ShapeDtypeStruct((m, nj * TN), F32),
        scratch_shapes=[pltpu.VMEM((d, TN), BF16)],
        compiler_params=_cparams(("arbitrary", "arbitrary"), 48),
        name="inproj",
    )(h, w_in_t, w_in_t, qn, kn, gm)


def _kv_prep_kernel(x_ref, kv_ref, vst_ref, vwt_ref):
    x = x_ref[...]
    kv_ref[...] = x.astype(BF16)
    vst_ref[...] = x[:, 3 * KV_W:4 * KV_W].T.astype(BF16)
    vwt_ref[...] = x[:, 5 * KV_W:6 * KV_W].T.astype(BF16)


def _kv_prep(proj, b, t, tm):
    m = proj.shape[0]
    width = 6 * KV_W
    per = t // tm
    vt_spec = pl.BlockSpec((None, KV_W, tm), lambda i: (i // per, 0, i % per))
    return pl.pallas_call(
        _kv_prep_kernel,
        grid=(m // tm,),
        in_specs=[pl.BlockSpec((tm, width), lambda i: (i, COL_KV // width))],
        out_specs=[pl.BlockSpec((tm, width), lambda i: (i, 0)), vt_spec, vt_spec],
        out_shape=[jax.ShapeDtypeStruct((m, width), BF16), jax.ShapeDtypeStruct((b, KV_W, t), BF16),
                   jax.ShapeDtypeStruct((b, KV_W, t), BF16)],
        compiler_params=_cparams(("arbitrary",), 40),
        name="kv_prep",
    )(proj)


def _chunkmix_kernel(u_ref, gv_ref, w_ref, bt_ref, an_ref, oa_ref, *v_refs, lc, causal_block):
    gv = gv_ref[...]
    v = gv * lax.rsqrt(jnp.mean(gv * gv, axis=-1, keepdims=True) + EPS) * an_ref[...]
    for v_ref in v_refs:
        v_ref[...] = v
    row = lax.broadcasted_iota(jnp.int32, (lc, lc), 0)
    col = lax.broadcasted_iota(jnp.int32, (lc, lc), 1)
    shift = causal_block.bit_length() - 1
    mask = (col <= row) & ((row >> shift) == (col >> shift))
    vb = v.astype(BF16)
    for g in range(A_GROUPS):
        sl = slice(g * A_GROUP_DIM, (g + 1) * A_GROUP_DIM)
        wg = jnp.where(mask, w_ref[g], 0.0).astype(BF16)
        s = jnp.dot(wg, vb[:, sl], preferred_element_type=F32) + bt_ref[:, g:g + 1]
        oa_ref[:, sl] = (u_ref[:, sl] * s).astype(BF16)


def _chunkmix(proj, w_s, b_t, a_norm, lc, causal_block, keep_v):
    m = proj.shape[0]
    n_out = 2 if keep_v else 1
    return pl.pallas_call(
        functools.partial(_chunkmix_kernel, lc=lc, causal_block=causal_block),
        grid=(m // lc,),
        in_specs=[
            pl.BlockSpec((lc, A_WIDTH), lambda c: (c, 0)),
            pl.BlockSpec((lc, A_WIDTH), lambda c: (c, 1)),
            pl.BlockSpec((A_GROUPS, lc, lc), lambda c: (0, 0, 0)),
            pl.BlockSpec((lc, A_GROUPS), lambda c: (0, 0)),
            pl.BlockSpec((1, A_WIDTH), lambda c: (0, 0)),
        ],
        out_specs=[pl.BlockSpec((lc, A_WIDTH), lambda c: (c, 0))] * n_out,
        out_shape=[jax.ShapeDtypeStruct((m, A_WIDTH), BF16), jax.ShapeDtypeStruct((m, A_WIDTH), F32)][:n_out],
        compiler_params=_cparams(("arbitrary",), 32),
        name="chunkmix",
    )(proj, proj, w_s, b_t, a_norm)


def _compress_rows(half_refs, n_blk, bd_ref, pe_ref):
    halves = []
    for ref in half_refs:
        acc = jnp.zeros((n_blk, LANES), F32)
        for i in range(CMP_BLOCK):
            rows = ref[pl.ds(i, n_blk, stride=CMP_BLOCK), :] + pe_ref[i:i + 1, :]
            acc = acc + jnp.dot(rows.astype(BF16), bd_ref[i], preferred_element_type=F32)
        halves.append(acc)
    return jnp.concatenate(halves, axis=1)


def _compress_kernel(k0_ref, k1_ref, v0_ref, v1_ref, bdk_ref, bdv_ref, pek_ref, pev_ref, knw_ref, gm_ref,
                     kc_ref, vc_ref, *, nc):
    kc_ref[...] = _group_rms(_compress_rows((k0_ref, k1_ref), nc, bdk_ref, pek_ref), gm_ref[...], knw_ref[...])
    vc_ref[...] = _compress_rows((v0_ref, v1_ref), nc, bdv_ref, pev_ref)


def _compress(src3, kblk, vblk, cmp_consts):
    b, t, _ = src3.shape
    nc = t // CMP_BLOCK
    bdk, bdv, pek, pev, knw, gm = cmp_consts
    const3 = lambda i: (0, 0, 0)
    const2 = lambda i: (0, 0)
    half = lambda blk, h: pl.BlockSpec((None, t, LANES), lambda i: (i, 0, 2 * blk + h))
    return pl.pallas_call(
        functools.partial(_compress_kernel, nc=nc),
        grid=(b,),
        in_specs=[
            half(kblk, 0), half(kblk, 1), half(vblk, 0), half(vblk, 1),
            pl.BlockSpec((CMP_BLOCK, LANES, LANES), const3),
            pl.BlockSpec((CMP_BLOCK, LANES, LANES), const3),
            pl.BlockSpec((CMP_BLOCK, LANES), const2),
            pl.BlockSpec((CMP_BLOCK, LANES), const2),
            pl.BlockSpec((1, KV_W), const2),
            pl.BlockSpec((KV_W, KV_W), const2),
        ],
        out_specs=[pl.BlockSpec((None, nc, KV_W), lambda i: (i, 0, 0))] * 2,
        out_shape=[jax.ShapeDtypeStruct((b, nc, KV_W), F32)] * 2,
        compiler_params=_cparams(("arbitrary",), 40),
        name="compress",
    )(src3, src3, src3, src3, bdk, bdv, pek, pev, knw, gm)


def _cmp_consts(cmp_pe_k, cmp_w_k, cmp_pe_v, cmp_w_v, k_norm_w):
    per = LANES // HEAD_DIM
    eye = jnp.eye(per, dtype=F32)

    def bd(w):
        return jnp.einsum('gh,ide->igdhe', eye, w).reshape(CMP_BLOCK, LANES, LANES).astype(BF16)

    return (bd(cmp_w_k), bd(cmp_w_v), jnp.tile(cmp_pe_k, (1, per)), jnp.tile(cmp_pe_v, (1, per)),
            jnp.tile(k_norm_w, N_KV)[None, :], _group_mean_matrix(KV_W))


def _pair_sum(psum, pair):
    hi = psum.astype(BF16)
    lo = (psum - hi.astype(F32)).astype(BF16)
    return jnp.dot(hi, pair, preferred_element_type=F32) + jnp.dot(lo, pair, preferred_element_type=F32)


def _top_mask(score, n_valid, top_n):
    rows, width = score.shape
    jj = lax.broadcasted_iota(jnp.int32, (rows, width), 1)
    rank = jnp.zeros((rows, width), F32)
    for k in range(n_valid):
        ck = score[:, k:k + 1]
        beats = (ck > score) | ((ck == score) & (jj > k))
        rank = rank + jnp.where(beats, 1.0, 0.0)
    return jnp.where((rank < top_n) & (jj < n_valid), 1.0, 0.0)


def _softmax_rows(s, mask):
    s = jnp.where(mask, s, NEG)
    m = jnp.max(s, axis=-1, keepdims=True)
    p = jnp.where(mask, jnp.exp(s - m), 0.0)
    return p / jnp.maximum(jnp.sum(p, axis=-1, keepdims=True), 1e-30)


def _softmax2_cols(s2, mask):
    s2 = jnp.where(mask, s2, NEG)
    m = jnp.max(s2, axis=0, keepdims=True)
    p = jnp.where(mask, jnp.exp2(s2 - m), 0.0)
    return p / jnp.maximum(jnp.sum(p, axis=0, keepdims=True), 1e-30)


def _nsa_prompt_kernel(slopes2_ref, q_ref, ng_ref, kc_ref, vct_ref, ks_ref, vst_ref, kw_ref, vwt_ref, bias_ref,
                       o_ref, qbd_sc, sel_sc, m_sc, l_sc, acc_sc, *, t, tk):
    i = pl.program_id(1)
    nc = t // CMP_BLOCK
    ns = t // SEL_BLOCK
    half = nc // 2
    start = i * Q_BLOCK
    groups = range(N_KV)
    lane_blk = [slice(r * Q_BLOCK, (r + 1) * Q_BLOCK) for r in range(Q_PER_KV)]
    slopes2 = [[slopes2_ref[g, r] for r in range(Q_PER_KV)] for g in groups]
    qpos = start + lax.broadcasted_iota(jnp.int32, (1, Q_BLOCK), 1)

    qbd_sc[...] = jnp.zeros(qbd_sc.shape, BF16)
    qt = (q_ref[...].T * (SCALE * LOG2E)).astype(BF16)
    for g in groups:
        for r in range(Q_PER_KV):
            h = g * Q_PER_KV + r
            qbd_sc[g, g * HEAD_DIM:(g + 1) * HEAD_DIM, lane_blk[r]] = qt[h * HEAD_DIM:(h + 1) * HEAD_DIM, :]
    ngt = ng_ref[...].T

    def gate(branch, g):
        r0 = branch * N_HEADS + g * Q_PER_KV
        return jnp.concatenate([ngt[r0 + r:r0 + r + 1, :] for r in range(Q_PER_KV)], axis=1)

    row = lax.broadcasted_iota(jnp.int32, (nc, 1), 0)
    blk_c = jnp.where(row < half, 2 * row, 2 * (row - half) + 1)
    dist_c = qpos - ((blk_c + 1) * CMP_BLOCK - 1)
    mask_c = dist_c >= 0
    dist_cf = dist_c.astype(F32)
    jj = lax.broadcasted_iota(jnp.int32, (ns, 1), 0)
    cur = qpos >> 6
    forced = (jj == 0) | (jj == cur) | (jj == cur - 1)
    kc = kc_ref[...]
    o_c = []
    for g in groups:
        st = jnp.dot(kc, qbd_sc[g], preferred_element_type=F32)
        p_heads = [_softmax2_cols(st[:, lane_blk[r]] - slopes2[g][r] * dist_cf, mask_c) for r in range(Q_PER_KV)]
        o_c.append(jnp.dot(vct_ref[g], jnp.concatenate(p_heads, axis=1).astype(BF16), preferred_element_type=F32))
        psum = ((p_heads[0] + p_heads[1]) + p_heads[2]) + p_heads[3]
        imp = psum[:half] + psum[half:]
        score = jnp.where(jj <= cur, imp + jnp.where(forced, FORCE, 0.0), NEG)
        rank = jnp.zeros((ns, Q_BLOCK), F32)
        for k in range(ns):
            sk = score[k:k + 1, :]
            lo = (k // SUBLANES) * SUBLANES
            hi = lo + SUBLANES
            one = lambda cond: jnp.where(cond, 1.0, 0.0)
            rank = rank + jnp.concatenate(
                ([one(sk > score[:lo])] if lo else [])
                + [jnp.where(jj[lo:hi] > k, one(sk >= score[lo:hi]), one(sk > score[lo:hi]))]
                + ([one(sk >= score[hi:])] if hi < ns else []), axis=0)
        sel_sc[g] = jnp.where(rank < min(TOP_N, ns), 0.0, NEG)

    m_sc[...] = jnp.full(m_sc.shape, NEG, F32)
    l_sc[...] = jnp.zeros(l_sc.shape, F32)
    acc_sc[...] = jnp.zeros(acc_sc.shape, F32)
    blocks_per_tile = tk // SEL_BLOCK
    last = start // tk

    def sel_tile(ti, causal):
        k0 = pl.multiple_of(ti * tk, tk)
        kt = ks_ref[pl.ds(k0, tk), :]
        off = start - k0
        offf = off.astype(F32)
        if causal:
            rel = lax.broadcasted_iota(jnp.int32, (tk, Q_BLOCK), 1) - lax.broadcasted_iota(jnp.int32, (tk, Q_BLOCK), 0)
            future = rel + off < 0
        for g in groups:
            s = jnp.dot(kt, qbd_sc[g], preferred_element_type=F32)
            sel_rows = sel_sc[g, pl.ds(pl.multiple_of(ti * blocks_per_tile, blocks_per_tile), blocks_per_tile), :]
            mask_bias = jnp.concatenate(
                [jnp.broadcast_to(sel_rows[jb:jb + 1, :], (SEL_BLOCK, Q_BLOCK)) for jb in range(blocks_per_tile)],
                axis=0)
            if causal:
                mask_bias = jnp.where(future, NEG, mask_bias)
            m_old = m_sc[g]
            p_parts, m_parts, sum_parts = [], [], []
            for r in range(Q_PER_KV):
                c_r = slopes2[g][r] * offf
                sr = (s[:, lane_blk[r]] - bias_ref[g, r, 0:tk, :]) + mask_bias
                m_new = jnp.maximum(m_old[:, lane_blk[r]], jnp.max(sr, axis=0, keepdims=True) - c_r)
                p = jnp.exp2(sr - (m_new + c_r))
                m_parts.append(m_new)
                sum_parts.append(jnp.sum(p, axis=0, keepdims=True))
                p_parts.append(p.astype(BF16))
            m_new = jnp.concatenate(m_parts, axis=1)
            alpha = jnp.exp2(m_old - m_new)
            l_sc[g] = alpha * l_sc[g] + jnp.concatenate(sum_parts, axis=1)
            pv = jnp.dot(vst_ref[g * HEAD_DIM:(g + 1) * HEAD_DIM, pl.ds(k0, tk)], jnp.concatenate(p_parts, axis=1),
                         preferred_element_type=F32)
            acc_sc[g] = alpha * acc_sc[g] + pv
            m_sc[g] = m_new

    def full_tile(ti, carry):
        sel_tile(ti, False)
        return carry

    lax.fori_loop(0, last, full_tile, 0)
    sel_tile(last, True)

    wk = WINDOW + Q_BLOCK
    k0w = pl.multiple_of(jnp.maximum(start - WINDOW, 0), Q_BLOCK)
    kwt = kw_ref[pl.ds(k0w, wk), :]
    dist_w = qpos - (k0w + lax.broadcasted_iota(jnp.int32, (wk, 1), 0))
    wbias = jnp.where((dist_w >= 0) & (dist_w < WINDOW), 0.0, NEG)
    out_rows = []
    for g in groups:
        sw = jnp.dot(kwt, qbd_sc[g], preferred_element_type=F32)
        pw, lw = [], []
        for r in range(Q_PER_KV):
            sr = (sw[:, lane_blk[r]] - bias_ref[g, r, 0:wk, :]) + wbias
            p = jnp.exp2(sr - jnp.max(sr, axis=0, keepdims=True))
            lw.append(jnp.sum(p, axis=0, keepdims=True))
            pw.append(p.astype(BF16))
        o_w = jnp.dot(vwt_ref[g * HEAD_DIM:(g + 1) * HEAD_DIM, pl.ds(k0w, wk)], jnp.concatenate(pw, axis=1),
                      preferred_element_type=F32)
        o_w = o_w / jnp.maximum(jnp.concatenate(lw, axis=1), 1e-30)
        o_s = acc_sc[g] / jnp.maximum(l_sc[g], 1e-30)
        o_g = gate(0, g) * o_c[g] + gate(1, g) * o_s + gate(2, g) * o_w
        out_rows += [o_g[:, lane_blk[r]] for r in range(Q_PER_KV)]
    o_ref[...] = jnp.concatenate(out_rows, axis=0).T.astype(BF16)


def _pair_matrix(nc, ns_pad):
    n = jnp.arange(nc)[:, None]
    j = jnp.arange(ns_pad)[None, :]
    return jnp.where((n // (SEL_BLOCK // CMP_BLOCK)) == j, 1.0, 0.0).astype(BF16)


def _nsa_prompt(slopes, proj3, kc_perm, vc_t, kv_bf, vs_t, vw_t):
    b, t, c = proj3.shape
    nqb = t // Q_BLOCK
    cols = Q_PER_KV * Q_BLOCK
    nc = t // CMP_BLOCK
    ns = t // SEL_BLOCK
    tk = min(512, t)
    wk = WINDOW + Q_BLOCK
    slopes2 = slopes * LOG2E
    rel = (jnp.arange(Q_BLOCK)[None, :] - jnp.arange(wk)[:, None]).astype(F32)
    bias_tab = slopes2[:, :, None, None] * rel[None, None]
    vt_spec = pl.BlockSpec((None, KV_W, t), lambda bi, i: (bi, 0, 0))
    kv_spec = lambda col: pl.BlockSpec((None, t, KV_W), lambda bi, i: (bi, 0, col))
    return pl.pallas_call(
        functools.partial(_nsa_prompt_kernel, t=t, tk=tk),
        grid=(b, nqb),
        in_specs=[
            pl.BlockSpec(memory_space=pltpu.SMEM),
            pl.BlockSpec((None, Q_BLOCK, B_WIDTH), lambda bi, i: (bi, i, COL_Q // B_WIDTH)),
            pl.BlockSpec((None, Q_BLOCK, TN), lambda bi, i: (bi, i, c // TN - 1)),
            pl.BlockSpec((None, nc, KV_W), lambda bi, i: (bi, 0, 0)),
            pl.BlockSpec((None, N_KV, HEAD_DIM, nc), lambda bi, i: (bi, 0, 0, 0)),
            kv_spec(2), vt_spec, kv_spec(4), vt_spec,
            pl.BlockSpec((N_KV, Q_PER_KV, wk, Q_BLOCK), lambda bi, i: (0, 0, 0, 0)),
        ],
        out_specs=pl.BlockSpec((None, Q_BLOCK, B_WIDTH), lambda bi, i: (bi, i, 0)),
        out_shape=jax.ShapeDtypeStruct((b, t, B_WIDTH), BF16),
        scratch_shapes=[pltpu.VMEM((N_KV, KV_W, cols), BF16), pltpu.VMEM((N_KV, ns, Q_BLOCK), F32),
                        pltpu.VMEM((N_KV, 1, cols), F32), pltpu.VMEM((N_KV, 1, cols), F32),
                        pltpu.VMEM((N_KV, HEAD_DIM, cols), F32)],
        compiler_params=_cparams(("arbitrary", "arbitrary"), 56),
        name="nsa_prompt",
    )(slopes2, proj3, proj3, kc_perm, vc_t, kv_bf, vs_t, kv_bf, vw_t, bias_tab)


def _branch_kernel(oa_ref, ob_ref, w_ref, g0_ref, g1_ref, o_ref, wbf_ref):
    @pl.when(pl.program_id(1) == 0)
    def _():
        wbf_ref[...] = w_ref[...].astype(BF16)

    tm = oa_ref.shape[0]
    sub = min(tm, SUB_ROWS)
    for c in range(tm // sub):
        rows = slice(c * sub, (c + 1) * sub)
        pa = jnp.dot(oa_ref[rows, :], wbf_ref[:A_WIDTH, :], preferred_element_type=F32)
        pb = jnp.dot(ob_ref[rows, :], wbf_ref[A_WIDTH:, :], preferred_element_type=F32)
        o_ref[rows, :] = (g0_ref[rows, :] * pa + g1_ref[rows, :] * pb).astype(BF16)


def _branch(o_a, o_b, w_branch, proj, l, tm):
    m = o_a.shape[0]
    kb, d = w_branch.shape[1:]
    nj = d // TN
    return pl.pallas_call(
        _branch_kernel,
        grid=(nj, m // tm),
        in_specs=[
            pl.BlockSpec((tm, A_WIDTH), lambda j, i: (i, 0)),
            pl.BlockSpec((tm, B_WIDTH), lambda j, i: (i, 0)),
            pl.BlockSpec((None, kb, TN), lambda j, i: (l, 0, j)),
            pl.BlockSpec((tm, TN), lambda j, i: (i, N_MAIN_BLOCKS + j)),
            pl.BlockSpec((tm, TN), lambda j, i: (i, N_MAIN_BLOCKS + nj + j)),
        ],
        out_specs=pl.BlockSpec((tm, TN), lambda j, i: (i, j)),
        out_shape=jax.ShapeDtypeStruct((m, d), BF16),
        scratch_shapes=[pltpu.VMEM((kb, TN), BF16)],
        compiler_params=_cparams(("arbitrary", "arbitrary"), 48),
        name="branch",
    )(o_a, o_b, w_branch, proj, proj)


def _resid_kernel(a_ref, w_ref, x_ref, g_ref, o_ref, wbf_ref):
    @pl.when(pl.program_id(1) == 0)
    def _():
        wbf_ref[...] = w_ref[...].astype(BF16)

    tm = a_ref.shape[0]
    sub = min(tm, SUB_ROWS)
    for c in range(tm // sub):
        rows = slice(c * sub, (c + 1) * sub)
        g = g_ref[...] if g_ref.shape[0] == 1 else g_ref[rows, :]
        o_ref[rows, :] = x_ref[rows, :] + g * jnp.dot(a_ref[rows, :], wbf_ref[...], preferred_element_type=F32)


def _resid_matmul(a, w, x2, gate, l, tm, name):
    m, k = a.shape
    d = w.shape[2]
    if gate.shape[0] == m:
        gate_spec = pl.BlockSpec((tm, TN), lambda j, i: (i, j))
    else:
        nb = gate.shape[0]
        per = (m // nb) // tm
        gate = gate.reshape(nb, 1, d)
        gate_spec = pl.BlockSpec((None, 1, TN), lambda j, i: (i // per, 0, j))
    return pl.pallas_call(
        _resid_kernel,
        grid=(d // TN, m // tm),
        in_specs=[
            pl.BlockSpec((tm, k), lambda j, i: (i, 0)),
            pl.BlockSpec((None, k, TN), lambda j, i: (l, 0, j), pipeline_mode=pl.Buffered(1)),
            pl.BlockSpec((tm, TN), lambda j, i: (i, j)),
            gate_spec,
        ],
        out_specs=pl.BlockSpec((tm, TN), lambda j, i: (i, j)),
        out_shape=jax.ShapeDtypeStruct((m, d), F32),
        scratch_shapes=[pltpu.VMEM((k, TN), BF16)],
        compiler_params=_cparams(("arbitrary", "arbitrary"), 56),
        name=name,
    )(a, w, x2, gate)


def _resid_norm_kernel(a_ref, w_ref, x_ref, g_ref, nw_ref, sh_ref, sc_ref, o_ref, h_ref, wbf_ref):
    @pl.when(pl.program_id(0) == 0)
    def _():
        for c in range(wbf_ref.shape[1] // TN):
            cols = slice(c * TN, (c + 1) * TN)
            wbf_ref[:, cols] = w_ref[:, cols].astype(BF16)

    tm = a_ref.shape[0]
    sub = min(tm, SUB_ROWS)
    for c in range(tm // sub):
        rows = slice(c * sub, (c + 1) * sub)
        x = x_ref[rows, :] + g_ref[...] * jnp.dot(a_ref[rows, :], wbf_ref[...], preferred_element_type=F32)
        o_ref[rows, :] = x
        y = x * lax.rsqrt(jnp.mean(x * x, axis=-1, keepdims=True) + EPS)
        h_ref[rows, :] = ((y * nw_ref[...]) * (1.0 + sc_ref[...]) + sh_ref[...]).astype(BF16)


def _resid_norm_matmul(a, w, x2, gate, norm_w, shift, scale, l, tm):
    m, k = a.shape
    d = w.shape[2]
    nb = gate.shape[0]
    per = (m // nb) // tm
    seq = pl.BlockSpec((None, 1, d), lambda i: (i // per, 0, 0))
    return pl.pallas_call(
        _resid_norm_kernel,
        grid=(m // tm,),
        in_specs=[
            pl.BlockSpec((tm, k), lambda i: (i, 0)),
            pl.BlockSpec((None, k, d), lambda i: (l, 0, 0), pipeline_mode=pl.Buffered(1)),
            pl.BlockSpec((tm, d), lambda i: (i, 0)),
            seq,
            pl.BlockSpec((None, 1, d), lambda i: (l, 0, 0)),
            seq, seq,
        ],
        out_specs=[pl.BlockSpec((tm, d), lambda i: (i, 0)), pl.BlockSpec((tm, d), lambda i: (i, 0))],
        out_shape=[jax.ShapeDtypeStruct((m, d), F32), jax.ShapeDtypeStruct((m, d), BF16)],
        scratch_shapes=[pltpu.VMEM((k, d), BF16)],
        compiler_params=_cparams(("arbitrary",), 56),
        name="attn_out_norm",
    )(a, w, x2, gate.reshape(nb, 1, d), norm_w.reshape(norm_w.shape[0], 1, d), shift.reshape(nb, 1, d),
      scale.reshape(nb, 1, d))


def _ffn_up_kernel(h_ref, wa_ref, wb_ref, cw_ref, cb_ref, s0_ref, s1_ref, o_ref, tail_ref,
                   wa_bf, wb_bf, abuf, *, tm, seg, per_batch, keep):
    i = pl.program_id(1)

    @pl.when(i == 0)
    def _():
        wa_bf[...] = wa_ref[...].astype(BF16)
        wb_bf[...] = wb_ref[...].astype(BF16)

    @pl.when(i % per_batch == 0)
    def _():
        abuf[0:SUBLANES, :] = jnp.zeros((SUBLANES, TN), F32)

    sub = min(tm, FFN_SUB_ROWS)
    for c in range(tm // sub):
        rows = slice(c * sub, (c + 1) * sub)
        h = h_ref[rows, :]
        a = jnp.dot(h, wa_bf[...], preferred_element_type=F32)
        b = jnp.dot(h, wb_bf[...], preferred_element_type=F32)
        abuf[SUBLANES + c * sub:SUBLANES + (c + 1) * sub, :] = a
        a1 = abuf[SUBLANES - 1 + c * sub:SUBLANES - 1 + (c + 1) * sub, :]
        a2 = abuf[SUBLANES - 2 + c * sub:SUBLANES - 2 + (c + 1) * sub, :]
        if seg < tm:
            r = lax.broadcasted_iota(jnp.int32, (sub, 1), 0) & (seg - 1)
            a1 = jnp.where(r == 0, s1_ref[rows, :], a1)
            a2 = jnp.where(r == 0, s0_ref[rows, :], jnp.where(r == 1, s1_ref[rows, :], a2))
        conv = a2 * cw_ref[0:1, :] + a1 * cw_ref[1:2, :] + a * cw_ref[2:3, :] + cb_ref[...]
        o_ref[rows, :] = (jax.nn.gelu(conv) * b).astype(BF16)
    tail_ref[...] = abuf[SUBLANES + tm - keep:SUBLANES + tm, :]
    abuf[0:SUBLANES, :] = abuf[tm:tm + SUBLANES, :]


def _ffn_up(h2, w_a, w_b, conv_w, conv_b, s0, s1, t, l, tm):
    m, d = h2.shape
    dff = w_a.shape[2]
    per_batch = max(t // tm, 1)
    seg = min(t, tm)
    keep = SUBLANES if seg == tm else tm
    if s0 is None:
        s0 = s1 = jnp.zeros((SUBLANES, dff), F32)
        s_spec = pl.BlockSpec((SUBLANES, TN), lambda j, i: (0, j))
    else:
        s_spec = pl.BlockSpec((tm, TN), lambda j, i: (i, j))
    w_spec = pl.BlockSpec((None, d, TN), lambda j, i: (l, 0, j))
    return pl.pallas_call(
        functools.partial(_ffn_up_kernel, tm=tm, seg=seg, per_batch=per_batch, keep=keep),
        grid=(dff // TN, m // tm),
        in_specs=[
            pl.BlockSpec((tm, d), lambda j, i: (i, 0)),
            w_spec, w_spec,
            pl.BlockSpec((None, CONV_W, TN), lambda j, i: (l, 0, j)),
            pl.BlockSpec((None, 1, TN), lambda j, i: (l, 0, j)),
            s_spec, s_spec,
        ],
        out_specs=[
            pl.BlockSpec((tm, TN), lambda j, i: (i, j)),
            pl.BlockSpec((None, keep, TN), lambda j, i: (i, 0, j)),
        ],
        out_shape=[jax.ShapeDtypeStruct((m, dff), BF16), jax.ShapeDtypeStruct((m // tm, keep, dff), F32)],
        scratch_shapes=[pltpu.VMEM((d, TN), BF16), pltpu.VMEM((d, TN), BF16), pltpu.VMEM((tm + SUBLANES, TN), F32)],
        compiler_params=_cparams(("arbitrary", "arbitrary"), 56),
        name="ffn_up",
    )(h2, w_a, w_b, conv_w, conv_b.reshape(conv_b.shape[0], 1, dff), s0, s1)


def _gather_pages(pt_ref, b, c, pool_ref, l, buf_ref, sem, n_pages):
    copies = []
    for p in range(n_pages):
        page = pt_ref[b, c * n_pages + p]
        copies.append(pltpu.make_async_copy(pool_ref.at[l, page], buf_ref.at[:, pl.ds(p * PAGE_SIZE, PAGE_SIZE)], sem))
    return copies


def _gather_compress_kernel(pt_ref, poolk_ref, poolv_ref, bdk_ref, bdv_ref, pek_ref, pev_ref, knw_ref, gm_ref,
                            kc_ref, vc_ref, kbuf, vbuf, tbuf, sems, *, l, n_chunks, n_steps):
    b = pl.program_id(0)
    c = pl.program_id(1)
    step = b * n_chunks + c
    slot = step % 2

    def chunk_copies(bb, cc, sl):
        return (_gather_pages(pt_ref, bb, cc, poolk_ref, l, kbuf.at[sl], sems.at[0, sl], CMP_PAGES),
                _gather_pages(pt_ref, bb, cc, poolv_ref, l, vbuf.at[sl], sems.at[1, sl], CMP_PAGES))

    def token_major(buf):
        for h in range(KV_W // LANES):
            tbuf[h] = buf[h * LANES:(h + 1) * LANES, :].T
        return (tbuf.at[0], tbuf.at[1])

    @pl.when(step == 0)
    def _():
        for cps in chunk_copies(b, c, slot):
            for cp in cps:
                cp.start()

    @pl.when(step + 1 < n_steps)
    def _():
        wrap = c + 1 == n_chunks
        for cps in chunk_copies(jnp.where(wrap, b + 1, b), jnp.where(wrap, 0, c + 1), 1 - slot):
            for cp in cps:
                cp.start()

    kcopies, vcopies = chunk_copies(b, c, slot)
    n_blk = CMP_PAGES * PAGE_SIZE // CMP_BLOCK
    for cp in kcopies:
        cp.wait()
    kc_ref[...] = _group_rms(_compress_rows(token_major(kbuf.at[slot]), n_blk, bdk_ref, pek_ref),
                             gm_ref[...], knw_ref[...])
    for cp in vcopies:
        cp.wait()
    vc_ref[...] = _compress_rows(token_major(vbuf.at[slot]), n_blk, bdv_ref, pev_ref)


def _gather_compress(page_table, pool_k, pool_v, cmp_consts, l):
    bd, n_pages = page_table.shape
    n_chunks = n_pages // CMP_PAGES
    n_blk = CMP_PAGES * PAGE_SIZE // CMP_BLOCK
    bdk, bdv, pek, pev, knw, gm = cmp_consts
    const3 = lambda b, c, pt: (0, 0, 0)
    const2 = lambda b, c, pt: (0, 0)
    grid_spec = pltpu.PrefetchScalarGridSpec(
        num_scalar_prefetch=1,
        grid=(bd, n_chunks),
        in_specs=[
            pl.BlockSpec(memory_space=pl.ANY),
            pl.BlockSpec(memory_space=pl.ANY),
            pl.BlockSpec((CMP_BLOCK, LANES, LANES), const3),
            pl.BlockSpec((CMP_BLOCK, LANES, LANES), const3),
            pl.BlockSpec((CMP_BLOCK, LANES), const2),
            pl.BlockSpec((CMP_BLOCK, LANES), const2),
            pl.BlockSpec((1, KV_W), const2),
            pl.BlockSpec((KV_W, KV_W), const2),
        ],
        out_specs=[pl.BlockSpec((None, n_blk, KV_W), lambda b, c, pt: (b, c, 0))] * 2,
        scratch_shapes=[
            pltpu.VMEM((2, KV_W, CMP_PAGES * PAGE_SIZE), F32),
            pltpu.VMEM((2, KV_W, CMP_PAGES * PAGE_SIZE), F32),
            pltpu.VMEM((KV_W // LANES, CMP_PAGES * PAGE_SIZE, LANES), F32),
            pltpu.SemaphoreType.DMA((2, 2)),
        ],
    )
    return pl.pallas_call(
        functools.partial(_gather_compress_kernel, l=l, n_chunks=n_chunks, n_steps=bd * n_chunks),
        grid_spec=grid_spec,
        out_shape=[jax.ShapeDtypeStruct((bd, n_chunks * n_blk, KV_W), F32)] * 2,
        compiler_params=_cparams(("arbitrary", "arbitrary"), 48),
        name="gather_compress",
    )(page_table, pool_k, pool_v, bdk, bdv, pek, pev, knw, gm)


def _fold_groups(x):
    rows = x.shape[0]
    grp = lax.broadcasted_iota(jnp.int32, (rows, 1), 0) >> 5
    out = jnp.zeros((rows, HEAD_DIM), F32)
    for g in range(N_KV):
        out = out + jnp.where(grp == g, x[:, g * HEAD_DIM:(g + 1) * HEAD_DIM], 0.0)
    return out


def _nt_dot(a, b):
    return lax.dot_general(a, b, (((1,), (1,)), ((), ())), preferred_element_type=F32)


def _sample_cmp_kernel(q_ref, kc_ref, vc_ref, slope_ref, pair_ref, oc_ref, sel_ref, *, past, nc, ns, ns_pad):
    rows = N_HEADS * SUBLANES
    q = (q_ref[...] * SCALE).astype(BF16)
    ncp = kc_ref.shape[0]
    s = _nt_dot(q, kc_ref[...].astype(BF16))
    qpos = past + (lax.broadcasted_iota(jnp.int32, (rows, 1), 0) & (SUBLANES - 1))
    n = lax.broadcasted_iota(jnp.int32, (1, ncp), 1)
    dist = qpos - ((n + 1) * CMP_BLOCK - 1)
    mask = (dist >= 0) & (n < nc)
    p = _softmax_rows(s - slope_ref[...] * dist.astype(F32), mask)
    oc_ref[...] = _fold_groups(jnp.dot(p.astype(BF16), vc_ref[...].astype(BF16), preferred_element_type=F32))

    parts = []
    for g in range(N_KV):
        acc = p[(g * Q_PER_KV) * SUBLANES:(g * Q_PER_KV + 1) * SUBLANES]
        for r in range(1, Q_PER_KV):
            acc = acc + p[(g * Q_PER_KV + r) * SUBLANES:(g * Q_PER_KV + r + 1) * SUBLANES]
        parts.append(acc)
    psum = jnp.concatenate(parts, axis=0)
    imp = _pair_sum(psum, pair_ref[...])
    gq = N_KV * SUBLANES
    jj = lax.broadcasted_iota(jnp.int32, (gq, ns_pad), 1)
    cur = (past + (lax.broadcasted_iota(jnp.int32, (gq, 1), 0) & (SUBLANES - 1))) >> 6
    forced = (jj == 0) | (jj == cur) | (jj == cur - 1)
    score = jnp.where(jj <= cur, imp + jnp.where(forced, FORCE, 0.0), NEG)
    sel = _top_mask(score, ns, min(TOP_N, ns))
    pieces = []
    for g in range(N_KV):
        for r in range(Q_PER_KV):
            pieces.append(sel[g * SUBLANES:(g + 1) * SUBLANES])
    sel_rows = jnp.concatenate(pieces, axis=0).astype(BF16)
    per = SEL_PAGES * PAGE_SIZE // SEL_BLOCK
    for c in range(ns_pad // per):
        sel_ref[c] = sel_rows[:, c * per:(c + 1) * per]


def _sample_cmp(qbd, kc_full, vc_full, slope_rows, past, nc, ns):
    bd, rows, _ = qbd.shape
    ncp = kc_full.shape[1]
    per = SEL_PAGES * PAGE_SIZE // SEL_BLOCK
    ns_pad = -(-ns // per) * per
    pair = _pair_matrix(ncp, ns_pad)
    return pl.pallas_call(
        functools.partial(_sample_cmp_kernel, past=past, nc=nc, ns=ns, ns_pad=ns_pad),
        grid=(bd,),
        in_specs=[
            pl.BlockSpec((None, rows, KV_W), lambda b: (b, 0, 0)),
            pl.BlockSpec((None, ncp, KV_W), lambda b: (b, 0, 0)),
            pl.BlockSpec((None, ncp, KV_W), lambda b: (b, 0, 0)),
            pl.BlockSpec((rows, 1), lambda b: (0, 0)),
            pl.BlockSpec((ncp, ns_pad), lambda b: (0, 0)),
        ],
        out_specs=[
            pl.BlockSpec((None, rows, HEAD_DIM), lambda b: (b, 0, 0)),
            pl.BlockSpec((None, ns_pad // per, rows, per), lambda b: (b, 0, 0, 0)),
        ],
        out_shape=[jax.ShapeDtypeStruct((bd, rows, HEAD_DIM), F32),
                   jax.ShapeDtypeStruct((bd, ns_pad // per, rows, per), BF16)],
        compiler_params=_cparams(("arbitrary",), 40),
        name="sample_cmp",
    )(qbd, kc_full, vc_full, slope_rows, pair)


def _sample_sel_kernel(pt_ref, poolk_ref, poolv_ref, q_ref, selc_ref, seln_ref, kn_ref, vn_ref, bk_ref, bv_ref,
                       kwn_ref, vwn_ref, oc_ref, gate_ref, slope_ref, o_ref,
                       kbuf, vbuf, sems, m_sc, l_sc, acc_sc, *, past, n_chunks, n_steps, l):
    b = pl.program_id(0)
    c = pl.program_id(1)
    rows = N_HEADS * SUBLANES
    step = b * n_chunks + c
    slot = step % 2

    def chunk_copies(bb, cc, sl):
        return (_gather_pages(pt_ref, bb, cc, poolk_ref, l, kbuf.at[sl], sems.at[0, sl], SEL_PAGES),
                _gather_pages(pt_ref, bb, cc, poolv_ref, l, vbuf.at[sl], sems.at[1, sl], SEL_PAGES))

    @pl.when(step == 0)
    def _():
        for cps in chunk_copies(b, c, slot):
            for cp in cps:
                cp.start()

    @pl.when(step + 1 < n_steps)
    def _():
        wrap = c + 1 == n_chunks
        for cps in chunk_copies(jnp.where(wrap, b + 1, b), jnp.where(wrap, 0, c + 1), 1 - slot):
            for cp in cps:
                cp.start()

    kcopies, vcopies = chunk_copies(b, c, slot)

    @pl.when(c == 0)
    def _():
        m_sc[...] = jnp.full(m_sc.shape, NEG, F32)
        l_sc[...] = jnp.zeros(l_sc.shape, F32)
        acc_sc[...] = jnp.zeros(acc_sc.shape, F32)

    q = (q_ref[...] * SCALE).astype(BF16)
    qpos = past + (lax.broadcasted_iota(jnp.int32, (rows, 1), 0) & (SUBLANES - 1))
    slope = slope_ref[...]

    def online_update(s, mask, v_bf, keys_on_lanes):
        s = jnp.where(mask, s, NEG)
        m_old = m_sc[...]
        m_new = jnp.maximum(m_old, jnp.max(s, axis=-1, keepdims=True))
        alpha = jnp.exp(m_old - m_new)
        p = jnp.where(mask, jnp.exp(s - m_new), 0.0)
        p_bf = p.astype(BF16)
        pv = _nt_dot(p_bf, v_bf) if keys_on_lanes else jnp.dot(p_bf, v_bf, preferred_element_type=F32)
        l_sc[...] = alpha * l_sc[...] + jnp.sum(p, axis=-1, keepdims=True)
        acc_sc[...] = alpha * acc_sc[...] + pv
        m_sc[...] = m_new

    def sel_scores(k_bf, pos0, sel_blocks, keys_on_lanes):
        nk = k_bf.shape[1] if keys_on_lanes else k_bf.shape[0]
        nb = sel_blocks.shape[1]
        kidx = lax.broadcasted_iota(jnp.int32, (1, nk), 1)
        blk = lax.broadcasted_iota(jnp.int32, (nb, nk), 0)
        expand = jnp.where((kidx >> 6) == blk, 1.0, 0.0).astype(BF16)
        chosen = jnp.dot(sel_blocks, expand, preferred_element_type=F32)
        dist = qpos - (pos0 + kidx)
        mask = (dist >= 0) & (chosen > 0.5)
        qk = jnp.dot(q, k_bf, preferred_element_type=F32) if keys_on_lanes else _nt_dot(q, k_bf)
        return qk - slope * dist.astype(F32), mask

    for cp in kcopies:
        cp.wait()
    s, mask = sel_scores(kbuf[slot].astype(BF16), c * (SEL_PAGES * PAGE_SIZE), selc_ref[...], True)
    for cp in vcopies:
        cp.wait()
    online_update(s, mask, vbuf[slot].astype(BF16), True)

    @pl.when(c == n_chunks - 1)
    def _():
        s2, mask2 = sel_scores(kn_ref[...].astype(BF16), past, seln_ref[:, 0:1], False)
        online_update(s2, mask2, vn_ref[...].astype(BF16), False)
        o_s = _fold_groups(acc_sc[...] / jnp.maximum(l_sc[...], 1e-30))

        wb = bk_ref.shape[0]
        kw = jnp.concatenate([bk_ref[...], kwn_ref[...]], axis=0).astype(BF16)
        vw = jnp.concatenate([bv_ref[...], vwn_ref[...]], axis=0).astype(BF16)
        nkw = kw.shape[0]
        kidx = lax.broadcasted_iota(jnp.int32, (1, nkw), 1)
        kpos = past - wb + kidx
        dist = qpos - kpos
        mask_w = (dist >= 0) & (dist < WINDOW) & (kpos >= 0)
        pw = _softmax_rows(_nt_dot(q, kw) - slope * dist.astype(F32), mask_w)
        o_w = _fold_groups(jnp.dot(pw.astype(BF16), vw, preferred_element_type=F32))
        gate = gate_ref[...]
        o_ref[...] = gate[:, 0:1] * oc_ref[...] + gate[:, 1:2] * o_s + gate[:, 2:3] * o_w


def _sample_sel(page_table, pool_k, pool_v, qbd, sel_chunks, k_new, v_new, buf_k, buf_v, kw_new, vw_new,
                o_c, gate_rows, slope_rows, past, l):
    bd, n_pages = page_table.shape
    n_chunks = n_pages // SEL_PAGES
    rows = qbd.shape[1]
    per = sel_chunks.shape[3]
    wb = buf_k.shape[2]
    npad = kw_new.shape[1]
    buf_spec = pl.BlockSpec((None, None, wb, KV_W), lambda b, c, pt: (l, b, 0, 0))
    row3 = lambda n, w: pl.BlockSpec((None, n, w), lambda b, c, pt: (b, 0, 0))
    grid_spec = pltpu.PrefetchScalarGridSpec(
        num_scalar_prefetch=1,
        grid=(bd, n_chunks),
        in_specs=[
            pl.BlockSpec(memory_space=pl.ANY),
            pl.BlockSpec(memory_space=pl.ANY),
            row3(rows, KV_W),
            pl.BlockSpec((None, None, rows, per), lambda b, c, pt: (b, c, 0, 0)),
            pl.BlockSpec((None, None, rows, per), lambda b, c, pt: (b, n_chunks, 0, 0)),
            row3(SEL_BLOCK, KV_W), row3(SEL_BLOCK, KV_W),
            buf_spec, buf_spec,
            row3(npad, KV_W), row3(npad, KV_W),
            row3(rows, HEAD_DIM),
            row3(rows, 3),
            pl.BlockSpec((rows, 1), lambda b, c, pt: (0, 0)),
        ],
        out_specs=row3(rows, HEAD_DIM),
        scratch_shapes=[
            pltpu.VMEM((2, KV_W, SEL_PAGES * PAGE_SIZE), F32),
            pltpu.VMEM((2, KV_W, SEL_PAGES * PAGE_SIZE), F32),
            pltpu.SemaphoreType.DMA((2, 2)),
            pltpu.VMEM((rows, 1), F32), pltpu.VMEM((rows, 1), F32), pltpu.VMEM((rows, KV_W), F32),
        ],
    )
    return pl.pallas_call(
        functools.partial(_sample_sel_kernel, past=past, n_chunks=n_chunks, n_steps=bd * n_chunks, l=l),
        grid_spec=grid_spec,
        out_shape=jax.ShapeDtypeStruct((bd, rows, HEAD_DIM), F32),
        compiler_params=_cparams(("arbitrary", "arbitrary"), 56),
        name="sample_sel",
    )(page_table, pool_k, pool_v, qbd, sel_chunks, sel_chunks, k_new, v_new, buf_k, buf_v, kw_new, vw_new,
      o_c, gate_rows, slope_rows)


def _alibi_slopes():
    h = jnp.arange(1, N_HEADS + 1, dtype=F32)
    return (2.0 ** (-8.0 * h / N_HEADS)).reshape(N_KV, Q_PER_KV)


def _pad_rows(x, n):
    return jnp.pad(x, ((0, 0), (0, n - x.shape[1]), (0, 0)))


def _layer(l, x, mods, p, sample):
    b, t, d = x.shape
    m = b * t
    x2 = x.reshape(m, d)
    tm = min(t, 1024) if sample is None else m
    slopes = _alibi_slopes()

    h1 = _modulate(x2, p['norm1_w'], mods[0], mods[1], t, l)
    proj = _inproj(h1, p['w_in_t'], p['qn'][l], p['kn'][l], l, tm)
    proj3 = proj.reshape(b, t, proj.shape[1])

    def kv_slice(idx):
        return proj3[:, :, COL_KV + idx * KV_W:COL_KV + (idx + 1) * KV_W]

    k_cmp, v_cmp, k_sel, v_sel, k_win, v_win = [kv_slice(i) for i in range(6)]
    nsa_col = proj.shape[1] - TN
    ng = proj3[:, :, nsa_col:nsa_col + 3 * N_HEADS]
    cmp_consts = p['cmp'][l]

    if sample is None:
        o_a, = _chunkmix(proj, p['a_spatial_w'][l], p['a_spatial_b'][l].T, p['a_norm_w'][l][None, :], CHUNK, CHUNK,
                         False)
        kc, vc = _compress(proj3, COL_KV // KV_W, COL_KV // KV_W + 1, cmp_consts)
        nc = t // CMP_BLOCK

        def vt(xv, n):
            return xv.reshape(b, n, N_KV, HEAD_DIM).transpose(0, 2, 3, 1).astype(BF16)

        def evens_first(xc):
            return xc.reshape(b, nc // 2, 2, KV_W).transpose(0, 2, 1, 3).reshape(b, nc, KV_W)

        kv_bf, vs_t, vw_t = _kv_prep(proj, b, t, tm)
        o_b = _nsa_prompt(slopes, proj3, evens_first(kc).astype(BF16), vt(evens_first(vc), nc),
                          kv_bf.reshape(b, t, 6 * KV_W), vs_t, vw_t).reshape(m, B_WIDTH)
        v_norm = None
        wp = min(WINDOW, t)
        new_bk, new_bv = k_win[:, -wp:], v_win[:, -wp:]
    else:
        ds = t
        eye = jnp.eye(b, dtype=F32)
        w8 = p['a_spatial_w'][l][:, :ds, :ds]
        w_bd = jnp.einsum('ab,gts->gatbs', eye, w8).reshape(A_GROUPS, m, m)
        b_t = jnp.tile(p['a_spatial_b'][l][:, :ds].T, (b, 1))
        o_a, v_norm = _chunkmix(proj, w_bd, b_t, p['a_norm_w'][l][None, :], m, ds, True)

        page_table = sample['page_table']
        n_pages = page_table.shape[1]
        past = n_pages * PAGE_SIZE
        new_pad = -(-ds // SEL_BLOCK) * SEL_BLOCK
        nc = (past + new_pad) // CMP_BLOCK
        ns = (past + new_pad) // SEL_BLOCK
        kc_past, vc_past = _gather_compress(page_table, sample['cache_k_cmp'], sample['cache_v_cmp'], cmp_consts, l)
        new_rows = jnp.concatenate([_pad_rows(k_cmp, new_pad), _pad_rows(v_cmp, new_pad)], axis=2)
        kc_new, vc_new = _compress(new_rows, 0, 1, cmp_consts)
        ncp = -(-nc // LANES) * LANES
        kc_full = _pad_rows(jnp.concatenate([kc_past, kc_new], axis=1), ncp)
        vc_full = _pad_rows(jnp.concatenate([vc_past, vc_new], axis=1), ncp)

        q5 = proj3[:, :, COL_Q:COL_Q + B_WIDTH].reshape(b, ds, N_KV, Q_PER_KV, HEAD_DIM).transpose(0, 2, 3, 1, 4)
        qbd = jnp.einsum('bgrqd,gh->bgrqhd', q5, jnp.eye(N_KV, dtype=F32)).reshape(b, N_HEADS * ds, KV_W)
        slope_rows = jnp.repeat(slopes.reshape(N_HEADS), ds)[:, None]
        gate_rows = ng.reshape(b, ds, 3, N_HEADS).transpose(0, 3, 1, 2).reshape(b, N_HEADS * ds, 3)
        o_c, sel_chunks = _sample_cmp(qbd, kc_full, vc_full, slope_rows, past, nc, ns)
        o_rows = _sample_sel(page_table, sample['cache_k_sel'], sample['cache_v_sel'], qbd, sel_chunks,
                             _pad_rows(k_sel, new_pad), _pad_rows(v_sel, new_pad), sample['cache_k_win'],
                             sample['cache_v_win'], _pad_rows(k_win, LANES), _pad_rows(v_win, LANES), o_c,
                             gate_rows, slope_rows, past, l)
        buf_k, buf_v = sample['cache_k_win'][l], sample['cache_v_win'][l]
        o_b = o_rows.reshape(b, N_HEADS, ds, HEAD_DIM).transpose(0, 2, 1, 3).reshape(m, B_WIDTH).astype(BF16)
        keep = min(WINDOW, buf_k.shape[1] + ds)
        new_bk = jnp.concatenate([buf_k, k_win], axis=1)[:, -keep:]
        new_bv = jnp.concatenate([buf_v, v_win], axis=1)[:, -keep:]

    def gate_of(vec):
        return vec if sample is None else jnp.repeat(vec, t, axis=0)

    mix = _branch(o_a, o_b, p['w_branch'], proj, l, tm)
    if sample is None:
        x_mid, h2 = _resid_norm_matmul(mix, p['w_out'], x2, mods[2], p['norm2_w'], mods[3], mods[4], l, min(t, 512))
    else:
        x_mid = _resid_matmul(mix, p['w_out'], x2, gate_of(mods[2]), l, tm, 'attn_out')
        h2 = _modulate(x_mid, p['norm2_w'], mods[3], mods[4], t, l)
    if sample is None:
        s0 = s1 = None
    else:
        state = sample['state_ffn_conv'][l]
        s0 = jnp.repeat(state[:, 0], t, axis=0)
        s1 = jnp.repeat(state[:, 1], t, axis=0)
    hid, tails = _ffn_up(h2, p['ffn_w_a'], p['ffn_w_b'], p['ffn_conv_w'], p['ffn_conv_b'], s0, s1, t, l, tm)
    x_out = _resid_matmul(hid, p['ffn_w_down'], x_mid, gate_of(mods[5]), l, tm, 'ffn_down')

    if sample is None:
        conv_rows = tails.reshape(b, t // tm, SUBLANES, -1)[:, -1, SUBLANES - (CONV_W - 1):]
    else:
        conv_rows = tails.reshape(b, t, -1)[:, t - (CONV_W - 1):]
    heads = lambda z: z.reshape(b, -1, N_KV, HEAD_DIM)
    state_out = dict(k_cmp=heads(k_cmp), v_cmp=heads(v_cmp), k_sel=heads(k_sel), v_sel=heads(v_sel),
                     k_win=heads(new_bk), v_win=heads(new_bv), conv=conv_rows)
    if sample is not None:
        state_out['chunk_v'] = v_norm.reshape(b, t, A_WIDTH)
    return x_out.reshape(b, t, d), state_out


def kernel(x_prompt, x_sample, cache_k_cmp, cache_v_cmp, cache_k_sel, cache_v_sel, cache_k_win, cache_v_win,
           state_ffn_conv, page_table, c_prompt, c_sample, w_ada, b_ada, norm1_w, norm2_w, w_in, a_norm_w,
           a_spatial_w, a_spatial_b, q_norm_w, k_norm_w, cmp_pe_k, cmp_w_k, cmp_pe_v, cmp_w_v, w_branch, w_out,
           ffn_w_a, ffn_w_b, ffn_conv_w, ffn_conv_b, ffn_w_down):
    depth, d, n_in = w_in.shape
    bp = x_prompt.shape[0]
    bs = x_sample.shape[0]

    rows = -(-(bp + bs) // SUBLANES) * SUBLANES
    c_all = jnp.pad(jnp.concatenate([c_prompt, c_sample], axis=0), ((0, rows - bp - bs), (0, 0)))
    mods_all = _ada(c_all, w_ada, b_ada)

    w_in_t = jnp.swapaxes(w_in, 1, 2)
    shared = dict(
        norm1_w=norm1_w, norm2_w=norm2_w, w_in_t=w_in_t,
        qn=jnp.tile(q_norm_w, (1, TN // HEAD_DIM))[:, None, :], kn=jnp.tile(k_norm_w, (1, TN // HEAD_DIM))[:, None, :],
        a_norm_w=a_norm_w, a_spatial_w=a_spatial_w, a_spatial_b=a_spatial_b,
        cmp=[_cmp_consts(cmp_pe_k[l], cmp_w_k[l], cmp_pe_v[l], cmp_w_v[l], k_norm_w[l]) for l in range(depth)],
        w_branch=w_branch, w_out=w_out, ffn_w_a=ffn_w_a, ffn_w_b=ffn_w_b, ffn_conv_w=ffn_conv_w,
        ffn_conv_b=ffn_conv_b, ffn_w_down=ffn_w_down)
    sample = dict(page_table=page_table, cache_k_cmp=cache_k_cmp, cache_v_cmp=cache_v_cmp, cache_k_sel=cache_k_sel,
                  cache_v_sel=cache_v_sel, cache_k_win=cache_k_win, cache_v_win=cache_v_win,
                  state_ffn_conv=state_ffn_conv)
    for name in ('cache_k_cmp', 'cache_v_cmp', 'cache_k_sel', 'cache_v_sel'):
        pool = sample[name]
        sample[name] = pool.transpose(0, 1, 3, 4, 2).reshape(pool.shape[0], pool.shape[1], KV_W, pool.shape[2])
    for name in ('cache_k_win', 'cache_v_win'):
        buf = sample[name]
        sample[name] = buf.reshape(buf.shape[0], buf.shape[1], buf.shape[2], KV_W)

    xp, xs = x_prompt, x_sample
    st_p, st_s = [], []
    for l in range(depth):
        mods = mods_all[l]
        mp = [mods[:bp, k * d:(k + 1) * d] for k in range(6)]
        ms = [mods[bp:bp + bs, k * d:(k + 1) * d] for k in range(6)]
        xp, sp = _layer(l, xp, mp, shared, None)
        xs, ss = _layer(l, xs, ms, shared, sample)
        st_p.append(sp)
        st_s.append(ss)

    def stack(sts, key):
        return jnp.stack([s[key] for s in sts], axis=0)

    return (xp, xs,
            stack(st_p, 'k_cmp'), stack(st_p, 'v_cmp'), stack(st_p, 'k_sel'), stack(st_p, 'v_sel'),
            stack(st_p, 'k_win'), stack(st_p, 'v_win'), stack(st_p, 'conv'),
            stack(st_s, 'k_cmp'), stack(st_s, 'v_cmp'), stack(st_s, 'k_sel'), stack(st_s, 'v_sel'),
            stack(st_s, 'k_win'), stack(st_s, 'v_win'), stack(st_s, 'chunk_v'), stack(st_s, 'conv'))
```

```python
import functools

import jax
import jax.numpy as jnp
from jax import lax
from jax.experimental import pallas as pl
from jax.experimental.pallas import tpu as pltpu

F32 = jnp.float32
BF16 = jnp.bfloat16

HEAD_DIM = 64
N_KV = 4
Q_PER_KV = 4
N_HEADS = N_KV * Q_PER_KV
KV_W = N_KV * HEAD_DIM
A_GROUPS = 8
A_GROUP_DIM = 128
A_WIDTH = A_GROUPS * A_GROUP_DIM
B_WIDTH = N_HEADS * HEAD_DIM
CHUNK = 128
CMP_BLOCK = 32
SEL_BLOCK = 64
TOP_N = 16
WINDOW = 512
Q_BLOCK = 128
PAGE_SIZE = 128
CONV_W = 3
SCALE = HEAD_DIM ** -0.5
EPS = 1e-6
NEG = -1e30
FORCE = 1e4
LOG2E = 1.4426950408889634

LANES = 128
SUBLANES = 8
MIB = 1024 * 1024

TN = 512
COL_U = 0
COL_V = A_WIDTH
COL_Q = 2 * A_WIDTH
COL_KV = COL_Q + B_WIDTH
N_MAIN = COL_KV + 6 * KV_W
N_MAIN_BLOCKS = N_MAIN // TN
SEL_PAGES = 32
CMP_PAGES = 32
SUB_ROWS = 256
FFN_SUB_ROWS = 128


def _cparams(sem, vmem_mib):
    return pltpu.CompilerParams(dimension_semantics=sem, vmem_limit_bytes=vmem_mib * MIB)


def _group_rms(x, gmat, w):
    y = x * x
    hi = y.astype(BF16)
    lo = (y - hi.astype(F32)).astype(BF16)
    ms = jnp.dot(hi, gmat, preferred_element_type=F32) + jnp.dot(lo, gmat, preferred_element_type=F32)
    return x * lax.rsqrt(ms + EPS) * w


def _group_mean_matrix(n):
    r = jnp.arange(n)
    return jnp.where((r[:, None] // HEAD_DIM) == (r[None, :] // HEAD_DIM), 1.0 / HEAD_DIM, 0.0).astype(BF16)


def _ada_kernel(c_ref, w_ref, b_ref, o_ref):
    c = c_ref[...]
    s = (c * jax.nn.sigmoid(c)).astype(BF16)
    o_ref[...] = jnp.dot(s, w_ref[...].astype(BF16), preferred_element_type=F32) + b_ref[...]


def _ada(c_all, w_ada, b_ada):
    depth, d, n6 = w_ada.shape
    rows = c_all.shape[0]
    tn = 1024
    return pl.pallas_call(
        _ada_kernel,
        grid=(depth, n6 // tn),
        in_specs=[
            pl.BlockSpec((rows, d), lambda l, j: (0, 0)),
            pl.BlockSpec((None, d, tn), lambda l, j: (l, 0, j)),
            pl.BlockSpec((None, 1, tn), lambda l, j: (l, 0, j)),
        ],
        out_specs=pl.BlockSpec((None, rows, tn), lambda l, j: (l, 0, j)),
        out_shape=jax.ShapeDtypeStruct((depth, rows, n6), F32),
        compiler_params=_cparams(("arbitrary", "arbitrary"), 40),
        name="ada",
    )(c_all, w_ada, b_ada.reshape(depth, 1, n6))


def _modulate_kernel(x_ref, g_ref, sh_ref, sc_ref, o_ref):
    x = x_ref[...]
    y = x * lax.rsqrt(jnp.mean(x * x, axis=-1, keepdims=True) + EPS)
    y = y * g_ref[...]
    o_ref[...] = (y * (1.0 + sc_ref[...]) + sh_ref[...]).astype(BF16)


def _modulate(x2, g, shift, scale, t, l):
    m, d = x2.shape
    nb = m // t
    tm = min(t, 512)
    per = t // tm
    return pl.pallas_call(
        _modulate_kernel,
        grid=(m // tm,),
        in_specs=[
            pl.BlockSpec((tm, d), lambda i: (i, 0)),
            pl.BlockSpec((None, 1, d), lambda i: (l, 0, 0)),
            pl.BlockSpec((None, 1, d), lambda i: (i // per, 0, 0)),
            pl.BlockSpec((None, 1, d), lambda i: (i // per, 0, 0)),
        ],
        out_specs=pl.BlockSpec((tm, d), lambda i: (i, 0)),
        out_shape=jax.ShapeDtypeStruct((m, d), BF16),
        compiler_params=_cparams(("arbitrary",), 32),
        name="modulate",
    )(x2, g.reshape(g.shape[0], 1, d), shift.reshape(nb, 1, d), scale.reshape(nb, 1, d))


def _inproj_kernel(h_ref, w1_ref, w2_ref, qn_ref, kn_ref, gm_ref, o_ref, wbf_ref):
    j = pl.program_id(0)
    i = pl.program_id(1)

    @pl.when((i == 0) & (j < N_MAIN_BLOCKS))
    def _():
        wbf_ref[...] = w1_ref[...].T.astype(BF16)

    @pl.when((i == 0) & (j >= N_MAIN_BLOCKS))
    def _():
        wbf_ref[...] = w2_ref[0].T.astype(BF16)

    tm = h_ref.shape[0]
    sub = min(tm, SUB_ROWS)

    def project(epilogue):
        for c in range(tm // sub):
            rows = slice(c * sub, (c + 1) * sub)
            o_ref[rows, :] = epilogue(jnp.dot(h_ref[rows, :], wbf_ref[...], preferred_element_type=F32))

    @pl.when(j < 4)
    def _():
        project(jax.nn.gelu)

    @pl.when((j == 4) | (j == 5))
    def _():
        project(lambda acc: _group_rms(acc, gm_ref[...], qn_ref[...]))

    @pl.when(j == 6)
    def _():
        project(lambda acc: acc)

    @pl.when((j == 7) | (j == 8))
    def _():
        lane = lax.broadcasted_iota(jnp.int32, (sub, TN), 1)
        project(lambda acc: jnp.where(lane < KV_W, _group_rms(acc, gm_ref[...], kn_ref[...]), acc))

    @pl.when(j >= N_MAIN_BLOCKS)
    def _():
        project(jax.nn.sigmoid)


def _inproj(h, w_in_t, qn, kn, l, tm):
    m, d = h.shape
    n_in = w_in_t.shape[1]
    n_merge = n_in - N_MAIN - 3 * N_HEADS
    n_gm = n_merge // TN + 1
    nj = N_MAIN_BLOCKS + n_gm
    gm = _group_mean_matrix(TN)

    def gate_rows(j, i):
        k = jnp.clip(j - N_MAIN_BLOCKS, 0, n_gm - 1)
        return (l, pl.multiple_of(jnp.where(k == n_gm - 1, N_MAIN, N_MAIN + 3 * N_HEADS + k * TN), SUBLANES), 0)

    return pl.pallas_call(
        _inproj_kernel,
        grid=(nj, m // tm),
        in_specs=[
            pl.BlockSpec((tm, d), lambda j, i: (i, 0)),
            pl.BlockSpec((None, TN, d), lambda j, i: (l, jnp.minimum(j, N_MAIN_BLOCKS - 1), 0)),
            pl.BlockSpec((pl.Element(1), pl.Element(TN), pl.Element(d)), gate_rows),
            pl.BlockSpec((1, TN), lambda j, i: (0, 0)),
            pl.BlockSpec((1, TN), lambda j, i: (0, 0)),
            pl.BlockSpec((TN, TN), lambda j, i: (0, 0)),
        ],
        out_specs=pl.BlockSpec((tm, TN), lambda j, i: (i, j)),
        out_shape=jax.ShapeDtypeStruct((m, nj * TN), F32),
        scratch_shapes=[pltpu.VMEM((d, TN), BF16)],
        compiler_params=_cparams(("arbitrary", "arbitrary"), 48),
        name="inproj",
    )(h, w_in_t, w_in_t, qn, kn, gm)


def _kv_prep_kernel(x_ref, kv_ref, vst_ref, vwt_ref):
    x = x_ref[...]
    kv_ref[...] = x.astype(BF16)
    vst_ref[...] = x[:, 3 * KV_W:4 * KV_W].T.astype(BF16)
    vwt_ref[...] = x[:, 5 * KV_W:6 * KV_W].T.astype(BF16)


def _kv_prep(proj, b, t, tm):
    m = proj.shape[0]
    width = 6 * KV_W
    per = t // tm
    vt_spec = pl.BlockSpec((None, KV_W, tm), lambda i: (i // per, 0, i % per))
    return pl.pallas_call(
        _kv_prep_kernel,
        grid=(m // tm,),
        in_specs=[pl.BlockSpec((tm, width), lambda i: (i, COL_KV // width))],
        out_specs=[pl.BlockSpec((tm, width), lambda i: (i, 0)), vt_spec, vt_spec],
        out_shape=[jax.ShapeDtypeStruct((m, width), BF16), jax.ShapeDtypeStruct((b, KV_W, t), BF16),
                   jax.ShapeDtypeStruct((b, KV_W, t), BF16)],
        compiler_params=_cparams(("arbitrary",), 40),
        name="kv_prep",
    )(proj)


def _chunkmix_kernel(u_ref, gv_ref, w_ref, bt_ref, an_ref, oa_ref, *v_refs, lc, causal_block):
    gv = gv_ref[...]
    v = gv * lax.rsqrt(jnp.mean(gv * gv, axis=-1, keepdims=True) + EPS) * an_ref[...]
    for v_ref in v_refs:
        v_ref[...] = v
    row = lax.broadcasted_iota(jnp.int32, (lc, lc), 0)
    col = lax.broadcasted_iota(jnp.int32, (lc, lc), 1)
    shift = causal_block.bit_length() - 1
    mask = (col <= row) & ((row >> shift) == (col >> shift))
    vb = v.astype(BF16)
    for g in range(A_GROUPS):
        sl = slice(g * A_GROUP_DIM, (g + 1) * A_GROUP_DIM)
        wg = jnp.where(mask, w_ref[g], 0.0).astype(BF16)
        s = jnp.dot(wg, vb[:, sl], preferred_element_type=F32) + bt_ref[:, g:g + 1]
        oa_ref[:, sl] = (u_ref[:, sl] * s).astype(BF16)


def _chunkmix(proj, w_s, b_t, a_norm, lc, causal_block, keep_v):
    m = proj.shape[0]
    n_out = 2 if keep_v else 1
    return pl.pallas_call(
        functools.partial(_chunkmix_kernel, lc=lc, causal_block=causal_block),
        grid=(m // lc,),
        in_specs=[
            pl.BlockSpec((lc, A_WIDTH), lambda c: (c, 0)),
            pl.BlockSpec((lc, A_WIDTH), lambda c: (c, 1)),
            pl.BlockSpec((A_GROUPS, lc, lc), lambda c: (0, 0, 0)),
            pl.BlockSpec((lc, A_GROUPS), lambda c: (0, 0)),
            pl.BlockSpec((1, A_WIDTH), lambda c: (0, 0)),
        ],
        out_specs=[pl.BlockSpec((lc, A_WIDTH), lambda c: (c, 0))] * n_out,
        out_shape=[jax.ShapeDtypeStruct((m, A_WIDTH), BF16), jax.ShapeDtypeStruct((m, A_WIDTH), F32)][:n_out],
        compiler_params=_cparams(("arbitrary",), 32),
        name="chunkmix",
    )(proj, proj, w_s, b_t, a_norm)


def _compress_rows(half_refs, n_blk, bd_ref, pe_ref):
    halves = []
    for ref in half_refs:
        acc = jnp.zeros((n_blk, LANES), F32)
        for i in range(CMP_BLOCK):
            rows = ref[pl.ds(i, n_blk, stride=CMP_BLOCK), :] + pe_ref[i:i + 1, :]
            acc = acc + jnp.dot(rows.astype(BF16), bd_ref[i], preferred_element_type=F32)
        halves.append(acc)
    return jnp.concatenate(halves, axis=1)


def _compress_phase_major(tbuf, n_pairs, bd_ref, pe_ref):
    halves = []
    for h in range(KV_W // LANES):
        acc = jnp.zeros((n_pairs * SUBLANES, LANES), F32)
        for i in range(CMP_BLOCK):
            rows = jnp.concatenate([tbuf[h, q, i * SUBLANES:(i + 1) * SUBLANES, :] for q in range(n_pairs)], axis=0)
            rows = rows + pe_ref[i:i + 1, :]
            acc = acc + jnp.dot(rows.astype(BF16), bd_ref[i], preferred_element_type=F32)
        halves.append(acc)
    return jnp.concatenate(halves, axis=1)


def _compress_kernel(k0_ref, k1_ref, v0_ref, v1_ref, bdk_ref, bdv_ref, pek_ref, pev_ref, knw_ref, gm_ref,
                     kc_ref, vc_ref, *, nc):
    kc_ref[...] = _group_rms(_compress_rows((k0_ref, k1_ref), nc, bdk_ref, pek_ref), gm_ref[...], knw_ref[...])
    vc_ref[...] = _compress_rows((v0_ref, v1_ref), nc, bdv_ref, pev_ref)


def _compress(src3, kblk, vblk, cmp_consts):
    b, t, _ = src3.shape
    nc = t // CMP_BLOCK
    bdk, bdv, pek, pev, knw, gm = cmp_consts
    const3 = lambda i: (0, 0, 0)
    const2 = lambda i: (0, 0)
    half = lambda blk, h: pl.BlockSpec((None, t, LANES), lambda i: (i, 0, 2 * blk + h))
    return pl.pallas_call(
        functools.partial(_compress_kernel, nc=nc),
        grid=(b,),
        in_specs=[
            half(kblk, 0), half(kblk, 1), half(vblk, 0), half(vblk, 1),
            pl.BlockSpec((CMP_BLOCK, LANES, LANES), const3),
            pl.BlockSpec((CMP_BLOCK, LANES, LANES), const3),
            pl.BlockSpec((CMP_BLOCK, LANES), const2),
            pl.BlockSpec((CMP_BLOCK, LANES), const2),
            pl.BlockSpec((1, KV_W), const2),
            pl.BlockSpec((KV_W, KV_W), const2),
        ],
        out_specs=[pl.BlockSpec((None, nc, KV_W), lambda i: (i, 0, 0))] * 2,
        out_shape=[jax.ShapeDtypeStruct((b, nc, KV_W), F32)] * 2,
        compiler_params=_cparams(("arbitrary",), 40),
        name="compress",
    )(src3, src3, src3, src3, bdk, bdv, pek, pev, knw, gm)


def _cmp_consts(cmp_pe_k, cmp_w_k, cmp_pe_v, cmp_w_v, k_norm_w):
    per = LANES // HEAD_DIM
    eye = jnp.eye(per, dtype=F32)

    def bd(w):
        return jnp.einsum('gh,ide->igdhe', eye, w).reshape(CMP_BLOCK, LANES, LANES).astype(BF16)

    return (bd(cmp_w_k), bd(cmp_w_v), jnp.tile(cmp_pe_k, (1, per)), jnp.tile(cmp_pe_v, (1, per)),
            jnp.tile(k_norm_w, N_KV)[None, :], _group_mean_matrix(KV_W))


def _pair_sum(psum, pair):
    hi = psum.astype(BF16)
    lo = (psum - hi.astype(F32)).astype(BF16)
    return jnp.dot(hi, pair, preferred_element_type=F32) + jnp.dot(lo, pair, preferred_element_type=F32)


def _top_mask(score, n_valid, top_n):
    rows, width = score.shape
    jj = lax.broadcasted_iota(jnp.int32, (rows, width), 1)
    rank = jnp.zeros((rows, width), F32)
    for k in range(n_valid):
        ck = score[:, k:k + 1]
        beats = (ck > score) | ((ck == score) & (jj > k))
        rank = rank + jnp.where(beats, 1.0, 0.0)
    return jnp.where((rank < top_n) & (jj < n_valid), 1.0, 0.0)


def _softmax_rows(s, mask):
    s = jnp.where(mask, s, NEG)
    m = jnp.max(s, axis=-1, keepdims=True)
    p = jnp.where(mask, jnp.exp(s - m), 0.0)
    return p / jnp.maximum(jnp.sum(p, axis=-1, keepdims=True), 1e-30)


def _softmax2_cols(s2, mask):
    s2 = jnp.where(mask, s2, NEG)
    m = jnp.max(s2, axis=0, keepdims=True)
    p = jnp.where(mask, jnp.exp2(s2 - m), 0.0)
    return p / jnp.maximum(jnp.sum(p, axis=0, keepdims=True), 1e-30)


def _nsa_prompt_kernel(slopes2_ref, q_ref, ng_ref, kc_ref, vct_ref, ks_ref, vst_ref, kw_ref, vwt_ref, bias_ref,
                       o_ref, qbd_sc, sel_sc, m_sc, l_sc, acc_sc, *, t, tk):
    i = pl.program_id(1)
    nc = t // CMP_BLOCK
    ns = t // SEL_BLOCK
    half = nc // 2
    start = i * Q_BLOCK
    groups = range(N_KV)
    lane_blk = [slice(r * Q_BLOCK, (r + 1) * Q_BLOCK) for r in range(Q_PER_KV)]
    slopes2 = [[slopes2_ref[g, r] for r in range(Q_PER_KV)] for g in groups]
    qpos = start + lax.broadcasted_iota(jnp.int32, (1, Q_BLOCK), 1)

    qbd_sc[...] = jnp.zeros(qbd_sc.shape, BF16)
    qt = (q_ref[...].T * (SCALE * LOG2E)).astype(BF16)
    for g in groups:
        for r in range(Q_PER_KV):
            h = g * Q_PER_KV + r
            qbd_sc[g, g * HEAD_DIM:(g + 1) * HEAD_DIM, lane_blk[r]] = qt[h * HEAD_DIM:(h + 1) * HEAD_DIM, :]
    ngt = ng_ref[...].T

    def gate(branch, g):
        r0 = branch * N_HEADS + g * Q_PER_KV
        return jnp.concatenate([ngt[r0 + r:r0 + r + 1, :] for r in range(Q_PER_KV)], axis=1)

    row = lax.broadcasted_iota(jnp.int32, (nc, 1), 0)
    blk_c = jnp.where(row < half, 2 * row, 2 * (row - half) + 1)
    dist_c = qpos - ((blk_c + 1) * CMP_BLOCK - 1)
    mask_c = dist_c >= 0
    dist_cf = dist_c.astype(F32)
    jj = lax.broadcasted_iota(jnp.int32, (ns, 1), 0)
    cur = qpos >> 6
    forced = (jj == 0) | (jj == cur) | (jj == cur - 1)
    kc = kc_ref[...]
    o_c = []
    for g in groups:
        st = jnp.dot(kc, qbd_sc[g], preferred_element_type=F32)
        p_heads = [_softmax2_cols(st[:, lane_blk[r]] - slopes2[g][r] * dist_cf, mask_c) for r in range(Q_PER_KV)]
        o_c.append(jnp.dot(vct_ref[g], jnp.concatenate(p_heads, axis=1).astype(BF16), preferred_element_type=F32))
        psum = ((p_heads[0] + p_heads[1]) + p_heads[2]) + p_heads[3]
        imp = psum[:half] + psum[half:]
        score = jnp.where(jj <= cur, imp + jnp.where(forced, FORCE, 0.0), NEG)
        rank = jnp.zeros((ns, Q_BLOCK), F32)
        for k in range(ns):
            sk = score[k:k + 1, :]
            lo = (k // SUBLANES) * SUBLANES
            hi = lo + SUBLANES
            one = lambda cond: jnp.where(cond, 1.0, 0.0)
            rank = rank + jnp.concatenate(
                ([one(sk > score[:lo])] if lo else [])
                + [jnp.where(jj[lo:hi] > k, one(sk >= score[lo:hi]), one(sk > score[lo:hi]))]
                + ([one(sk >= score[hi:])] if hi < ns else []), axis=0)
        sel_sc[g] = jnp.where(rank < min(TOP_N, ns), 0.0, NEG)

    m_sc[...] = jnp.full(m_sc.shape, NEG, F32)
    l_sc[...] = jnp.zeros(l_sc.shape, F32)
    acc_sc[...] = jnp.zeros(acc_sc.shape, F32)
    blocks_per_tile = tk // SEL_BLOCK
    last = start // tk

    def sel_tile(ti, causal):
        k0 = pl.multiple_of(ti * tk, tk)
        kt = ks_ref[pl.ds(k0, tk), :]
        off = start - k0
        offf = off.astype(F32)
        if causal:
            rel = lax.broadcasted_iota(jnp.int32, (tk, Q_BLOCK), 1) - lax.broadcasted_iota(jnp.int32, (tk, Q_BLOCK), 0)
            future = rel + off < 0
        for g in groups:
            s = jnp.dot(kt, qbd_sc[g], preferred_element_type=F32)
            sel_rows = sel_sc[g, pl.ds(pl.multiple_of(ti * blocks_per_tile, blocks_per_tile), blocks_per_tile), :]
            mask_bias = jnp.concatenate(
                [jnp.broadcast_to(sel_rows[jb:jb + 1, :], (SEL_BLOCK, Q_BLOCK)) for jb in range(blocks_per_tile)],
                axis=0)
            if causal:
                mask_bias = jnp.where(future, NEG, mask_bias)
            m_old = m_sc[g]
            p_parts, m_parts, sum_parts = [], [], []
            for r in range(Q_PER_KV):
                c_r = slopes2[g][r] * offf
                sr = (s[:, lane_blk[r]] - bias_ref[g, r, 0:tk, :]) + mask_bias
                m_new = jnp.maximum(m_old[:, lane_blk[r]], jnp.max(sr, axis=0, keepdims=True) - c_r)
                p = jnp.exp2(sr - (m_new + c_r))
                m_parts.append(m_new)
                sum_parts.append(jnp.sum(p, axis=0, keepdims=True))
                p_parts.append(p.astype(BF16))
            m_new = jnp.concatenate(m_parts, axis=1)
            alpha = jnp.exp2(m_old - m_new)
            l_sc[g] = alpha * l_sc[g] + jnp.concatenate(sum_parts, axis=1)
            pv = jnp.dot(vst_ref[g * HEAD_DIM:(g + 1) * HEAD_DIM, pl.ds(k0, tk)], jnp.concatenate(p_parts, axis=1),
                         preferred_element_type=F32)
            acc_sc[g] = alpha * acc_sc[g] + pv
            m_sc[g] = m_new

    def full_tile(ti, carry):
        sel_tile(ti, False)
        return carry

    lax.fori_loop(0, last, full_tile, 0)
    sel_tile(last, True)

    wk = WINDOW + Q_BLOCK
    k0w = pl.multiple_of(jnp.maximum(start - WINDOW, 0), Q_BLOCK)
    kwt = kw_ref[pl.ds(k0w, wk), :]
    dist_w = qpos - (k0w + lax.broadcasted_iota(jnp.int32, (wk, 1), 0))
    wbias = jnp.where((dist_w >= 0) & (dist_w < WINDOW), 0.0, NEG)
    out_rows = []
    for g in groups:
        sw = jnp.dot(kwt, qbd_sc[g], preferred_element_type=F32)
        pw, lw = [], []
        for r in range(Q_PER_KV):
            sr = (sw[:, lane_blk[r]] - bias_ref[g, r, 0:wk, :]) + wbias
            p = jnp.exp2(sr - jnp.max(sr, axis=0, keepdims=True))
            lw.append(jnp.sum(p, axis=0, keepdims=True))
            pw.append(p.astype(BF16))
        o_w = jnp.dot(vwt_ref[g * HEAD_DIM:(g + 1) * HEAD_DIM, pl.ds(k0w, wk)], jnp.concatenate(pw, axis=1),
                      preferred_element_type=F32)
        o_w = o_w / jnp.maximum(jnp.concatenate(lw, axis=1), 1e-30)
        o_s = acc_sc[g] / jnp.maximum(l_sc[g], 1e-30)
        o_g = gate(0, g) * o_c[g] + gate(1, g) * o_s + gate(2, g) * o_w
        out_rows += [o_g[:, lane_blk[r]] for r in range(Q_PER_KV)]
    o_ref[...] = jnp.concatenate(out_rows, axis=0).T.astype(BF16)


def _pair_matrix(nc, ns_pad):
    n = jnp.arange(nc)[:, None]
    j = jnp.arange(ns_pad)[None, :]
    return jnp.where((n // (SEL_BLOCK // CMP_BLOCK)) == j, 1.0, 0.0).astype(BF16)


def _nsa_prompt(slopes, proj3, kc_perm, vc_t, kv_bf, vs_t, vw_t):
    b, t, c = proj3.shape
    nqb = t // Q_BLOCK
    cols = Q_PER_KV * Q_BLOCK
    nc = t // CMP_BLOCK
    ns = t // SEL_BLOCK
    tk = min(512, t)
    wk = WINDOW + Q_BLOCK
    slopes2 = slopes * LOG2E
    rel = (jnp.arange(Q_BLOCK)[None, :] - jnp.arange(wk)[:, None]).astype(F32)
    bias_tab = slopes2[:, :, None, None] * rel[None, None]
    vt_spec = pl.BlockSpec((None, KV_W, t), lambda bi, i: (bi, 0, 0))
    kv_spec = lambda col: pl.BlockSpec((None, t, KV_W), lambda bi, i: (bi, 0, col))
    return pl.pallas_call(
        functools.partial(_nsa_prompt_kernel, t=t, tk=tk),
        grid=(b, nqb),
        in_specs=[
            pl.BlockSpec(memory_space=pltpu.SMEM),
            pl.BlockSpec((None, Q_BLOCK, B_WIDTH), lambda bi, i: (bi, i, COL_Q // B_WIDTH)),
            pl.BlockSpec((None, Q_BLOCK, TN), lambda bi, i: (bi, i, c // TN - 1)),
            pl.BlockSpec((None, nc, KV_W), lambda bi, i: (bi, 0, 0)),
            pl.BlockSpec((None, N_KV, HEAD_DIM, nc), lambda bi, i: (bi, 0, 0, 0)),
            kv_spec(2), vt_spec, kv_spec(4), vt_spec,
            pl.BlockSpec((N_KV, Q_PER_KV, wk, Q_BLOCK), lambda bi, i: (0, 0, 0, 0)),
        ],
        out_specs=pl.BlockSpec((None, Q_BLOCK, B_WIDTH), lambda bi, i: (bi, i, 0)),
        out_shape=jax.ShapeDtypeStruct((b, t, B_WIDTH), BF16),
        scratch_shapes=[pltpu.VMEM((N_KV, KV_W, cols), BF16), pltpu.VMEM((N_KV, ns, Q_BLOCK), F32),
                        pltpu.VMEM((N_KV, 1, cols), F32), pltpu.VMEM((N_KV, 1, cols), F32),
                        pltpu.VMEM((N_KV, HEAD_DIM, cols), F32)],
        compiler_params=_cparams(("arbitrary", "arbitrary"), 56),
        name="nsa_prompt",
    )(slopes2, proj3, proj3, kc_perm, vc_t, kv_bf, vs_t, kv_bf, vw_t, bias_tab)


def _branch_kernel(oa_ref, ob_ref, w_ref, g0_ref, g1_ref, o_ref, wbf_ref):
    @pl.when(pl.program_id(1) == 0)
    def _():
        wbf_ref[...] = w_ref[...].astype(BF16)

    tm = oa_ref.shape[0]
    sub = min(tm, SUB_ROWS)
    for c in range(tm // sub):
        rows = slice(c * sub, (c + 1) * sub)
        pa = jnp.dot(oa_ref[rows, :], wbf_ref[:A_WIDTH, :], preferred_element_type=F32)
        pb = jnp.dot(ob_ref[rows, :], wbf_ref[A_WIDTH:, :], preferred_element_type=F32)
        o_ref[rows, :] = (g0_ref[rows, :] * pa + g1_ref[rows, :] * pb).astype(BF16)


def _branch(o_a, o_b, w_branch, proj, l, tm):
    m = o_a.shape[0]
    kb, d = w_branch.shape[1:]
    nj = d // TN
    return pl.pallas_call(
        _branch_kernel,
        grid=(nj, m // tm),
        in_specs=[
            pl.BlockSpec((tm, A_WIDTH), lambda j, i: (i, 0)),
            pl.BlockSpec((tm, B_WIDTH), lambda j, i: (i, 0)),
            pl.BlockSpec((None, kb, TN), lambda j, i: (l, 0, j)),
            pl.BlockSpec((tm, TN), lambda j, i: (i, N_MAIN_BLOCKS + j)),
            pl.BlockSpec((tm, TN), lambda j, i: (i, N_MAIN_BLOCKS + nj + j)),
        ],
        out_specs=pl.BlockSpec((tm, TN), lambda j, i: (i, j)),
        out_shape=jax.ShapeDtypeStruct((m, d), BF16),
        scratch_shapes=[pltpu.VMEM((kb, TN), BF16)],
        compiler_params=_cparams(("arbitrary", "arbitrary"), 48),
        name="branch",
    )(o_a, o_b, w_branch, proj, proj)


def _resid_kernel(a_ref, w_ref, x_ref, g_ref, o_ref, wbf_ref):
    @pl.when(pl.program_id(1) == 0)
    def _():
        wbf_ref[...] = w_ref[...].astype(BF16)

    tm = a_ref.shape[0]
    sub = min(tm, SUB_ROWS)
    for c in range(tm // sub):
        rows = slice(c * sub, (c + 1) * sub)
        g = g_ref[...] if g_ref.shape[0] == 1 else g_ref[rows, :]
        o_ref[rows, :] = x_ref[rows, :] + g * jnp.dot(a_ref[rows, :], wbf_ref[...], preferred_element_type=F32)


def _resid_matmul(a, w, x2, gate, l, tm, name):
    m, k = a.shape
    d = w.shape[2]
    if gate.shape[0] == m:
        gate_spec = pl.BlockSpec((tm, TN), lambda j, i: (i, j))
    else:
        nb = gate.shape[0]
        per = (m // nb) // tm
        gate = gate.reshape(nb, 1, d)
        gate_spec = pl.BlockSpec((None, 1, TN), lambda j, i: (i // per, 0, j))
    return pl.pallas_call(
        _resid_kernel,
        grid=(d // TN, m // tm),
        in_specs=[
            pl.BlockSpec((tm, k), lambda j, i: (i, 0)),
            pl.BlockSpec((None, k, TN), lambda j, i: (l, 0, j), pipeline_mode=pl.Buffered(1)),
            pl.BlockSpec((tm, TN), lambda j, i: (i, j)),
            gate_spec,
        ],
        out_specs=pl.BlockSpec((tm, TN), lambda j, i: (i, j)),
        out_shape=jax.ShapeDtypeStruct((m, d), F32),
        scratch_shapes=[pltpu.VMEM((k, TN), BF16)],
        compiler_params=_cparams(("arbitrary", "arbitrary"), 56),
        name=name,
    )(a, w, x2, gate)


def _resid_norm_kernel(a_ref, w_ref, x_ref, g_ref, nw_ref, sh_ref, sc_ref, o_ref, h_ref, wbf_ref):
    @pl.when(pl.program_id(0) == 0)
    def _():
        for c in range(wbf_ref.shape[1] // TN):
            cols = slice(c * TN, (c + 1) * TN)
            wbf_ref[:, cols] = w_ref[:, cols].astype(BF16)

    tm = a_ref.shape[0]
    sub = min(tm, SUB_ROWS)
    for c in range(tm // sub):
        rows = slice(c * sub, (c + 1) * sub)
        x = x_ref[rows, :] + g_ref[...] * jnp.dot(a_ref[rows, :], wbf_ref[...], preferred_element_type=F32)
        o_ref[rows, :] = x
        y = x * lax.rsqrt(jnp.mean(x * x, axis=-1, keepdims=True) + EPS)
        h_ref[rows, :] = ((y * nw_ref[...]) * (1.0 + sc_ref[...]) + sh_ref[...]).astype(BF16)


def _resid_norm_matmul(a, w, x2, gate, norm_w, shift, scale, l, tm):
    m, k = a.shape
    d = w.shape[2]
    nb = gate.shape[0]
    per = (m // nb) // tm
    seq = pl.BlockSpec((None, 1, d), lambda i: (i // per, 0, 0))
    return pl.pallas_call(
        _resid_norm_kernel,
        grid=(m // tm,),
        in_specs=[
            pl.BlockSpec((tm, k), lambda i: (i, 0)),
            pl.BlockSpec((None, k, d), lambda i: (l, 0, 0), pipeline_mode=pl.Buffered(1)),
            pl.BlockSpec((tm, d), lambda i: (i, 0)),
            seq,
            pl.BlockSpec((None, 1, d), lambda i: (l, 0, 0)),
            seq, seq,
        ],
        out_specs=[pl.BlockSpec((tm, d), lambda i: (i, 0)), pl.BlockSpec((tm, d), lambda i: (i, 0))],
        out_shape=[jax.ShapeDtypeStruct((m, d), F32), jax.ShapeDtypeStruct((m, d), BF16)],
        scratch_shapes=[pltpu.VMEM((k, d), BF16)],
        compiler_params=_cparams(("arbitrary",), 56),
        name="attn_out_norm",
    )(a, w, x2, gate.reshape(nb, 1, d), norm_w.reshape(norm_w.shape[0], 1, d), shift.reshape(nb, 1, d),
      scale.reshape(nb, 1, d))


def _ffn_up_kernel(h_ref, wa_ref, wb_ref, cw_ref, cb_ref, s0_ref, s1_ref, o_ref, tail_ref,
                   wa_bf, wb_bf, abuf, *, tm, seg, per_batch, keep):
    i = pl.program_id(1)

    @pl.when(i == 0)
    def _():
        wa_bf[...] = wa_ref[...].astype(BF16)
        wb_bf[...] = wb_ref[...].astype(BF16)

    @pl.when(i % per_batch == 0)
    def _():
        abuf[0:SUBLANES, :] = jnp.zeros((SUBLANES, TN), F32)

    sub = min(tm, FFN_SUB_ROWS)
    for c in range(tm // sub):
        rows = slice(c * sub, (c + 1) * sub)
        h = h_ref[rows, :]
        a = jnp.dot(h, wa_bf[...], preferred_element_type=F32)
        b = jnp.dot(h, wb_bf[...], preferred_element_type=F32)
        abuf[SUBLANES + c * sub:SUBLANES + (c + 1) * sub, :] = a
        a1 = abuf[SUBLANES - 1 + c * sub:SUBLANES - 1 + (c + 1) * sub, :]
        a2 = abuf[SUBLANES - 2 + c * sub:SUBLANES - 2 + (c + 1) * sub, :]
        if seg < tm:
            r = lax.broadcasted_iota(jnp.int32, (sub, 1), 0) & (seg - 1)
            a1 = jnp.where(r == 0, s1_ref[rows, :], a1)
            a2 = jnp.where(r == 0, s0_ref[rows, :], jnp.where(r == 1, s1_ref[rows, :], a2))
        conv = a2 * cw_ref[0:1, :] + a1 * cw_ref[1:2, :] + a * cw_ref[2:3, :] + cb_ref[...]
        o_ref[rows, :] = (jax.nn.gelu(conv) * b).astype(BF16)
    tail_ref[...] = abuf[SUBLANES + tm - keep:SUBLANES + tm, :]
    abuf[0:SUBLANES, :] = abuf[tm:tm + SUBLANES, :]


def _ffn_up(h2, w_a, w_b, conv_w, conv_b, s0, s1, t, l, tm):
    m, d = h2.shape
    dff = w_a.shape[2]
    per_batch = max(t // tm, 1)
    seg = min(t, tm)
    keep = SUBLANES if seg == tm else tm
    if s0 is None:
        s0 = s1 = jnp.zeros((SUBLANES, dff), F32)
        s_spec = pl.BlockSpec((SUBLANES, TN), lambda j, i: (0, j))
    else:
        s_spec = pl.BlockSpec((tm, TN), lambda j, i: (i, j))
    w_spec = pl.BlockSpec((None, d, TN), lambda j, i: (l, 0, j))
    return pl.pallas_call(
        functools.partial(_ffn_up_kernel, tm=tm, seg=seg, per_batch=per_batch, keep=keep),
        grid=(dff // TN, m // tm),
        in_specs=[
            pl.BlockSpec((tm, d), lambda j, i: (i, 0)),
            w_spec, w_spec,
            pl.BlockSpec((None, CONV_W, TN), lambda j, i: (l, 0, j)),
            pl.BlockSpec((None, 1, TN), lambda j, i: (l, 0, j)),
            s_spec, s_spec,
        ],
        out_specs=[
            pl.BlockSpec((tm, TN), lambda j, i: (i, j)),
            pl.BlockSpec((None, keep, TN), lambda j, i: (i, 0, j)),
        ],
        out_shape=[jax.ShapeDtypeStruct((m, dff), BF16), jax.ShapeDtypeStruct((m // tm, keep, dff), F32)],
        scratch_shapes=[pltpu.VMEM((d, TN), BF16), pltpu.VMEM((d, TN), BF16), pltpu.VMEM((tm + SUBLANES, TN), F32)],
        compiler_params=_cparams(("arbitrary", "arbitrary"), 56),
        name="ffn_up",
    )(h2, w_a, w_b, conv_w, conv_b.reshape(conv_b.shape[0], 1, dff), s0, s1)


def _gather_pages(pt_ref, b, c, pool_ref, l, buf_ref, sem, n_pages):
    copies = []
    for p in range(n_pages):
        page = pt_ref[b, c * n_pages + p]
        copies.append(pltpu.make_async_copy(pool_ref.at[l, page], buf_ref.at[:, pl.ds(p * PAGE_SIZE, PAGE_SIZE)], sem))
    return copies


def _gather_compress_kernel(pt_ref, poolk_ref, poolv_ref, bdk_ref, bdv_ref, pek_ref, pev_ref, knw_ref, gm_ref,
                            perm_ref, kc_ref, vc_ref, kbuf, vbuf, tbuf, sems, *, l, n_chunks, n_steps):
    b = pl.program_id(0)
    c = pl.program_id(1)
    step = b * n_chunks + c
    slot = step % 2

    def chunk_copies(bb, cc, sl):
        return (_gather_pages(pt_ref, bb, cc, poolk_ref, l, kbuf.at[sl], sems.at[0, sl], CMP_PAGES),
                _gather_pages(pt_ref, bb, cc, poolv_ref, l, vbuf.at[sl], sems.at[1, sl], CMP_PAGES))

    n_pairs = CMP_PAGES // 2
    pair = 2 * PAGE_SIZE

    def phase_major(buf):
        perm = perm_ref[...]
        for h in range(KV_W // LANES):
            for q in range(n_pairs):
                xt = buf[h * LANES:(h + 1) * LANES, q * pair:(q + 1) * pair].astype(BF16)
                tbuf[h, q] = _nt_dot(perm, xt)

    @pl.when(step == 0)
    def _():
        for cps in chunk_copies(b, c, slot):
            for cp in cps:
                cp.start()

    @pl.when(step + 1 < n_steps)
    def _():
        wrap = c + 1 == n_chunks
        for cps in chunk_copies(jnp.where(wrap, b + 1, b), jnp.where(wrap, 0, c + 1), 1 - slot):
            for cp in cps:
                cp.start()

    kcopies, vcopies = chunk_copies(b, c, slot)
    n_blk = CMP_PAGES * PAGE_SIZE // CMP_BLOCK
    for cp in kcopies:
        cp.wait()
    phase_major(kbuf.at[slot])
    kc_ref[...] = _group_rms(_compress_phase_major(tbuf, n_pairs, bdk_ref, pek_ref), gm_ref[...], knw_ref[...])
    for cp in vcopies:
        cp.wait()
    phase_major(vbuf.at[slot])
    vc_ref[...] = _compress_phase_major(tbuf, n_pairs, bdv_ref, pev_ref)


def _gather_compress(page_table, pool_k, pool_v, cmp_consts, l):
    bd, n_pages = page_table.shape
    n_chunks = n_pages // CMP_PAGES
    n_blk = CMP_PAGES * PAGE_SIZE // CMP_BLOCK
    bdk, bdv, pek, pev, knw, gm = cmp_consts
    r = jnp.arange(2 * PAGE_SIZE)
    src = (r % SUBLANES) * CMP_BLOCK + r // SUBLANES
    perm = (src[:, None] == r[None, :]).astype(BF16)
    const3 = lambda b, c, pt: (0, 0, 0)
    const2 = lambda b, c, pt: (0, 0)
    grid_spec = pltpu.PrefetchScalarGridSpec(
        num_scalar_prefetch=1,
        grid=(bd, n_chunks),
        in_specs=[
            pl.BlockSpec(memory_space=pl.ANY),
            pl.BlockSpec(memory_space=pl.ANY),
            pl.BlockSpec((CMP_BLOCK, LANES, LANES), const3),
            pl.BlockSpec((CMP_BLOCK, LANES, LANES), const3),
            pl.BlockSpec((CMP_BLOCK, LANES), const2),
            pl.BlockSpec((CMP_BLOCK, LANES), const2),
            pl.BlockSpec((1, KV_W), const2),
            pl.BlockSpec((KV_W, KV_W), const2),
            pl.BlockSpec((2 * PAGE_SIZE, 2 * PAGE_SIZE), const2),
        ],
        out_specs=[pl.BlockSpec((None, n_blk, KV_W), lambda b, c, pt: (b, c, 0))] * 2,
        scratch_shapes=[
            pltpu.VMEM((2, KV_W, CMP_PAGES * PAGE_SIZE), F32),
            pltpu.VMEM((2, KV_W, CMP_PAGES * PAGE_SIZE), F32),
            pltpu.VMEM((KV_W // LANES, CMP_PAGES // 2, 2 * PAGE_SIZE, LANES), F32),
            pltpu.SemaphoreType.DMA((2, 2)),
        ],
    )
    return pl.pallas_call(
        functools.partial(_gather_compress_kernel, l=l, n_chunks=n_chunks, n_steps=bd * n_chunks),
        grid_spec=grid_spec,
        out_shape=[jax.ShapeDtypeStruct((bd, n_chunks * n_blk, KV_W), F32)] * 2,
        compiler_params=_cparams(("arbitrary", "arbitrary"), 48),
        name="gather_compress",
    )(page_table, pool_k, pool_v, bdk, bdv, pek, pev, knw, gm, perm)


def _fold_groups(x):
    rows = x.shape[0]
    grp = lax.broadcasted_iota(jnp.int32, (rows, 1), 0) >> 5
    out = jnp.zeros((rows, HEAD_DIM), F32)
    for g in range(N_KV):
        out = out + jnp.where(grp == g, x[:, g * HEAD_DIM:(g + 1) * HEAD_DIM], 0.0)
    return out


def _nt_dot(a, b):
    return lax.dot_general(a, b, (((1,), (1,)), ((), ())), preferred_element_type=F32)


def _sample_cmp_kernel(q_ref, kc_ref, vc_ref, slope_ref, pair_ref, oc_ref, sel_ref, *, past, nc, ns, ns_pad):
    rows = N_HEADS * SUBLANES
    q = (q_ref[...] * SCALE).astype(BF16)
    ncp = kc_ref.shape[0]
    s = _nt_dot(q, kc_ref[...].astype(BF16))
    qpos = past + (lax.broadcasted_iota(jnp.int32, (rows, 1), 0) & (SUBLANES - 1))
    n = lax.broadcasted_iota(jnp.int32, (1, ncp), 1)
    dist = qpos - ((n + 1) * CMP_BLOCK - 1)
    mask = (dist >= 0) & (n < nc)
    p = _softmax_rows(s - slope_ref[...] * dist.astype(F32), mask)
    oc_ref[...] = _fold_groups(jnp.dot(p.astype(BF16), vc_ref[...].astype(BF16), preferred_element_type=F32))

    parts = []
    for g in range(N_KV):
        acc = p[(g * Q_PER_KV) * SUBLANES:(g * Q_PER_KV + 1) * SUBLANES]
        for r in range(1, Q_PER_KV):
            acc = acc + p[(g * Q_PER_KV + r) * SUBLANES:(g * Q_PER_KV + r + 1) * SUBLANES]
        parts.append(acc)
    psum = jnp.concatenate(parts, axis=0)
    imp = _pair_sum(psum, pair_ref[...])
    gq = N_KV * SUBLANES
    jj = lax.broadcasted_iota(jnp.int32, (gq, ns_pad), 1)
    cur = (past + (lax.broadcasted_iota(jnp.int32, (gq, 1), 0) & (SUBLANES - 1))) >> 6
    forced = (jj == 0) | (jj == cur) | (jj == cur - 1)
    score = jnp.where(jj <= cur, imp + jnp.where(forced, FORCE, 0.0), NEG)
    sel = _top_mask(score, ns, min(TOP_N, ns))
    pieces = []
    for g in range(N_KV):
        for r in range(Q_PER_KV):
            pieces.append(sel[g * SUBLANES:(g + 1) * SUBLANES])
    sel_rows = jnp.concatenate(pieces, axis=0).astype(BF16)
    per = SEL_PAGES * PAGE_SIZE // SEL_BLOCK
    for c in range(ns_pad // per):
        sel_ref[c] = sel_rows[:, c * per:(c + 1) * per]


def _sample_cmp(qbd, kc_full, vc_full, slope_rows, past, nc, ns):
    bd, rows, _ = qbd.shape
    ncp = kc_full.shape[1]
    per = SEL_PAGES * PAGE_SIZE // SEL_BLOCK
    ns_pad = -(-ns // per) * per
    pair = _pair_matrix(ncp, ns_pad)
    return pl.pallas_call(
        functools.partial(_sample_cmp_kernel, past=past, nc=nc, ns=ns, ns_pad=ns_pad),
        grid=(bd,),
        in_specs=[
            pl.BlockSpec((None, rows, KV_W), lambda b: (b, 0, 0)),
            pl.BlockSpec((None, ncp, KV_W), lambda b: (b, 0, 0)),
            pl.BlockSpec((None, ncp, KV_W), lambda b: (b, 0, 0)),
            pl.BlockSpec((rows, 1), lambda b: (0, 0)),
            pl.BlockSpec((ncp, ns_pad), lambda b: (0, 0)),
        ],
        out_specs=[
            pl.BlockSpec((None, rows, HEAD_DIM), lambda b: (b, 0, 0)),
            pl.BlockSpec((None, ns_pad // per, rows, per), lambda b: (b, 0, 0, 0)),
        ],
        out_shape=[jax.ShapeDtypeStruct((bd, rows, HEAD_DIM), F32),
                   jax.ShapeDtypeStruct((bd, ns_pad // per, rows, per), BF16)],
        compiler_params=_cparams(("arbitrary",), 40),
        name="sample_cmp",
    )(qbd, kc_full, vc_full, slope_rows, pair)


def _sample_sel_kernel(pt_ref, poolk_ref, poolv_ref, q_ref, selc_ref, seln_ref, kn_ref, vn_ref, bk_ref, bv_ref,
                       kwn_ref, vwn_ref, oc_ref, gate_ref, slope_ref, o_ref,
                       kbuf, vbuf, sems, m_sc, l_sc, acc_sc, *, past, n_chunks, n_steps, l):
    b = pl.program_id(0)
    c = pl.program_id(1)
    rows = N_HEADS * SUBLANES
    step = b * n_chunks + c
    slot = step % 2

    def chunk_copies(bb, cc, sl):
        return (_gather_pages(pt_ref, bb, cc, poolk_ref, l, kbuf.at[sl], sems.at[0, sl], SEL_PAGES),
                _gather_pages(pt_ref, bb, cc, poolv_ref, l, vbuf.at[sl], sems.at[1, sl], SEL_PAGES))

    @pl.when(step == 0)
    def _():
        for cps in chunk_copies(b, c, slot):
            for cp in cps:
                cp.start()

    @pl.when(step + 1 < n_steps)
    def _():
        wrap = c + 1 == n_chunks
        for cps in chunk_copies(jnp.where(wrap, b + 1, b), jnp.where(wrap, 0, c + 1), 1 - slot):
            for cp in cps:
                cp.start()

    kcopies, vcopies = chunk_copies(b, c, slot)

    @pl.when(c == 0)
    def _():
        m_sc[...] = jnp.full(m_sc.shape, NEG, F32)
        l_sc[...] = jnp.zeros(l_sc.shape, F32)
        acc_sc[...] = jnp.zeros(acc_sc.shape, F32)

    q = (q_ref[...] * SCALE).astype(BF16)
    qpos = past + (lax.broadcasted_iota(jnp.int32, (rows, 1), 0) & (SUBLANES - 1))
    slope = slope_ref[...]

    def online_update(s, mask, v_bf, keys_on_lanes):
        s = jnp.where(mask, s, NEG)
        m_old = m_sc[...]
        m_new = jnp.maximum(m_old, jnp.max(s, axis=-1, keepdims=True))
        alpha = jnp.exp(m_old - m_new)
        p = jnp.where(mask, jnp.exp(s - m_new), 0.0)
        p_bf = p.astype(BF16)
        pv = _nt_dot(p_bf, v_bf) if keys_on_lanes else jnp.dot(p_bf, v_bf, preferred_element_type=F32)
        l_sc[...] = alpha * l_sc[...] + jnp.sum(p, axis=-1, keepdims=True)
        acc_sc[...] = alpha * acc_sc[...] + pv
        m_sc[...] = m_new

    def sel_scores(k_bf, pos0, sel_blocks, keys_on_lanes):
        nk = k_bf.shape[1] if keys_on_lanes else k_bf.shape[0]
        nb = sel_blocks.shape[1]
        kidx = lax.broadcasted_iota(jnp.int32, (1, nk), 1)
        blk = lax.broadcasted_iota(jnp.int32, (nb, nk), 0)
        expand = jnp.where((kidx >> 6) == blk, 1.0, 0.0).astype(BF16)
        chosen = jnp.dot(sel_blocks, expand, preferred_element_type=F32)
        dist = qpos - (pos0 + kidx)
        mask = (dist >= 0) & (chosen > 0.5)
        qk = jnp.dot(q, k_bf, preferred_element_type=F32) if keys_on_lanes else _nt_dot(q, k_bf)
        return qk - slope * dist.astype(F32), mask

    for cp in kcopies:
        cp.wait()
    s, mask = sel_scores(kbuf[slot].astype(BF16), c * (SEL_PAGES * PAGE_SIZE), selc_ref[...], True)
    for cp in vcopies:
        cp.wait()
    online_update(s, mask, vbuf[slot].astype(BF16), True)

    @pl.when(c == n_chunks - 1)
    def _():
        s2, mask2 = sel_scores(kn_ref[...].astype(BF16), past, seln_ref[:, 0:1], False)
        online_update(s2, mask2, vn_ref[...].astype(BF16), False)
        o_s = _fold_groups(acc_sc[...] / jnp.maximum(l_sc[...], 1e-30))

        wb = bk_ref.shape[0]
        kw = jnp.concatenate([bk_ref[...], kwn_ref[...]], axis=0).astype(BF16)
        vw = jnp.concatenate([bv_ref[...], vwn_ref[...]], axis=0).astype(BF16)
        nkw = kw.shape[0]
        kidx = lax.broadcasted_iota(jnp.int32, (1, nkw), 1)
        kpos = past - wb + kidx
        dist = qpos - kpos
        mask_w = (dist >= 0) & (dist < WINDOW) & (kpos >= 0)
        pw = _softmax_rows(_nt_dot(q, kw) - slope * dist.astype(F32), mask_w)
        o_w = _fold_groups(jnp.dot(pw.astype(BF16), vw, preferred_element_type=F32))
        gate = gate_ref[...]
        o_ref[...] = gate[:, 0:1] * oc_ref[...] + gate[:, 1:2] * o_s + gate[:, 2:3] * o_w


def _sample_sel(page_table, pool_k, pool_v, qbd, sel_chunks, k_new, v_new, buf_k, buf_v, kw_new, vw_new,
                o_c, gate_rows, slope_rows, past, l):
    bd, n_pages = page_table.shape
    n_chunks = n_pages // SEL_PAGES
    rows = qbd.shape[1]
    per = sel_chunks.shape[3]
    wb = buf_k.shape[2]
    npad = kw_new.shape[1]
    buf_spec = pl.BlockSpec((None, None, wb, KV_W), lambda b, c, pt: (l, b, 0, 0))
    row3 = lambda n, w: pl.BlockSpec((None, n, w), lambda b, c, pt: (b, 0, 0))
    grid_spec = pltpu.PrefetchScalarGridSpec(
        num_scalar_prefetch=1,
        grid=(bd, n_chunks),
        in_specs=[
            pl.BlockSpec(memory_space=pl.ANY),
            pl.BlockSpec(memory_space=pl.ANY),
            row3(rows, KV_W),
            pl.BlockSpec((None, None, rows, per), lambda b, c, pt: (b, c, 0, 0)),
            pl.BlockSpec((None, None, rows, per), lambda b, c, pt: (b, n_chunks, 0, 0)),
            row3(SEL_BLOCK, KV_W), row3(SEL_BLOCK, KV_W),
            buf_spec, buf_spec,
            row3(npad, KV_W), row3(npad, KV_W),
            row3(rows, HEAD_DIM),
            row3(rows, 3),
            pl.BlockSpec((rows, 1), lambda b, c, pt: (0, 0)),
        ],
        out_specs=row3(rows, HEAD_DIM),
        scratch_shapes=[
            pltpu.VMEM((2, KV_W, SEL_PAGES * PAGE_SIZE), F32),
            pltpu.VMEM((2, KV_W, SEL_PAGES * PAGE_SIZE), F32),
            pltpu.SemaphoreType.DMA((2, 2)),
            pltpu.VMEM((rows, 1), F32), pltpu.VMEM((rows, 1), F32), pltpu.VMEM((rows, KV_W), F32),
        ],
    )
    return pl.pallas_call(
        functools.partial(_sample_sel_kernel, past=past, n_chunks=n_chunks, n_steps=bd * n_chunks, l=l),
        grid_spec=grid_spec,
        out_shape=jax.ShapeDtypeStruct((bd, rows, HEAD_DIM), F32),
        compiler_params=_cparams(("arbitrary", "arbitrary"), 56),
        name="sample_sel",
    )(page_table, pool_k, pool_v, qbd, sel_chunks, sel_chunks, k_new, v_new, buf_k, buf_v, kw_new, vw_new,
      o_c, gate_rows, slope_rows)


def _alibi_slopes():
    h = jnp.arange(1, N_HEADS + 1, dtype=F32)
    return (2.0 ** (-8.0 * h / N_HEADS)).reshape(N_KV, Q_PER_KV)


def _pad_rows(x, n):
    return jnp.pad(x, ((0, 0), (0, n - x.shape[1]), (0, 0)))


def _layer(l, x, mods, p, sample):
    b, t, d = x.shape
    m = b * t
    x2 = x.reshape(m, d)
    tm = min(t, 1024) if sample is None else m
    slopes = _alibi_slopes()

    h1 = _modulate(x2, p['norm1_w'], mods[0], mods[1], t, l)
    proj = _inproj(h1, p['w_in_t'], p['qn'][l], p['kn'][l], l, tm)
    proj3 = proj.reshape(b, t, proj.shape[1])

    def kv_slice(idx):
        return proj3[:, :, COL_KV + idx * KV_W:COL_KV + (idx + 1) * KV_W]

    k_cmp, v_cmp, k_sel, v_sel, k_win, v_win = [kv_slice(i) for i in range(6)]
    nsa_col = proj.shape[1] - TN
    ng = proj3[:, :, nsa_col:nsa_col + 3 * N_HEADS]
    cmp_consts = p['cmp'][l]

    if sample is None:
        o_a, = _chunkmix(proj, p['a_spatial_w'][l], p['a_spatial_b'][l].T, p['a_norm_w'][l][None, :], CHUNK, CHUNK,
                         False)
        kc, vc = _compress(proj3, COL_KV // KV_W, COL_KV // KV_W + 1, cmp_consts)
        nc = t // CMP_BLOCK

        def vt(xv, n):
            return xv.reshape(b, n, N_KV, HEAD_DIM).transpose(0, 2, 3, 1).astype(BF16)

        def evens_first(xc):
            return xc.reshape(b, nc // 2, 2, KV_W).transpose(0, 2, 1, 3).reshape(b, nc, KV_W)

        kv_bf, vs_t, vw_t = _kv_prep(proj, b, t, tm)
        o_b = _nsa_prompt(slopes, proj3, evens_first(kc).astype(BF16), vt(evens_first(vc), nc),
                          kv_bf.reshape(b, t, 6 * KV_W), vs_t, vw_t).reshape(m, B_WIDTH)
        v_norm = None
        wp = min(WINDOW, t)
        new_bk, new_bv = k_win[:, -wp:], v_win[:, -wp:]
    else:
        ds = t
        eye = jnp.eye(b, dtype=F32)
        w8 = p['a_spatial_w'][l][:, :ds, :ds]
        w_bd = jnp.einsum('ab,gts->gatbs', eye, w8).reshape(A_GROUPS, m, m)
        b_t = jnp.tile(p['a_spatial_b'][l][:, :ds].T, (b, 1))
        o_a, v_norm = _chunkmix(proj, w_bd, b_t, p['a_norm_w'][l][None, :], m, ds, True)

        page_table = sample['page_table']
        n_pages = page_table.shape[1]
        past = n_pages * PAGE_SIZE
        new_pad = -(-ds // SEL_BLOCK) * SEL_BLOCK
        nc = (past + new_pad) // CMP_BLOCK
        ns = (past + new_pad) // SEL_BLOCK
        kc_past, vc_past = _gather_compress(page_table, sample['cache_k_cmp'], sample['cache_v_cmp'], cmp_consts, l)
        new_rows = jnp.concatenate([_pad_rows(k_cmp, new_pad), _pad_rows(v_cmp, new_pad)], axis=2)
        kc_new, vc_new = _compress(new_rows, 0, 1, cmp_consts)
        ncp = -(-nc // LANES) * LANES
        kc_full = _pad_rows(jnp.concatenate([kc_past, kc_new], axis=1), ncp)
        vc_full = _pad_rows(jnp.concatenate([vc_past, vc_new], axis=1), ncp)

        q5 = proj3[:, :, COL_Q:COL_Q + B_WIDTH].reshape(b, ds, N_KV, Q_PER_KV, HEAD_DIM).transpose(0, 2, 3, 1, 4)
        qbd = jnp.einsum('bgrqd,gh->bgrqhd', q5, jnp.eye(N_KV, dtype=F32)).reshape(b, N_HEADS * ds, KV_W)
        slope_rows = jnp.repeat(slopes.reshape(N_HEADS), ds)[:, None]
        gate_rows = ng.reshape(b, ds, 3, N_HEADS).transpose(0, 3, 1, 2).reshape(b, N_HEADS * ds, 3)
        o_c, sel_chunks = _sample_cmp(qbd, kc_full, vc_full, slope_rows, past, nc, ns)
        o_rows = _sample_sel(page_table, sample['cache_k_sel'], sample['cache_v_sel'], qbd, sel_chunks,
                             _pad_rows(k_sel, new_pad), _pad_rows(v_sel, new_pad), sample['cache_k_win'],
                             sample['cache_v_win'], _pad_rows(k_win, LANES), _pad_rows(v_win, LANES), o_c,
                             gate_rows, slope_rows, past, l)
        buf_k, buf_v = sample['cache_k_win'][l], sample['cache_v_win'][l]
        o_b = o_rows.reshape(b, N_HEADS, ds, HEAD_DIM).transpose(0, 2, 1, 3).reshape(m, B_WIDTH).astype(BF16)
        keep = min(WINDOW, buf_k.shape[1] + ds)
        new_bk = jnp.concatenate([buf_k, k_win], axis=1)[:, -keep:]
        new_bv = jnp.concatenate([buf_v, v_win], axis=1)[:, -keep:]

    def gate_of(vec):
        return vec if sample is None else jnp.repeat(vec, t, axis=0)

    mix = _branch(o_a, o_b, p['w_branch'], proj, l, tm)
    if sample is None:
        x_mid, h2 = _resid_norm_matmul(mix, p['w_out'], x2, mods[2], p['norm2_w'], mods[3], mods[4], l, min(t, 512))
    else:
        x_mid = _resid_matmul(mix, p['w_out'], x2, gate_of(mods[2]), l, tm, 'attn_out')
        h2 = _modulate(x_mid, p['norm2_w'], mods[3], mods[4], t, l)
    if sample is None:
        s0 = s1 = None
    else:
        state = sample['state_ffn_conv'][l]
        s0 = jnp.repeat(state[:, 0], t, axis=0)
        s1 = jnp.repeat(state[:, 1], t, axis=0)
    hid, tails = _ffn_up(h2, p['ffn_w_a'], p['ffn_w_b'], p['ffn_conv_w'], p['ffn_conv_b'], s0, s1, t, l, tm)
    x_out = _resid_matmul(hid, p['ffn_w_down'], x_mid, gate_of(mods[5]), l, tm, 'ffn_down')

    if sample is None:
        conv_rows = tails.reshape(b, t // tm, SUBLANES, -1)[:, -1, SUBLANES - (CONV_W - 1):]
    else:
        conv_rows = tails.reshape(b, t, -1)[:, t - (CONV_W - 1):]
    heads = lambda z: z.reshape(b, -1, N_KV, HEAD_DIM)
    state_out = dict(k_cmp=heads(k_cmp), v_cmp=heads(v_cmp), k_sel=heads(k_sel), v_sel=heads(v_sel),
                     k_win=heads(new_bk), v_win=heads(new_bv), conv=conv_rows)
    if sample is not None:
        state_out['chunk_v'] = v_norm.reshape(b, t, A_WIDTH)
    return x_out.reshape(b, t, d), state_out


def kernel(x_prompt, x_sample, cache_k_cmp, cache_v_cmp, cache_k_sel, cache_v_sel, cache_k_win, cache_v_win,
           state_ffn_conv, page_table, c_prompt, c_sample, w_ada, b_ada, norm1_w, norm2_w, w_in, a_norm_w,
           a_spatial_w, a_spatial_b, q_norm_w, k_norm_w, cmp_pe_k, cmp_w_k, cmp_pe_v, cmp_w_v, w_branch, w_out,
           ffn_w_a, ffn_w_b, ffn_conv_w, ffn_conv_b, ffn_w_down):
    depth, d, n_in = w_in.shape
    bp = x_prompt.shape[0]
    bs = x_sample.shape[0]

    rows = -(-(bp + bs) // SUBLANES) * SUBLANES
    c_all = jnp.pad(jnp.concatenate([c_prompt, c_sample], axis=0), ((0, rows - bp - bs), (0, 0)))
    mods_all = _ada(c_all, w_ada, b_ada)

    w_in_t = jnp.swapaxes(w_in, 1, 2)
    shared = dict(
        norm1_w=norm1_w, norm2_w=norm2_w, w_in_t=w_in_t,
        qn=jnp.tile(q_norm_w, (1, TN // HEAD_DIM))[:, None, :], kn=jnp.tile(k_norm_w, (1, TN // HEAD_DIM))[:, None, :],
        a_norm_w=a_norm_w, a_spatial_w=a_spatial_w, a_spatial_b=a_spatial_b,
        cmp=[_cmp_consts(cmp_pe_k[l], cmp_w_k[l], cmp_pe_v[l], cmp_w_v[l], k_norm_w[l]) for l in range(depth)],
        w_branch=w_branch, w_out=w_out, ffn_w_a=ffn_w_a, ffn_w_b=ffn_w_b, ffn_conv_w=ffn_conv_w,
        ffn_conv_b=ffn_conv_b, ffn_w_down=ffn_w_down)
    sample = dict(page_table=page_table, cache_k_cmp=cache_k_cmp, cache_v_cmp=cache_v_cmp, cache_k_sel=cache_k_sel,
                  cache_v_sel=cache_v_sel, cache_k_win=cache_k_win, cache_v_win=cache_v_win,
                  state_ffn_conv=state_ffn_conv)
    for name in ('cache_k_cmp', 'cache_v_cmp', 'cache_k_sel', 'cache_v_sel'):
        pool = sample[name]
        sample[name] = pool.transpose(0, 1, 3, 4, 2).reshape(pool.shape[0], pool.shape[1], KV_W, pool.shape[2])
    for name in ('cache_k_win', 'cache_v_win'):
        buf = sample[name]
        sample[name] = buf.reshape(buf.shape[0], buf.shape[1], buf.shape[2], KV_W)

    xp, xs = x_prompt, x_sample
    st_p, st_s = [], []
    for l in range(depth):
        mods = mods_all[l]
        mp = [mods[:bp, k * d:(k + 1) * d] for k in range(6)]
        ms = [mods[bp:bp + bs, k * d:(k + 1) * d] for k in range(6)]
        xp, sp = _layer(l, xp, mp, shared, None)
        xs, ss = _layer(l, xs, ms, shared, sample)
        st_p.append(sp)
        st_s.append(ss)

    def stack(sts, key):
        return jnp.stack([s[key] for s in sts], axis=0)

    return (xp, xs,
            stack(st_p, 'k_cmp'), stack(st_p, 'v_cmp'), stack(st_p, 'k_sel'), stack(st_p, 'v_sel'),
            stack(st_p, 'k_win'), stack(st_p, 'v_win'), stack(st_p, 'conv'),
            stack(st_s, 'k_cmp'), stack(st_s, 'v_cmp'), stack(st_s, 'k_sel'), stack(st_s, 'v_sel'),
            stack(st_s, 'k_win'), stack(st_s, 'v_win'), stack(st_s, 'chunk_v'), stack(st_s, 'conv'))
```
